```python
import math
import jax, jax.numpy as jnp
from jax import lax
import numpy as np

D_MODEL = 1024
BATCH = 2
SEQ = 16384
DEPTH = 2

GRID_W = 64
CTX_LEN = 256

DN_HEADS = 4
DN_HEAD_DIM = 128
DN_WIDTH = DN_HEADS * DN_HEAD_DIM
DN_CONV = 3
DN_CHUNK = 64
DN_DIRS = 2
NA_HEADS = 4
NA_HEAD_DIM = 64
NA_WIDTH = NA_HEADS * NA_HEAD_DIM
NA_WIN_H = 8
NA_WIN_W = 16
NA_QBLOCK = 128
FT_GROUPS = 4
FT_GROUP_DIM = 64
FT_WIDTH = FT_GROUPS * FT_GROUP_DIM
MIX_WIDTH = DN_WIDTH + NA_WIDTH + FT_WIDTH
IN_SPLITS = (3 * DN_WIDTH, DN_WIDTH, DN_DIRS * DN_HEADS, DN_DIRS * DN_HEADS,
             NA_WIDTH, NA_WIDTH, NA_WIDTH, FT_WIDTH)
IN_WIDTH = sum(IN_SPLITS)
D_FF = 2816
N_EXPERTS = 8
TOP_K = 2
D_FF_EXPERT = 3584
MOE_BLOCK = 256
N_DENSE = (DEPTH + 1) // 2
N_MOE = DEPTH // 2
N_MOD = 6
EPS = 1e-6

kernel_name = "hybrid_deltanet_natten_fnet_moe_dit"


def rms_norm(x, g):
    xf = x.astype(jnp.float32)
    y = xf * lax.rsqrt(jnp.mean(xf * xf, axis=-1, keepdims=True) + EPS)
    return y.astype(x.dtype) * g


def l2_norm(x):
    return x * lax.rsqrt(jnp.sum(x * x, axis=-1, keepdims=True) + EPS)


def centred_depthwise_conv(x, w):
    k_size = w.shape[0]
    pad = k_size // 2
    length = x.shape[1]
    xp = jnp.pad(x, ((0, 0), (pad, k_size - 1 - pad), (0, 0)))
    out = xp[:, :length] * w[0]
    for j in range(1, k_size):
        out = out + xp[:, j:j + length] * w[j]
    return out


def gated_delta_rule(q, k, v, g, beta, s0):
    bsz, length, heads, _ = q.shape
    dv = v.shape[-1]
    n_chunks = length // DN_CHUNK

    def to_chunks(t):
        t = t.reshape((bsz, n_chunks, DN_CHUNK, heads) + t.shape[3:])
        return jnp.moveaxis(t, (1, 3), (0, 2))

    qc, kc, vc = to_chunks(q), to_chunks(k), to_chunks(v)
    gc = jnp.cumsum(to_chunks(g), axis=-1)
    bc = to_chunks(beta)
    incl = jnp.tril(jnp.ones((DN_CHUNK, DN_CHUNK), bool))
    strict = jnp.tril(jnp.ones((DN_CHUNK, DN_CHUNK), bool), -1)
    diff = gc[..., :, None] - gc[..., None, :]
    decay = jnp.where(incl, jnp.exp(jnp.where(incl, diff, 0.0)), 0.0)
    kb = kc * bc[..., None]
    a_mat = jnp.where(strict, jnp.einsum('nbhid,nbhjd->nbhij', kb, kc) * decay, 0.0)
    eye = jnp.eye(DN_CHUNK, dtype=q.dtype)
    t_mat = lax.linalg.triangular_solve(eye + a_mat, jnp.broadcast_to(eye, a_mat.shape),
                                        left_side=True, lower=True, unit_diagonal=True)
    u = t_mat @ (vc * bc[..., None])
    w = t_mat @ (kb * jnp.exp(gc)[..., None])
    qk = jnp.einsum('nbhid,nbhjd->nbhij', qc, kc) * decay

    def step(state, inp):
        q_i, k_i, u_i, w_i, g_i, qk_i = inp
        v_new = u_i - w_i @ state
        o_i = (q_i * jnp.exp(g_i)[..., None]) @ state + qk_i @ v_new
        g_last = g_i[..., -1:]
        state = state * jnp.exp(g_last)[..., None] + jnp.einsum(
            'bhcd,bhce->bhde', k_i * jnp.exp(g_last - g_i)[..., None], v_new)
        return state, o_i

    s_fin, o = lax.scan(step, s0, (qc, kc, u, w, gc, qk))
    o = jnp.moveaxis(o, (0, 2), (1, 3)).reshape(bsz, length, heads, dv)
    return o, s_fin


def deltanet_branch(qkv, z, alpha, beta_logit, conv_w, a_log, dt_bias, norm_w, s0_fwd, s0_bwd):
    bsz, length, _ = qkv.shape
    out_dtype = qkv.dtype
    qkv = jax.nn.silu(centred_depthwise_conv(qkv, conv_w)).astype(jnp.float32)
    q, k, v = jnp.split(qkv, 3, axis=-1)
    heads = lambda t: t.reshape(bsz, length, DN_HEADS, DN_HEAD_DIM)
    q = l2_norm(heads(q)) * (DN_HEAD_DIM ** -0.5)
    k = l2_norm(heads(k))
    v = heads(v)
    a = alpha.astype(jnp.float32).reshape(bsz, length, DN_DIRS, DN_HEADS)
    g = -jnp.exp(a_log.astype(jnp.float32)) * jax.nn.softplus(a + dt_bias.astype(jnp.float32))
    beta = jax.nn.sigmoid(beta_logit.astype(jnp.float32).reshape(bsz, length, DN_DIRS, DN_HEADS))
    o_f, s_f = gated_delta_rule(q, k, v, g[:, :, 0], beta[:, :, 0], s0_fwd)
    flip = lambda t: jnp.flip(t, axis=1)
    o_b, s_b = gated_delta_rule(flip(q), flip(k), flip(v), flip(g[:, :, 1]), flip(beta[:, :, 1]), s0_bwd)
    o = o_f + flip(o_b)
    o = rms_norm(o, norm_w.astype(jnp.float32)) * jax.nn.silu(heads(z).astype(jnp.float32))
    return o.reshape(bsz, length, DN_WIDTH).astype(out_dtype), s_f, s_b


def neighbourhood_attention(q, k, v, kc, vc, rpb):
    bsz, length, heads, hd = q.shape
    rows = length // GRID_W
    kh = min(NA_WIN_H, rows)
    kw = NA_WIN_W
    n_keys = kh * kw
    pos = jnp.arange(length)
    r = pos // GRID_W
    col = pos % GRID_W
    r0 = jnp.clip(r - kh // 2, 0, rows - kh)
    c0 = jnp.clip(col - kw // 2, 0, GRID_W - kw)
    key_r = r0[:, None, None] + jnp.arange(kh)[None, :, None]
    key_c = c0[:, None, None] + jnp.arange(kw)[None, None, :]
    key_idx = (key_r * GRID_W + key_c).reshape(length, n_keys)
    off_r = jnp.broadcast_to(key_r - r[:, None, None] + NA_WIN_H - 1, (length, kh, kw)).reshape(length, n_keys)
    off_c = jnp.broadcast_to(key_c - col[:, None, None] + NA_WIN_W - 1, (length, kh, kw)).reshape(length, n_keys)
    n_blocks = length // NA_QBLOCK
    scale = hd ** -0.5

    def block(args):
        qb, idx, o_r, o_c = args
        kg = k[:, idx]
        vg = v[:, idx]
        s_loc = jnp.einsum('bqhd,bqkhd->bhqk', qb, kg) * scale + rpb[:, o_r, o_c][None]
        s_ctx = jnp.einsum('bqhd,bchd->bhqc', qb, kc) * scale
        p = jax.nn.softmax(jnp.concatenate([s_loc, s_ctx], axis=-1).astype(jnp.float32), axis=-1).astype(v.dtype)
        return (jnp.einsum('bhqk,bqkhd->bqhd', p[..., :n_keys], vg)
                + jnp.einsum('bhqc,bchd->bqhd', p[..., n_keys:], vc))

    qs = jnp.moveaxis(q.reshape(bsz, n_blocks, NA_QBLOCK, heads, hd), 1, 0)
    o = lax.map(block, (qs,
                        key_idx.reshape(n_blocks, NA_QBLOCK, n_keys),
                        off_r.reshape(n_blocks, NA_QBLOCK, n_keys),
                        off_c.reshape(n_blocks, NA_QBLOCK, n_keys)))
    return jnp.moveaxis(o, 0, 1).reshape(bsz, length, heads * hd)


def context_attention(q, k, v):
    bsz, length, heads, hd = q.shape
    s = jnp.einsum('bqhd,bkhd->bhqk', q, k) * (hd ** -0.5)
    p = jax.nn.softmax(s.astype(jnp.float32), axis=-1).astype(v.dtype)
    return jnp.einsum('bhqk,bkhd->bqhd', p, v).reshape(bsz, length, heads * hd)


def fourier_mix(u, w):
    bsz, length, _ = u.shape
    ug = u.astype(jnp.float32).reshape(bsz, length, FT_GROUPS, FT_GROUP_DIM)
    f = jnp.fft.fftn(ug, axes=(1, 3), norm="ortho").real
    return f.reshape(bsz, length, FT_WIDTH).astype(u.dtype) @ w


def token_mixers(h, hc, need_ctx, w_in, conv_w, a_log, dt_bias, dn_norm_w, q_norm, k_norm, rpb, fno_w, w_out):
    offs = [int(o) for o in np.cumsum(IN_SPLITS)[:-1]]
    qkv, z, al, be, nq, nk, nv, ft = jnp.split(h @ w_in, offs, axis=-1)
    qkv_c, z_c, al_c, be_c, nq_c, nk_c, nv_c, ft_c = jnp.split(hc @ w_in, offs, axis=-1)
    bsz = h.shape[0]
    zero = jnp.zeros((bsz, DN_HEADS, DN_HEAD_DIM, DN_HEAD_DIM), jnp.float32)
    oa_c, s_f, s_b = deltanet_branch(qkv_c, z_c, al_c, be_c, conv_w, a_log, dt_bias, dn_norm_w, zero, zero)
    oa, _, _ = deltanet_branch(qkv, z, al, be, conv_w, a_log, dt_bias, dn_norm_w, s_f, s_b)
    heads = lambda t: t.reshape(t.shape[0], t.shape[1], NA_HEADS, NA_HEAD_DIM)
    q = rms_norm(heads(nq), q_norm)
    k = rms_norm(heads(nk), k_norm)
    v = heads(nv)
    k_c = rms_norm(heads(nk_c), k_norm)
    v_c = heads(nv_c)
    ob = neighbourhood_attention(q, k, v, k_c, v_c, rpb)
    oc = fourier_mix(ft, fno_w)
    y = jnp.concatenate([oa, ob, oc], axis=-1) @ w_out
    if not need_ctx:
        return y, None
    ob_c = context_attention(rms_norm(heads(nq_c), q_norm), k_c, v_c)
    oc_c = fourier_mix(ft_c, fno_w)
    yc = jnp.concatenate([oa_c, ob_c, oc_c], axis=-1) @ w_out
    return y, yc


def swiglu(h, w1, w3, w2):
    return (jax.nn.silu(h @ w1) * (h @ w3)) @ w2


def moe_swiglu(h, w_router, w1, w3, w2):
    bsz, length, d = h.shape
    t = h.reshape(-1, d)
    n_tok = t.shape[0]
    logits = (t @ w_router).astype(jnp.float32)
    top_l, top_e = lax.top_k(logits, TOP_K)
    top_w = jax.nn.softmax(top_l, axis=-1)
    n_assign = n_tok * TOP_K
    e_flat = top_e.reshape(n_assign)
    tok_flat = jnp.arange(n_assign) // TOP_K
    w_flat = top_w.reshape(n_assign)
    order = jnp.argsort(e_flat)
    e_s, tok_s, w_s = e_flat[order], tok_flat[order], w_flat[order]
    counts = jnp.bincount(e_flat, length=N_EXPERTS)
    padded = (counts + MOE_BLOCK - 1) // MOE_BLOCK * MOE_BLOCK
    pad_end = jnp.cumsum(padded)
    pad_start = pad_end - padded
    start = jnp.cumsum(counts) - counts
    dest = pad_start[e_s] + (jnp.arange(n_assign) - start[e_s])
    n_blocks = -(-n_assign // MOE_BLOCK) + N_EXPERTS
    n_slots = n_blocks * MOE_BLOCK
    slot_tok = jnp.zeros((n_slots,), jnp.int32).at[dest].set(tok_s.astype(jnp.int32))
    slot_w = jnp.zeros((n_slots,), t.dtype).at[dest].set(w_s.astype(t.dtype))
    block_e = jnp.minimum(jnp.searchsorted(pad_end, jnp.arange(n_blocks) * MOE_BLOCK, side='right'), N_EXPERTS - 1)
    xb = t[slot_tok].reshape(n_blocks, MOE_BLOCK, d)

    def expert_block(args):
        x_blk, e = args
        return swiglu(x_blk, w1[e], w3[e], w2[e])

    yb = lax.map(expert_block, (xb, block_e)).reshape(n_slots, d)
    out = jax.ops.segment_sum(yb * slot_w[:, None], slot_tok, num_segments=n_tok)
    return out.reshape(bsz, length, d)


def channel_mixer(h, layer, ffn_w1, ffn_w3, ffn_w2, moe_router, moe_w1, moe_w3, moe_w2):
    j = layer // 2
    if layer % 2 == 0:
        return swiglu(h, ffn_w1[j], ffn_w3[j], ffn_w2[j])
    return moe_swiglu(h, moe_router[j], moe_w1[j], moe_w3[j], moe_w2[j])


def setup_inputs(seed: int = 0) -> dict:
    key = jax.random.key(seed)
    ks = jax.random.split(key, 32)
    f32 = jnp.float32
    D = D_MODEL
    nrm = lambda k, shape, s: jax.random.normal(k, shape, f32) * s
    dt = jnp.exp(jax.random.uniform(ks[11], (DEPTH, DN_DIRS, DN_HEADS), f32, math.log(1e-3), math.log(1e-1)))
    return {
        "x": nrm(ks[0], (BATCH, SEQ, D), 1.0),
        "c": nrm(ks[1], (BATCH, D), 1.0),
        "ctx": nrm(ks[2], (BATCH, CTX_LEN, D), 1.0),
        "c_ctx": nrm(ks[3], (D,), 1.0),
        "w_mod": nrm(ks[4], (DEPTH, D, N_MOD * D), D ** -0.5),
        "b_mod": nrm(ks[5], (DEPTH, N_MOD * D), 0.01),
        "g_mix": 1.0 + nrm(ks[6], (DEPTH, D), 0.05),
        "g_ffn": 1.0 + nrm(ks[7], (DEPTH, D), 0.05),
        "w_in": nrm(ks[8], (DEPTH, D, IN_WIDTH), D ** -0.5),
        "dn_conv": nrm(ks[9], (DEPTH, DN_CONV, 3 * DN_WIDTH), DN_CONV ** -0.5),
        "dn_a_log": jnp.log(jax.random.uniform(ks[10], (DEPTH, DN_DIRS, DN_HEADS), f32, 1.0, 16.0)),
        "dn_dt_bias": dt + jnp.log(-jnp.expm1(-dt)),
        "dn_norm_w": 1.0 + nrm(ks[12], (DEPTH, DN_HEAD_DIM), 0.05),
        "na_q_norm": 1.0 + nrm(ks[13], (DEPTH, NA_HEAD_DIM), 0.05),
        "na_k_norm": 1.0 + nrm(ks[14], (DEPTH, NA_HEAD_DIM), 0.05),
        "na_rpb": nrm(ks[15], (DEPTH, NA_HEADS, 2 * NA_WIN_H - 1, 2 * NA_WIN_W - 1), 0.1),
        "fno_w": nrm(ks[16], (DEPTH, FT_WIDTH, FT_WIDTH), FT_WIDTH ** -0.5),
        "w_out": nrm(ks[17], (DEPTH, MIX_WIDTH, D), MIX_WIDTH ** -0.5),
        "ffn_w1": nrm(ks[18], (N_DENSE, D, D_FF), D ** -0.5),
        "ffn_w3": nrm(ks[19], (N_DENSE, D, D_FF), D ** -0.5),
        "ffn_w2": nrm(ks[20], (N_DENSE, D_FF, D), D_FF ** -0.5),
        "moe_router": nrm(ks[21], (N_MOE, D, N_EXPERTS), D ** -0.5),
        "moe_w1": nrm(ks[22], (N_MOE, N_EXPERTS, D, D_FF_EXPERT), D ** -0.5),
        "moe_w3": nrm(ks[23], (N_MOE, N_EXPERTS, D, D_FF_EXPERT), D ** -0.5),
        "moe_w2": nrm(ks[24], (N_MOE, N_EXPERTS, D_FF_EXPERT, D), D_FF_EXPERT ** -0.5),
    }


def reference(x, c, ctx, c_ctx, w_mod, b_mod, g_mix, g_ffn, w_in, dn_conv, dn_a_log, dn_dt_bias, dn_norm_w,
              na_q_norm, na_k_norm, na_rpb, fno_w, w_out, ffn_w1, ffn_w3, ffn_w2,
              moe_router, moe_w1, moe_w3, moe_w2):
    s_lat = jax.nn.silu(c)
    s_ctx = jax.nn.silu(c_ctx)
    for layer in range(DEPTH):
        need_ctx = layer < DEPTH - 1
        mod = jnp.split((s_lat @ w_mod[layer] + b_mod[layer])[:, None, :], N_MOD, axis=-1)
        modc = jnp.split(s_ctx @ w_mod[layer] + b_mod[layer], N_MOD, axis=-1)
        h = rms_norm(x, g_mix[layer]) * (1.0 + mod[1]) + mod[0]
        hc = rms_norm(ctx, g_mix[layer]) * (1.0 + modc[1]) + modc[0]
        y, yc = token_mixers(h, hc, need_ctx, w_in[layer], dn_conv[layer], dn_a_log[layer], dn_dt_bias[layer],
                             dn_norm_w[layer], na_q_norm[layer], na_k_norm[layer], na_rpb[layer],
                             fno_w[layer], w_out[layer])
        x = x + mod[2] * y
        h2 = rms_norm(x, g_ffn[layer]) * (1.0 + mod[4]) + mod[3]
        x = x + mod[5] * channel_mixer(h2, layer, ffn_w1, ffn_w3, ffn_w2, moe_router, moe_w1, moe_w3, moe_w2)
        if need_ctx:
            ctx = ctx + modc[2] * yc
            hc2 = rms_norm(ctx, g_ffn[layer]) * (1.0 + modc[4]) + modc[3]
            ctx = ctx + modc[5] * channel_mixer(hc2, layer, ffn_w1, ffn_w3, ffn_w2,
                                                moe_router, moe_w1, moe_w3, moe_w2)
    return x
```

```python
import functools
import math

import jax
import jax.numpy as jnp
import numpy as np
from jax import lax
from jax.experimental import pallas as pl
from jax.experimental.pallas import tpu as pltpu

D_MODEL = 1024
DEPTH = 2
GRID_W = 64
DN_HEADS = 4
DN_HEAD_DIM = 128
DN_WIDTH = DN_HEADS * DN_HEAD_DIM
DN_CONV = 3
DN_CHUNK = 64
DN_DIRS = 2
NA_HEADS = 4
NA_HEAD_DIM = 64
NA_WIDTH = NA_HEADS * NA_HEAD_DIM
NA_WIN_H = 8
NA_WIN_W = 16
FT_GROUPS = 4
FT_GROUP_DIM = 64
FT_WIDTH = FT_GROUPS * FT_GROUP_DIM
N_EXPERTS = 8
TOP_K = 2
N_MOD = 6
EPS = 1e-6

SUBLANES = 8
LANES = 128
VMEM_LIMIT = 48 * 1024 * 1024

BF16 = jnp.bfloat16
F32 = jnp.float32
N_DH = DN_DIRS * DN_HEADS


def _cparams(*sem):
    return pltpu.CompilerParams(dimension_semantics=sem, vmem_limit_bytes=VMEM_LIMIT)


def _silu(x):
    return x * jax.nn.sigmoid(x)


def _dot(a, b):
    return jnp.dot(a, b, preferred_element_type=F32)


def _dot_nt(a, b):
    return lax.dot_general(a, b, (((1,), (1,)), ((), ())), preferred_element_type=F32)


def _dot_tn(a, b):
    return lax.dot_general(a, b, (((0,), (0,)), ((), ())), preferred_element_type=F32)


def _group_ones(width, group):
    idx = np.arange(width) // group
    return jnp.asarray((idx[:, None] == idx[None, :]).astype(np.float32), BF16)


def _mod_kernel(c_ref, w_ref, b_ref, o_ref):
    s = _silu(c_ref[...])
    o_ref[0] = jnp.dot(s, w_ref[0], preferred_element_type=F32, precision=lax.Precision.HIGHEST) + b_ref[0]


def adaln_mod(cond_rows, w_mod, b_mod, *, tn=1536):
    depth, d, n = w_mod.shape
    rows = cond_rows.shape[0]
    return pl.pallas_call(
        _mod_kernel,
        out_shape=jax.ShapeDtypeStruct((depth, rows, n), F32),
        grid=(depth, n // tn),
        in_specs=[pl.BlockSpec((rows, d), lambda l, j: (0, 0)),
                  pl.BlockSpec((1, d, tn), lambda l, j: (l, 0, j)),
                  pl.BlockSpec((1, 1, tn), lambda l, j: (l, 0, j))],
        out_specs=pl.BlockSpec((1, rows, tn), lambda l, j: (l, 0, j)),
        compiler_params=_cparams("arbitrary", "arbitrary"),
        name="adaln_mod",
    )(cond_rows, w_mod, b_mod.reshape(depth, 1, n))


C_QKV = 0
C_Z = 3 * DN_WIDTH
C_NQ = C_Z + DN_WIDTH
C_NK = C_NQ + NA_WIDTH
C_NV = C_NK + NA_WIDTH
C_FT = C_NV + NA_WIDTH
C_END = C_FT + FT_WIDTH


def _modulated_norm(x, g, shift, scale):
    y = x * lax.rsqrt(jnp.mean(x * x, axis=-1, keepdims=True) + EPS)
    return y * g * (1.0 + scale) + shift


def _head_rms(x, ones, width, gain):
    ss = _dot((x * x).astype(BF16), ones)
    return x * lax.rsqrt(ss * (1.0 / width) + EPS) * gain


def _in_proj_kernel(rows_per_mod, fixed_row, x_ref, mod_ref, g_ref, w_ref, wab_ref, wabt_ref, qn_ref, kn_ref,
                    ones_ref, qkv_ref, z_ref, ab_ref, abt_ref, nq_ref, nk_ref, nv_ref, ft_ref):
    tm = x_ref.shape[0]
    row = fixed_row if fixed_row is not None else (pl.program_id(0) * tm) // rows_per_mod
    m = mod_ref[pl.ds(row, 1), :]
    shift, scale = m[:, 0:D_MODEL], m[:, D_MODEL:2 * D_MODEL]
    h = _modulated_norm(x_ref[...], g_ref[...], shift, scale).astype(BF16)
    qkv_ref[...] = _dot(h, w_ref[:, C_QKV:C_Z]).astype(qkv_ref.dtype)
    z_ref[...] = _dot(h, w_ref[:, C_Z:C_NQ]).astype(z_ref.dtype)
    ones = ones_ref[...]
    nq = _dot(h, w_ref[:, C_NQ:C_NK])
    nq_ref[...] = (_head_rms(nq, ones, NA_HEAD_DIM, qn_ref[...]) * (NA_HEAD_DIM ** -0.5)).astype(nq_ref.dtype)
    nk = _dot(h, w_ref[:, C_NK:C_NV])
    nk_ref[...] = _head_rms(nk, ones, NA_HEAD_DIM, kn_ref[...]).astype(nk_ref.dtype)
    nv_ref[...] = _dot(h, w_ref[:, C_NV:C_FT]).astype(nv_ref.dtype)
    ft_ref[...] = _dot(h, w_ref[:, C_FT:C_END]).astype(ft_ref.dtype)
    ab_ref[...] = _dot(h, wab_ref[...])
    abt_ref[...] = _dot_nt(wabt_ref[...], h)


def in_proj(x2d, mod, g, w_main, w_ab, w_abt, q_norm, k_norm, *, rows_per_mod, fixed_row=None, tm=512):
    n, d = x2d.shape
    tm = min(tm, n)
    nab = w_ab.shape[1]
    full = lambda shape: pl.BlockSpec(shape, lambda i: (0,) * len(shape))
    rowblk = lambda w: pl.BlockSpec((tm, w), lambda i: (i, 0))
    outs = [jax.ShapeDtypeStruct((n, 3 * DN_WIDTH), BF16), jax.ShapeDtypeStruct((n, DN_WIDTH), BF16),
            jax.ShapeDtypeStruct((n, nab), F32), jax.ShapeDtypeStruct((nab, n), F32),
            jax.ShapeDtypeStruct((n, NA_WIDTH), BF16), jax.ShapeDtypeStruct((n, NA_WIDTH), BF16),
            jax.ShapeDtypeStruct((n, NA_WIDTH), BF16), jax.ShapeDtypeStruct((n, FT_WIDTH), BF16)]
    return pl.pallas_call(
        functools.partial(_in_proj_kernel, rows_per_mod, fixed_row),
        out_shape=outs,
        grid=(n // tm,),
        in_specs=[rowblk(d), full(mod.shape), full((1, d)), full(w_main.shape), full(w_ab.shape), full(w_abt.shape),
                  full((1, NA_WIDTH)), full((1, NA_WIDTH)), full((NA_WIDTH, NA_WIDTH))],
        out_specs=[rowblk(3 * DN_WIDTH), rowblk(DN_WIDTH), rowblk(nab), pl.BlockSpec((nab, tm), lambda i: (0, i)),
                   rowblk(NA_WIDTH), rowblk(NA_WIDTH), rowblk(NA_WIDTH), rowblk(FT_WIDTH)],
        compiler_params=_cparams("parallel"),
        name="in_proj",
    )(x2d, mod, g.reshape(1, d), w_main, w_ab, w_abt,
      jnp.tile(q_norm, NA_HEADS).reshape(1, NA_WIDTH), jnp.tile(k_norm, NA_HEADS).reshape(1, NA_WIDTH),
      _group_ones(NA_WIDTH, NA_HEAD_DIM))


QKV_HALO = 16


def _softplus(x):
    return jnp.maximum(x, 0.0) + jnp.log1p(jnp.exp(-jnp.abs(x)))


def _tri(n, lower):
    r = lax.broadcasted_iota(jnp.int32, (n, n), 0)
    c = lax.broadcasted_iota(jnp.int32, (n, n), 1)
    return jnp.where((r >= c) if lower else (r <= c), 1.0, 0.0).astype(F32)


def _dot_hi(a, b):
    return jnp.dot(a, b, preferred_element_type=F32, precision=lax.Precision.HIGHEST)


def _dn_prep_kernel(x_ref, xp_ref, xn_ref, ab_ref, abt_ref, cw_ref, alr_ref, dtr_ref, alc_ref, dtc_ref, ones_ref,
                    q_ref, k_ref, v_ref, gcc_ref, beta_ref, gcr_ref):
    i = pl.program_id(1)
    tl = x_ref.shape[0]
    x = x_ref[...].astype(F32)
    prev = jnp.where(i > 0, xp_ref[QKV_HALO - 1:QKV_HALO, :].astype(F32), 0.0)
    nxt = jnp.where(i < pl.num_programs(1) - 1, xn_ref[0:1, :].astype(F32), 0.0)
    rows = lax.broadcasted_iota(jnp.int32, x.shape, 0)
    xm1 = jnp.where(rows == 0, prev, pltpu.roll(x, 1, axis=0))
    xp1 = jnp.where(rows == tl - 1, nxt, pltpu.roll(x, tl - 1, axis=0))
    y = _silu(xm1 * cw_ref[0:1, :] + x * cw_ref[1:2, :] + xp1 * cw_ref[2:3, :])
    ones = ones_ref[...]
    q = y[:, 0:DN_WIDTH]
    k = y[:, DN_WIDTH:2 * DN_WIDTH]
    q_ref[...] = (q * lax.rsqrt(_dot((q * q).astype(BF16), ones) + EPS) * (DN_HEAD_DIM ** -0.5)).astype(q_ref.dtype)
    k_ref[...] = (k * lax.rsqrt(_dot((k * k).astype(BF16), ones) + EPS)).astype(k_ref.dtype)
    v_ref[...] = y[:, 2 * DN_WIDTH:].astype(v_ref.dtype)

    ab = ab_ref[...]
    g_col = -jnp.exp(alr_ref[...]) * _softplus(ab[:, 0:N_DH] + dtr_ref[...])
    beta = jax.nn.sigmoid(ab[:, N_DH:2 * N_DH])
    beta_ref[0] = beta[:, 0:DN_HEADS]
    beta_ref[1] = beta[:, DN_HEADS:]
    g_row = -jnp.exp(alc_ref[...]) * _softplus(abt_ref[0:N_DH, :] + dtc_ref[...])
    lo, up = _tri(DN_CHUNK, True), _tri(DN_CHUNK, False)
    sub = lax.broadcasted_iota(jnp.int32, (N_DH, DN_CHUNK), 0)
    for c in range(tl // DN_CHUNK):
        sl = slice(c * DN_CHUNK, (c + 1) * DN_CHUNK)
        gc = g_col[sl, :]
        gcc_ref[0, sl, :] = _dot_hi(lo, gc)[:, 0:DN_HEADS]
        gcc_ref[1, sl, :] = _dot_hi(up, gc)[:, DN_HEADS:]
        gr = g_row[:, sl]
        cs = jnp.where(sub < DN_HEADS, _dot_hi(gr, up), _dot_hi(gr, lo))
        gcr_ref[0, 0, c] = cs[0:DN_HEADS]
        gcr_ref[1, 0, c] = cs[DN_HEADS:]


def dn_prep(qkv, ab, abt, conv_w, a_log, dt_bias, *, batch, tl=512):
    n = qkv.shape[0]
    seq = n // batch
    tl = min(tl, seq)
    nblk = seq // tl
    hb = tl // QKV_HALO
    nc_all = seq // DN_CHUNK
    full = lambda shape: pl.BlockSpec(shape, lambda b, i: (0,) * len(shape))
    rowblk = lambda w: pl.BlockSpec((tl, w), lambda b, i: (b * nblk + i, 0))
    outs = [jax.ShapeDtypeStruct((n, DN_WIDTH), BF16)] * 3 + [
        jax.ShapeDtypeStruct((DN_DIRS, n, DN_HEADS), F32), jax.ShapeDtypeStruct((DN_DIRS, n, DN_HEADS), F32),
        jax.ShapeDtypeStruct((DN_DIRS, batch, nc_all, DN_HEADS, DN_CHUNK), F32)]
    a_log = a_log.reshape(N_DH)
    dt_bias = dt_bias.reshape(N_DH)
    return pl.pallas_call(
        _dn_prep_kernel,
        out_shape=outs,
        grid=(batch, nblk),
        in_specs=[rowblk(3 * DN_WIDTH),
                  pl.BlockSpec((QKV_HALO, 3 * DN_WIDTH), lambda b, i: (jnp.maximum((b * nblk + i) * hb - 1, 0), 0)),
                  pl.BlockSpec((QKV_HALO, 3 * DN_WIDTH),
                               lambda b, i: (jnp.minimum((b * nblk + i + 1) * hb, n // QKV_HALO - 1), 0)),
                  rowblk(2 * N_DH),
                  pl.BlockSpec((2 * N_DH, tl), lambda b, i: (0, b * nblk + i)),
                  full((DN_CONV, 3 * DN_WIDTH)), full((1, N_DH)), full((1, N_DH)), full((N_DH, 1)), full((N_DH, 1)),
                  full((DN_WIDTH, DN_WIDTH))],
        out_specs=[rowblk(DN_WIDTH)] * 3 + [
            pl.BlockSpec((DN_DIRS, tl, DN_HEADS), lambda b, i: (0, b * nblk + i, 0)),
            pl.BlockSpec((DN_DIRS, tl, DN_HEADS), lambda b, i: (0, b * nblk + i, 0)),
            pl.BlockSpec((DN_DIRS, 1, tl // DN_CHUNK, DN_HEADS, DN_CHUNK), lambda b, i: (0, b, i, 0, 0))],
        compiler_params=_cparams("parallel", "parallel"),
        name="dn_prep",
    )(qkv, qkv, qkv, ab, abt, conv_w, a_log.reshape(1, N_DH), dt_bias.reshape(1, N_DH),
      a_log.reshape(N_DH, 1), dt_bias.reshape(N_DH, 1), _group_ones(DN_WIDTH, DN_HEAD_DIM))


def _unit_tri_inverse(a):
    n = a.shape[0]
    r = lax.broadcasted_iota(jnp.int32, (n, n), 0)
    c = lax.broadcasted_iota(jnp.int32, (n, n), 1)
    p = jnp.where(r == c, 1.0, 0.0) - a
    apow = a
    for _ in range(int(math.log2(n)) - 1):
        apow = _dot_hi(apow, apow)
        p = p + _dot_hi(p, apow)
    return p


def _dn_scan_kernel(q_ref, k_ref, v_ref, gcc_ref, beta_ref, gcr_ref, s0_ref, o_ref, s_ref,
                    wq_scr, u_scr, qk_scr, kd_scr, eg_scr):
    d = pl.program_id(1)
    i = pl.program_id(2)
    nc = wq_scr.shape[0]
    cs = DN_CHUNK
    hd = DN_HEAD_DIM
    fwd = d == 0

    @pl.when(i == 0)
    def _():
        s_ref[...] = s0_ref[...]

    r = lax.broadcasted_iota(jnp.int32, (cs, cs), 0)
    c = lax.broadcasted_iota(jnp.int32, (cs, cs), 1)
    rel = (r - c) * jnp.where(fwd, 1, -1)
    incl = rel >= 0
    strict = rel > 0

    def local(ci, carry):
        row0 = pl.multiple_of(ci * cs, cs)
        grows = gcr_ref[0, 0, ci]
        for h in range(DN_HEADS):
            lanes = slice(h * hd, (h + 1) * hd)
            q = q_ref[pl.ds(row0, cs), lanes]
            k = k_ref[pl.ds(row0, cs), lanes]
            v = v_ref[pl.ds(row0, cs), lanes].astype(F32)
            gcol = gcc_ref[0, pl.ds(row0, cs), h:h + 1]
            bcol = beta_ref[0, pl.ds(row0, cs), h:h + 1]
            diff = gcol - grows[h:h + 1, :]
            decay = jnp.where(incl, jnp.exp(jnp.where(incl, diff, 0.0)), 0.0)
            kf = k.astype(F32)
            kb = kf * bcol
            a = jnp.where(strict, _dot_nt(kb.astype(BF16), k) * decay, 0.0)
            t = _unit_tri_inverse(a)
            eg = jnp.exp(gcol)
            u = _dot_hi(t, v * bcol)
            w = _dot_hi(t, kb * eg)
            qk = jnp.where(incl, _dot_nt(q, k) * decay, 0.0)
            glast = jnp.where(fwd, gcol[cs - 1:cs, :], gcol[0:1, :])
            wq_scr[ci, h, 0:cs, :] = w.astype(BF16)
            wq_scr[ci, h, cs:2 * cs, :] = (q.astype(F32) * eg).astype(BF16)
            u_scr[ci, h] = u
            qk_scr[ci, h] = qk.astype(BF16)
            kd_scr[ci, h] = (kf * jnp.exp(glast - gcol)).astype(BF16)
            eg_scr[ci, h] = jnp.broadcast_to(jnp.exp(glast), (1, hd))
        return carry

    lax.fori_loop(0, nc, local, 0)

    def step(cc, carry):
        ci = jnp.where(fwd, cc, nc - 1 - cc)
        row0 = pl.multiple_of(ci * cs, cs)
        for h in range(DN_HEADS):
            s = s_ref[0, 0, h]
            rs = _dot(wq_scr[ci, h], s.astype(BF16))
            v_new = (u_scr[ci, h] - rs[0:cs]).astype(BF16)
            o = rs[cs:2 * cs] + _dot(qk_scr[ci, h], v_new)
            s_ref[0, 0, h] = s * eg_scr[ci, h] + _dot_tn(kd_scr[ci, h], v_new)
            o_ref[0, pl.ds(row0, cs), h * hd:(h + 1) * hd] = o.astype(o_ref.dtype)
        return carry

    lax.fori_loop(0, nc, step, 0)


def dn_scan(q, k, v, gcc, beta, gcr, s0, *, batch, tl=512):
    n = q.shape[0]
    seq = n // batch
    tl = min(tl, seq)
    nblk = seq // tl
    nc = tl // DN_CHUNK
    blk = lambda b, d, i: b * nblk + jnp.where(d == 0, i, nblk - 1 - i)
    chunk_blk = lambda d, i: jnp.where(d == 0, i, nblk - 1 - i)
    rowspec = pl.BlockSpec((tl, DN_WIDTH), lambda b, d, i: (blk(b, d, i), 0))
    gate = pl.BlockSpec((1, tl, DN_HEADS), lambda b, d, i: (d, blk(b, d, i), 0))
    state = pl.BlockSpec((1, 1, DN_HEADS, DN_HEAD_DIM, DN_HEAD_DIM), lambda b, d, i: (b, d, 0, 0, 0))
    return pl.pallas_call(
        _dn_scan_kernel,
        out_shape=[jax.ShapeDtypeStruct((DN_DIRS, n, DN_WIDTH), BF16), jax.ShapeDtypeStruct(s0.shape, F32)],
        grid=(batch, DN_DIRS, nblk),
        in_specs=[rowspec, rowspec, rowspec, gate, gate,
                  pl.BlockSpec((1, 1, nc, DN_HEADS, DN_CHUNK), lambda b, d, i: (d, b, chunk_blk(d, i), 0, 0)),
                  state],
        out_specs=[pl.BlockSpec((1, tl, DN_WIDTH), lambda b, d, i: (d, blk(b, d, i), 0)), state],
        scratch_shapes=[pltpu.VMEM((nc, DN_HEADS, 2 * DN_CHUNK, DN_HEAD_DIM), BF16),
                        pltpu.VMEM((nc, DN_HEADS, DN_CHUNK, DN_HEAD_DIM), F32),
                        pltpu.VMEM((nc, DN_HEADS, DN_CHUNK, DN_CHUNK), BF16),
                        pltpu.VMEM((nc, DN_HEADS, DN_CHUNK, DN_HEAD_DIM), BF16),
                        pltpu.VMEM((nc, DN_HEADS, 1, DN_HEAD_DIM), F32)],
        compiler_params=_cparams("parallel", "parallel", "arbitrary"),
        name="dn_scan",
    )(q, k, v, gcc, beta, gcr, s0)


NA_ROWS_PER_STEP = 8
MASKED = -1e30


def na_bias_table(rpb):
    qc = np.arange(GRID_W)[:, None]
    kc = np.arange(GRID_W)[None, :]
    c0 = np.clip(qc - NA_WIN_W // 2, 0, GRID_W - NA_WIN_W)
    valid = (kc >= c0) & (kc < c0 + NA_WIN_W)
    rel_c = np.clip(kc - qc + NA_WIN_W - 1, 0, 2 * NA_WIN_W - 2)
    case = np.arange(NA_WIN_H)[:, None]
    i = np.arange(NA_WIN_H)[None, :]
    rel_r = i + NA_WIN_H - 1 - case
    tab = rpb[:, rel_r[:, :, None, None], rel_c[None, None, :, :]]
    tab = jnp.where(valid[None, None, None], tab, MASKED)
    return jnp.transpose(tab, (0, 1, 3, 2, 4)).reshape(NA_HEADS, NA_WIN_H, GRID_W, NA_WIN_H * GRID_W)


def _head_masks():
    lane = lax.broadcasted_iota(jnp.int32, (1, NA_WIDTH), 1)
    return [(lane // NA_HEAD_DIM == h) for h in range(NA_HEADS)]


def _na_kernel(grid_rows, q_ref, kp_ref, kc_ref, kn_ref, vp_ref, vc_ref, vn_ref, kx_ref, vx_ref, bias_ref, o_ref,
               kwin, vwin):
    j = pl.program_id(1)
    tq = q_ref.shape[0]
    nkeys = NA_WIN_H * GRID_W
    kwin[0:tq] = kp_ref[...]
    kwin[tq:2 * tq] = kc_ref[...]
    kwin[2 * tq:3 * tq] = kn_ref[...]
    vwin[0:tq] = vp_ref[...]
    vwin[tq:2 * tq] = vc_ref[...]
    vwin[2 * tq:3 * tq] = vn_ref[...]
    kx = kx_ref[...]
    vx = vx_ref[...]
    masks = _head_masks()
    for rl in range(NA_ROWS_PER_STEP):
        r = j * NA_ROWS_PER_STEP + rl
        r0 = jnp.clip(r - NA_WIN_H // 2, 0, grid_rows - NA_WIN_H)
        case = r - r0
        start = pl.multiple_of((r0 - (j - 1) * NA_ROWS_PER_STEP) * GRID_W, GRID_W)
        kk = kwin[pl.ds(start, nkeys), :]
        vv = vwin[pl.ds(start, nkeys), :]
        q = q_ref[rl * GRID_W:(rl + 1) * GRID_W, :]
        acc = jnp.zeros((GRID_W, NA_WIDTH), F32)
        for h in range(NA_HEADS):
            qh = jnp.where(masks[h], q, jnp.zeros_like(q))
            s_loc = _dot_nt(qh, kk) + bias_ref[h, case]
            s_ctx = _dot_nt(qh, kx)
            m = jnp.maximum(jnp.max(s_loc, axis=-1, keepdims=True), jnp.max(s_ctx, axis=-1, keepdims=True))
            p_loc = jnp.exp(s_loc - m)
            p_ctx = jnp.exp(s_ctx - m)
            denom = jnp.sum(p_loc, axis=-1, keepdims=True) + jnp.sum(p_ctx, axis=-1, keepdims=True)
            pv = _dot(p_loc.astype(BF16), vv) + _dot(p_ctx.astype(BF16), vx)
            acc = acc + jnp.where(masks[h], pv / denom, 0.0)
        o_ref[rl * GRID_W:(rl + 1) * GRID_W, :] = acc.astype(o_ref.dtype)


def na_attention(q, k, v, k_ctx, v_ctx, bias, *, batch):
    n = q.shape[0]
    seq = n // batch
    ctx_len = k_ctx.shape[0] // batch
    grid_rows = seq // GRID_W
    tq = NA_ROWS_PER_STEP * GRID_W
    nblk = seq // tq
    cur = lambda b, j: (b * nblk + j, 0)
    prev = lambda b, j: (b * nblk + jnp.maximum(j - 1, 0), 0)
    nxt = lambda b, j: (b * nblk + jnp.minimum(j + 1, nblk - 1), 0)
    blk = lambda f: pl.BlockSpec((tq, NA_WIDTH), f)
    ctx = pl.BlockSpec((ctx_len, NA_WIDTH), lambda b, j: (b, 0))
    return pl.pallas_call(
        functools.partial(_na_kernel, grid_rows),
        out_shape=jax.ShapeDtypeStruct((n, NA_WIDTH), BF16),
        grid=(batch, nblk),
        in_specs=[blk(cur), blk(prev), blk(cur), blk(nxt), blk(prev), blk(cur), blk(nxt), ctx, ctx,
                  pl.BlockSpec(bias.shape, lambda b, j: (0, 0, 0, 0))],
        out_specs=blk(cur),
        scratch_shapes=[pltpu.VMEM((3 * tq, NA_WIDTH), BF16), pltpu.VMEM((3 * tq, NA_WIDTH), BF16)],
        compiler_params=_cparams("parallel", "parallel"),
        name="na_attention",
    )(q, k, k, k, v, v, v, k_ctx, v_ctx, bias)


def _ctx_attn_kernel(q_ref, k_ref, v_ref, o_ref):
    q = q_ref[...]
    k = k_ref[...]
    v = v_ref[...]
    masks = _head_masks()
    acc = jnp.zeros(q.shape, F32)
    for h in range(NA_HEADS):
        s = _dot_nt(jnp.where(masks[h], q, jnp.zeros_like(q)), k)
        p = jnp.exp(s - jnp.max(s, axis=-1, keepdims=True))
        pv = _dot(p.astype(BF16), v) / jnp.sum(p, axis=-1, keepdims=True)
        acc = acc + jnp.where(masks[h], pv, 0.0)
    o_ref[...] = acc.astype(o_ref.dtype)


def ctx_attention(q, k, v, *, batch):
    n = q.shape[0]
    blk = pl.BlockSpec((n // batch, NA_WIDTH), lambda b: (b, 0))
    return pl.pallas_call(
        _ctx_attn_kernel,
        out_shape=jax.ShapeDtypeStruct((n, NA_WIDTH), BF16),
        grid=(batch,),
        in_specs=[blk, blk, blk],
        out_specs=blk,
        compiler_params=_cparams("parallel"),
        name="ctx_attention",
    )(q, k, v)


FT_N1 = 128
FT_T2_BLOCK = 8
FT_K1_BLOCK = 8


def _dft_cos_sin(n, scale=1.0):
    ang = 2.0 * np.pi * np.outer(np.arange(n), np.arange(n)) / n
    return np.cos(ang) * scale, np.sin(ang) * scale


def _channel_dft():
    c, s = _dft_cos_sin(FT_GROUP_DIM)
    eye = np.eye(FT_GROUPS)
    return np.concatenate([np.kron(eye, c), -np.kron(eye, s)], axis=1)


def _ft_stage1_kernel(u_ref, cs_ref, ff_ref, yr_ref, yi_ref):
    n1 = u_ref.shape[1]
    for s in range(u_ref.shape[2] // FT_WIDTH):
        lanes = slice(s * FT_WIDTH, (s + 1) * FT_WIDTH)
        z = _dot(u_ref[0, :, lanes], cs_ref[...])
        zz = jnp.concatenate([z[:, 0:FT_WIDTH], z[:, FT_WIDTH:]], axis=0).astype(BF16)
        y = _dot(ff_ref[...], zz)
        yr_ref[0, :, lanes] = y[0:n1]
        yi_ref[0, :, lanes] = y[n1:]


def _ft_stage2_kernel(yr_ref, yi_ref, twr_ref, twi_ref, g_ref, w_ref, o_ref):
    for i in range(yr_ref.shape[1]):
        yr = yr_ref[0, i]
        yi = yi_ref[0, i]
        tr = twr_ref[0, :, i:i + 1]
        ti = twi_ref[0, :, i:i + 1]
        yy = jnp.concatenate([yr * tr - yi * ti, yr * ti + yi * tr], axis=0).astype(BF16)
        xr = _dot(g_ref[...], yy)
        o_ref[0, :, i, :] = _dot(xr.astype(BF16), w_ref[...]).astype(o_ref.dtype)


def fourier_mix(u, fno_w, *, batch):
    n = u.shape[0]
    seq = n // batch
    n1 = FT_N1
    n2 = seq // n1
    tb = min(FT_T2_BLOCK, n2)
    norm = 1.0 / math.sqrt(seq * FT_GROUP_DIM)
    c1, s1 = _dft_cos_sin(n1)
    ff = jnp.asarray(np.block([[c1, s1], [-s1, c1]]), BF16)
    yr, yi = pl.pallas_call(
        _ft_stage1_kernel,
        out_shape=[jax.ShapeDtypeStruct((batch, n1, n2 * FT_WIDTH), F32)] * 2,
        grid=(batch, n2 // tb),
        in_specs=[pl.BlockSpec((1, n1, tb * FT_WIDTH), lambda b, j: (b, 0, j)),
                  pl.BlockSpec((FT_WIDTH, 2 * FT_WIDTH), lambda b, j: (0, 0)),
                  pl.BlockSpec((2 * n1, 2 * n1), lambda b, j: (0, 0))],
        out_specs=[pl.BlockSpec((1, n1, tb * FT_WIDTH), lambda b, j: (b, 0, j))] * 2,
        compiler_params=_cparams("parallel", "parallel"),
        name="fourier_stage1",
    )(u.reshape(batch, n1, n2 * FT_WIDTH), jnp.asarray(_channel_dft(), BF16), ff)
    kb = FT_K1_BLOCK
    ang = 2.0 * np.pi * np.outer(np.arange(n1), np.arange(n2)) / seq
    tw = lambda f: jnp.asarray(f(ang).reshape(n1 // kb, kb, n2).transpose(0, 2, 1), F32)
    c2, s2 = _dft_cos_sin(n2, norm)
    g = jnp.asarray(np.concatenate([c2, s2], axis=1), BF16)
    out = pl.pallas_call(
        _ft_stage2_kernel,
        out_shape=jax.ShapeDtypeStruct((batch, n2, n1, FT_WIDTH), BF16),
        grid=(batch, n1 // kb),
        in_specs=[pl.BlockSpec((1, kb, n2, FT_WIDTH), lambda b, j: (b, j, 0, 0))] * 2 + [
            pl.BlockSpec((1, n2, kb), lambda b, j: (j, 0, 0))] * 2 + [
            pl.BlockSpec((n2, 2 * n2), lambda b, j: (0, 0)), pl.BlockSpec((FT_WIDTH, FT_WIDTH), lambda b, j: (0, 0))],
        out_specs=pl.BlockSpec((1, n2, kb, FT_WIDTH), lambda b, j: (b, 0, j, 0)),
        compiler_params=_cparams("parallel", "parallel"),
        name="fourier_stage2",
    )(yr.reshape(batch, n1, n2, FT_WIDTH), yi.reshape(batch, n1, n2, FT_WIDTH),
      tw(np.cos), tw(lambda a: -np.sin(a)), g, fno_w.astype(BF16))
    return out.reshape(n, FT_WIDTH)


def _ft_direct_kernel(u_ref, cs_ref, g_ref, w_ref, o_ref):
    z = _dot(u_ref[...], cs_ref[...])
    zz = jnp.concatenate([z[:, 0:FT_WIDTH], z[:, FT_WIDTH:]], axis=0).astype(BF16)
    xr = _dot(g_ref[...], zz)
    o_ref[...] = _dot(xr.astype(BF16), w_ref[...]).astype(o_ref.dtype)


def fourier_mix_direct(u, fno_w, *, batch):
    n = u.shape[0]
    seq = n // batch
    c, s = _dft_cos_sin(seq, 1.0 / math.sqrt(seq * FT_GROUP_DIM))
    g = jnp.asarray(np.concatenate([c, s], axis=1), BF16)
    return pl.pallas_call(
        _ft_direct_kernel,
        out_shape=jax.ShapeDtypeStruct((n, FT_WIDTH), BF16),
        grid=(batch,),
        in_specs=[pl.BlockSpec((seq, FT_WIDTH), lambda b: (b, 0)),
                  pl.BlockSpec((FT_WIDTH, 2 * FT_WIDTH), lambda b: (0, 0)),
                  pl.BlockSpec((seq, 2 * seq), lambda b: (0, 0)),
                  pl.BlockSpec((FT_WIDTH, FT_WIDTH), lambda b: (0, 0))],
        out_specs=pl.BlockSpec((seq, FT_WIDTH), lambda b: (b, 0)),
        compiler_params=_cparams("parallel"),
        name="fourier_direct",
    )(u, jnp.asarray(_channel_dft(), BF16), g, fno_w.astype(BF16))


def _mod_row(mod_ref, rows_per_mod, fixed_row, tm):
    row = fixed_row if fixed_row is not None else (pl.program_id(0) * tm) // rows_per_mod
    return mod_ref[pl.ds(row, 1), :]


def _mix_out_kernel(rows_per_mod, fixed_row, route, o_ref, z_ref, ob_ref, oc_ref, x_ref, mod_ref, nw_ref, ones_ref,
                    w_ref, g_ref, *rest):
    if route:
        wr_ref, x_out, h_out, e_out, p_out = rest
    else:
        x_out, h_out = rest
    tm = x_ref.shape[0]
    m = _mod_row(mod_ref, rows_per_mod, fixed_row, tm)
    o = o_ref[0].astype(F32) + o_ref[1].astype(F32)
    oa = _head_rms(o, ones_ref[...], DN_HEAD_DIM, nw_ref[...]) * _silu(z_ref[...].astype(F32))
    y = (_dot(oa.astype(BF16), w_ref[0:DN_WIDTH, :]) + _dot(ob_ref[...], w_ref[DN_WIDTH:DN_WIDTH + NA_WIDTH, :])
         + _dot(oc_ref[...], w_ref[DN_WIDTH + NA_WIDTH:, :]))
    x = x_ref[...] + m[:, 2 * D_MODEL:3 * D_MODEL] * y
    x_out[...] = x
    h = _modulated_norm(x, g_ref[...], m[:, 3 * D_MODEL:4 * D_MODEL], m[:, 4 * D_MODEL:5 * D_MODEL])
    h_out[...] = h.astype(h_out.dtype)
    if route:
        logits = _dot_hi(h, wr_ref[...])
        idx = lax.broadcasted_iota(jnp.int32, logits.shape, 1)
        m1 = jnp.max(logits, axis=-1, keepdims=True)
        e1 = jnp.min(jnp.where(logits == m1, idx, N_EXPERTS), axis=-1, keepdims=True)
        rest_l = jnp.where(idx == e1, -jnp.inf, logits)
        m2 = jnp.max(rest_l, axis=-1, keepdims=True)
        e2 = jnp.min(jnp.where(rest_l == m2, idx, N_EXPERTS), axis=-1, keepdims=True)
        t = jnp.exp(m2 - m1)
        e_out[...] = jnp.concatenate([e1, e2], axis=1)
        p_out[...] = jnp.concatenate([1.0 / (1.0 + t), t / (1.0 + t)], axis=1)


def mix_out(o_dirs, z, ob, oc, x2d, mod, dn_norm_w, w_out, g_ffn, w_router=None, *, rows_per_mod, fixed_row=None,
            tm=512):
    n, d = x2d.shape
    tm = min(tm, n)
    route = w_router is not None
    full = lambda shape: pl.BlockSpec(shape, lambda i: (0,) * len(shape))
    rowblk = lambda w: pl.BlockSpec((tm, w), lambda i: (i, 0))
    ins = [o_dirs, z, ob, oc, x2d, mod, jnp.tile(dn_norm_w, DN_HEADS).reshape(1, DN_WIDTH),
           _group_ones(DN_WIDTH, DN_HEAD_DIM), w_out, g_ffn.reshape(1, d)]
    in_specs = [pl.BlockSpec((DN_DIRS, tm, DN_WIDTH), lambda i: (0, i, 0)), rowblk(DN_WIDTH), rowblk(NA_WIDTH),
                rowblk(FT_WIDTH), rowblk(d), full(mod.shape), full((1, DN_WIDTH)), full((DN_WIDTH, DN_WIDTH)),
                full(w_out.shape), full((1, d))]
    outs = [jax.ShapeDtypeStruct((n, d), F32), jax.ShapeDtypeStruct((n, d), F32 if route else BF16)]
    out_specs = [rowblk(d), rowblk(d)]
    if route:
        ins.append(w_router)
        in_specs.append(full(w_router.shape))
        outs += [jax.ShapeDtypeStruct((n, TOP_K), jnp.int32), jax.ShapeDtypeStruct((n, TOP_K), F32)]
        out_specs += [rowblk(TOP_K), rowblk(TOP_K)]
    return pl.pallas_call(
        functools.partial(_mix_out_kernel, rows_per_mod, fixed_row, route),
        out_shape=outs,
        grid=(n // tm,),
        in_specs=in_specs,
        out_specs=out_specs,
        compiler_params=_cparams("parallel"),
        name="mix_out_route" if route else "mix_out",
    )(*ins)


def _ffn_kernel(rows_per_mod, fixed_row, h_ref, x_ref, mod_ref, w1_ref, w3_ref, w2_ref, o_ref, acc_ref):
    f = pl.program_id(1)
    h = h_ref[...]
    part = _dot((_silu(_dot(h, w1_ref[...])) * _dot(h, w3_ref[...])).astype(BF16), w2_ref[...])

    @pl.when(f == 0)
    def _():
        acc_ref[...] = part

    @pl.when(f > 0)
    def _():
        acc_ref[...] += part

    @pl.when(f == pl.num_programs(1) - 1)
    def _():
        m = _mod_row(mod_ref, rows_per_mod, fixed_row, h_ref.shape[0])
        o_ref[...] = x_ref[...] + m[:, 5 * D_MODEL:6 * D_MODEL] * acc_ref[...]


def dense_ffn(h, x2d, mod, w1, w3, w2, *, rows_per_mod, fixed_row=None, tm=512, tf=1408):
    n, d = x2d.shape
    tm = min(tm, n)
    dff = w1.shape[1]
    return pl.pallas_call(
        functools.partial(_ffn_kernel, rows_per_mod, fixed_row),
        out_shape=jax.ShapeDtypeStruct((n, d), F32),
        grid=(n // tm, dff // tf),
        in_specs=[pl.BlockSpec((tm, d), lambda i, f: (i, 0)), pl.BlockSpec((tm, d), lambda i, f: (i, 0)),
                  pl.BlockSpec(mod.shape, lambda i, f: (0, 0)),
                  pl.BlockSpec((d, tf), lambda i, f: (0, f)), pl.BlockSpec((d, tf), lambda i, f: (0, f)),
                  pl.BlockSpec((tf, d), lambda i, f: (f, 0))],
        out_specs=pl.BlockSpec((tm, d), lambda i, f: (i, 0)),
        scratch_shapes=[pltpu.VMEM((tm, d), F32)],
        compiler_params=_cparams("parallel", "arbitrary"),
        name="dense_ffn",
    )(h, x2d, mod, w1, w3, w2)


MOE_ROWS = 1024
MOE_DMA_ROWS = 256


def _row_copy(src, src_row, dst, dst_row, sem):
    return pltpu.make_async_copy(src.at[pl.ds(src_row, 1)], dst.at[pl.ds(dst_row, 1)], sem)


def _moe_scatter_kernel(dest_ref, h_hbm, xb_in, xb_out, sem):
    del xb_in
    tm = dest_ref.shape[2] // TOP_K
    base = pl.program_id(0) * tm

    def start(t, carry):
        for j in range(TOP_K):
            _row_copy(h_hbm, base + t, xb_out, dest_ref[0, 0, TOP_K * t + j], sem).start()
        return carry

    def wait(t, carry):
        for j in range(TOP_K):
            _row_copy(h_hbm, 0, xb_out, 0, sem).wait()
        return carry

    lax.fori_loop(0, tm, start, 0, unroll=8)
    lax.fori_loop(0, tm, wait, 0, unroll=8)


def _moe_ffn_kernel(be_ref, na_ref, x_ref, w1_ref, w3_ref, w2_ref, y_ref, acc_ref):
    i = pl.program_id(0)
    f = pl.program_id(1)
    last = f == pl.num_programs(1) - 1
    active = i < na_ref[0]

    @pl.when(active)
    def _():
        h = x_ref[...].astype(BF16)
        part = _dot((_silu(_dot(h, w1_ref[0])) * _dot(h, w3_ref[0])).astype(BF16), w2_ref[0])

        @pl.when(f == 0)
        def _():
            acc_ref[...] = part

        @pl.when(f > 0)
        def _():
            acc_ref[...] += part

        @pl.when(last)
        def _():
            y_ref[...] = acc_ref[...]

    @pl.when(jnp.logical_and(jnp.logical_not(active), last))
    def _():
        y_ref[...] = jnp.zeros_like(y_ref)


def _moe_combine_kernel(rows_per_mod, dest_ref, p_ref, x_ref, mod_ref, yb_hbm, o_ref, buf, sem):
    tm = x_ref.shape[0]

    def start(t, carry):
        for j in range(TOP_K):
            _row_copy(yb_hbm, dest_ref[0, 0, TOP_K * t + j], buf.at[j], t, sem).start()
        return carry

    def wait(t, carry):
        for j in range(TOP_K):
            _row_copy(yb_hbm, 0, buf.at[j], 0, sem).wait()
        return carry

    lax.fori_loop(0, tm, start, 0, unroll=8)
    lax.fori_loop(0, tm, wait, 0, unroll=8)
    p = p_ref[...]
    y = p[:, 0:1] * buf[0] + p[:, 1:2] * buf[1]
    m = _mod_row(mod_ref, rows_per_mod, None, tm)
    o_ref[...] = x_ref[...] + m[:, 5 * D_MODEL:6 * D_MODEL] * y


def moe_ffn(h, x2d, mod, top_e, top_p, w1, w3, w2, *, rows_per_mod, tf=896):
    n, d = x2d.shape
    dff = w1.shape[2]
    n_assign = n * TOP_K
    n_blocks = -(-n_assign // MOE_ROWS) + N_EXPERTS
    n_slots = n_blocks * MOE_ROWS
    e_flat = top_e.reshape(n_assign)
    onehot = (e_flat[:, None] == jnp.arange(N_EXPERTS, dtype=jnp.int32)[None, :]).astype(jnp.int32)
    csum = jnp.cumsum(onehot, axis=0)
    rank = jnp.sum(onehot * csum, axis=1) - 1
    counts = csum[-1]
    padded = (counts + MOE_ROWS - 1) // MOE_ROWS * MOE_ROWS
    pad_end = jnp.cumsum(padded)
    dest = ((pad_end - padded)[e_flat] + rank).astype(jnp.int32)
    n_active = (pad_end[-1] // MOE_ROWS).astype(jnp.int32).reshape(1)
    blk_ids = jnp.minimum(jnp.arange(n_blocks, dtype=jnp.int32), n_active[0] - 1)
    block_e = jnp.minimum(jnp.searchsorted(pad_end, blk_ids * MOE_ROWS, side='right'), N_EXPERTS - 1).astype(jnp.int32)

    tmd = min(MOE_DMA_ROWS, n)
    dest2d = dest.reshape(n // tmd, 1, TOP_K * tmd)
    dest_spec = pl.BlockSpec((1, 1, TOP_K * tmd), lambda i: (i, 0, 0), memory_space=pltpu.SMEM)
    hbm = pl.BlockSpec(memory_space=pl.ANY)
    xb = pl.pallas_call(
        _moe_scatter_kernel,
        out_shape=jax.ShapeDtypeStruct((n_slots, d), F32),
        grid=(n // tmd,),
        in_specs=[dest_spec, hbm, hbm],
        out_specs=hbm,
        scratch_shapes=[pltpu.SemaphoreType.DMA],
        input_output_aliases={2: 0},
        compiler_params=_cparams("arbitrary"),
        name="moe_scatter",
    )(dest2d, h, jnp.zeros((n_slots, d), F32))

    yb = pl.pallas_call(
        _moe_ffn_kernel,
        out_shape=jax.ShapeDtypeStruct((n_slots, d), F32),
        grid_spec=pltpu.PrefetchScalarGridSpec(
            num_scalar_prefetch=2,
            grid=(n_blocks, dff // tf),
            in_specs=[pl.BlockSpec((MOE_ROWS, d), lambda i, f, be, na: (i, 0)),
                      pl.BlockSpec((1, d, tf), lambda i, f, be, na: (be[i], 0, f)),
                      pl.BlockSpec((1, d, tf), lambda i, f, be, na: (be[i], 0, f)),
                      pl.BlockSpec((1, tf, d), lambda i, f, be, na: (be[i], f, 0))],
            out_specs=pl.BlockSpec((MOE_ROWS, d), lambda i, f, be, na: (i, 0)),
            scratch_shapes=[pltpu.VMEM((MOE_ROWS, d), F32)]),
        compiler_params=_cparams("arbitrary", "arbitrary"),
        name="moe_expert_ffn",
    )(block_e, n_active, xb, w1, w3, w2)

    return pl.pallas_call(
        functools.partial(_moe_combine_kernel, rows_per_mod),
        out_shape=jax.ShapeDtypeStruct((n, d), F32),
        grid=(n // tmd,),
        in_specs=[dest_spec, pl.BlockSpec((tmd, TOP_K), lambda i: (i, 0)), pl.BlockSpec((tmd, d), lambda i: (i, 0)),
                  pl.BlockSpec(mod.shape, lambda i: (0, 0)), hbm],
        out_specs=pl.BlockSpec((tmd, d), lambda i: (i, 0)),
        scratch_shapes=[pltpu.VMEM((TOP_K, tmd, d), F32), pltpu.SemaphoreType.DMA],
        compiler_params=_cparams("arbitrary"),
        name="moe_combine",
    )(dest2d, top_p, x2d, mod, yb)


def _split_w_in(w_in_l):
    g0 = 4 * DN_WIDTH
    g1 = g0 + 2 * N_DH
    w_main = jnp.concatenate([w_in_l[:, :g0], w_in_l[:, g1:]], axis=1).astype(BF16)
    w_ab = w_in_l[:, g0:g1].astype(BF16)
    return w_main, w_ab, w_ab.T


def kernel(x, c, ctx, c_ctx, w_mod, b_mod, g_mix, g_ffn, w_in, dn_conv, dn_a_log, dn_dt_bias, dn_norm_w, na_q_norm,
           na_k_norm, na_rpb, fno_w, w_out, ffn_w1, ffn_w3, ffn_w2, moe_router, moe_w1, moe_w3, moe_w2):
    bsz, seq, d = x.shape
    ctx_len = ctx.shape[1]
    ctx_row = bsz
    cond = jnp.zeros((SUBLANES, d), F32).at[:bsz].set(c).at[ctx_row].set(c_ctx)
    mods = adaln_mod(cond, w_mod, b_mod)
    x2 = x.reshape(bsz * seq, d)
    c2 = ctx.reshape(bsz * ctx_len, d)
    zero_state = jnp.zeros((bsz, DN_DIRS, DN_HEADS, DN_HEAD_DIM, DN_HEAD_DIM), F32)

    for layer in range(DEPTH):
        need_ctx = layer < DEPTH - 1
        mod = mods[layer]
        w_main, w_ab, w_abt = _split_w_in(w_in[layer])
        proj = functools.partial(in_proj, mod=mod, g=g_mix[layer], w_main=w_main, w_ab=w_ab, w_abt=w_abt,
                                 q_norm=na_q_norm[layer], k_norm=na_k_norm[layer])
        qkv, z, ab, abt, nq, nk, nv, ft = proj(x2, rows_per_mod=seq)
        qkv_c, z_c, ab_c, abt_c, nq_c, nk_c, nv_c, ft_c = proj(c2, rows_per_mod=ctx_len, fixed_row=ctx_row)

        prep = functools.partial(dn_prep, conv_w=dn_conv[layer], a_log=dn_a_log[layer], dt_bias=dn_dt_bias[layer],
                                 batch=bsz)
        q_c, k_c, v_c, gcc_c, beta_c, gcr_c = prep(qkv_c, ab_c, abt_c)
        o_c, s_ctx = dn_scan(q_c, k_c, v_c, gcc_c, beta_c, gcr_c, zero_state, batch=bsz)
        q_l, k_l, v_l, gcc_l, beta_l, gcr_l = prep(qkv, ab, abt)
        o_l, _ = dn_scan(q_l, k_l, v_l, gcc_l, beta_l, gcr_l, s_ctx, batch=bsz)

        ob = na_attention(nq, nk, nv, nk_c, nv_c, na_bias_table(na_rpb[layer]), batch=bsz)
        oc = fourier_mix(ft, fno_w[layer], batch=bsz)

        w_out_l = w_out[layer].astype(BF16)
        j = layer // 2
        if layer % 2 == 0:
            x2, h2 = mix_out(o_l, z, ob, oc, x2, mod, dn_norm_w[layer], w_out_l, g_ffn[layer], rows_per_mod=seq)
            ffn_w = (ffn_w1[j].astype(BF16), ffn_w3[j].astype(BF16), ffn_w2[j].astype(BF16))
            x2 = dense_ffn(h2, x2, mod, *ffn_w, rows_per_mod=seq)
        else:
            x2, h2, top_e, top_p = mix_out(o_l, z, ob, oc, x2, mod, dn_norm_w[layer], w_out_l, g_ffn[layer],
                                           moe_router[j], rows_per_mod=seq)
            x2 = moe_ffn(h2, x2, mod, top_e, top_p, moe_w1[j].astype(BF16), moe_w3[j].astype(BF16),
                         moe_w2[j].astype(BF16), rows_per_mod=seq)
        if need_ctx:
            ob_c = ctx_attention(nq_c, nk_c, nv_c, batch=bsz)
            oc_c = fourier_mix_direct(ft_c, fno_w[layer], batch=bsz)
            assert layer % 2 == 0, "context tokens only pass through dense layers at this depth"
            c2, hc2 = mix_out(o_c, z_c, ob_c, oc_c, c2, mod, dn_norm_w[layer], w_out_l, g_ffn[layer],
                              rows_per_mod=ctx_len, fixed_row=ctx_row)
            c2 = dense_ffn(hc2, c2, mod, *ffn_w, rows_per_mod=ctx_len, fixed_row=ctx_row)
    return x2.reshape(bsz, seq, d)
```

```python
import functools
import math

import jax
import jax.numpy as jnp
import numpy as np
from jax import lax
from jax.experimental import pallas as pl
from jax.experimental.pallas import tpu as pltpu

D_MODEL = 1024
DEPTH = 2
GRID_W = 64
DN_HEADS = 4
DN_HEAD_DIM = 128
DN_WIDTH = DN_HEADS * DN_HEAD_DIM
DN_CONV = 3
DN_CHUNK = 64
DN_DIRS = 2
NA_HEADS = 4
NA_HEAD_DIM = 64
NA_WIDTH = NA_HEADS * NA_HEAD_DIM
NA_WIN_H = 8
NA_WIN_W = 16
FT_GROUPS = 4
FT_GROUP_DIM = 64
FT_WIDTH = FT_GROUPS * FT_GROUP_DIM
N_EXPERTS = 8
TOP_K = 2
N_MOD = 6
EPS = 1e-6

SUBLANES = 8
LANES = 128
VMEM_LIMIT = 48 * 1024 * 1024

BF16 = jnp.bfloat16
F32 = jnp.float32
N_DH = DN_DIRS * DN_HEADS


def _cparams(*sem):
    return pltpu.CompilerParams(dimension_semantics=sem, vmem_limit_bytes=VMEM_LIMIT)


def _silu(x):
    return x * jax.nn.sigmoid(x)


def _dot(a, b):
    return jnp.dot(a, b, preferred_element_type=F32)


def _dot_nt(a, b):
    return lax.dot_general(a, b, (((1,), (1,)), ((), ())), preferred_element_type=F32)


def _dot_tn(a, b):
    return lax.dot_general(a, b, (((0,), (0,)), ((), ())), preferred_element_type=F32)


def _group_ones(width, group):
    idx = np.arange(width) // group
    return jnp.asarray((idx[:, None] == idx[None, :]).astype(np.float32), BF16)


def _mod_kernel(c_ref, w_ref, b_ref, o_ref):
    s = _silu(c_ref[...])
    o_ref[0] = jnp.dot(s, w_ref[0], preferred_element_type=F32, precision=lax.Precision.HIGHEST) + b_ref[0]


def adaln_mod(cond_rows, w_mod, b_mod, *, tn=1536):
    depth, d, n = w_mod.shape
    rows = cond_rows.shape[0]
    return pl.pallas_call(
        _mod_kernel,
        out_shape=jax.ShapeDtypeStruct((depth, rows, n), F32),
        grid=(depth, n // tn),
        in_specs=[pl.BlockSpec((rows, d), lambda l, j: (0, 0)),
                  pl.BlockSpec((1, d, tn), lambda l, j: (l, 0, j)),
                  pl.BlockSpec((1, 1, tn), lambda l, j: (l, 0, j))],
        out_specs=pl.BlockSpec((1, rows, tn), lambda l, j: (l, 0, j)),
        compiler_params=_cparams("arbitrary", "arbitrary"),
        name="adaln_mod",
    )(cond_rows, w_mod, b_mod.reshape(depth, 1, n))


C_QKV = 0
C_Z = 3 * DN_WIDTH
C_NQ = C_Z + DN_WIDTH
C_NK = C_NQ + NA_WIDTH
C_NV = C_NK + NA_WIDTH
C_FT = C_NV + NA_WIDTH
C_END = C_FT + FT_WIDTH


def _modulated_norm(x, g, shift, scale):
    y = x * lax.rsqrt(jnp.mean(x * x, axis=-1, keepdims=True) + EPS)
    return y * g * (1.0 + scale) + shift


def _head_rms(x, ones, width, gain):
    ss = _dot((x * x).astype(BF16), ones)
    return x * lax.rsqrt(ss * (1.0 / width) + EPS) * gain


def _in_proj_kernel(rows_per_mod, fixed_row, x_ref, mod_ref, g_ref, w_ref, wab_ref, wabt_ref, qn_ref, kn_ref,
                    ones_ref, qkv_ref, z_ref, ab_ref, abt_ref, nq_ref, nk_ref, nv_ref, ft_ref):
    tm = x_ref.shape[0]
    row = fixed_row if fixed_row is not None else (pl.program_id(0) * tm) // rows_per_mod
    m = mod_ref[pl.ds(row, 1), :]
    shift, scale = m[:, 0:D_MODEL], m[:, D_MODEL:2 * D_MODEL]
    h = _modulated_norm(x_ref[...], g_ref[...], shift, scale).astype(BF16)
    qkv_ref[...] = _dot(h, w_ref[:, C_QKV:C_Z]).astype(qkv_ref.dtype)
    z_ref[...] = _dot(h, w_ref[:, C_Z:C_NQ]).astype(z_ref.dtype)
    ones = ones_ref[...]
    nq = _dot(h, w_ref[:, C_NQ:C_NK])
    nq_ref[...] = (_head_rms(nq, ones, NA_HEAD_DIM, qn_ref[...]) * (NA_HEAD_DIM ** -0.5)).astype(nq_ref.dtype)
    nk = _dot(h, w_ref[:, C_NK:C_NV])
    nk_ref[...] = _head_rms(nk, ones, NA_HEAD_DIM, kn_ref[...]).astype(nk_ref.dtype)
    nv_ref[...] = _dot(h, w_ref[:, C_NV:C_FT]).astype(nv_ref.dtype)
    ft_ref[...] = _dot(h, w_ref[:, C_FT:C_END]).astype(ft_ref.dtype)
    ab_ref[...] = _dot(h, wab_ref[...])
    abt_ref[...] = _dot_nt(wabt_ref[...], h)


def in_proj(x2d, mod, g, w_main, w_ab, w_abt, q_norm, k_norm, *, rows_per_mod, fixed_row=None, tm=512):
    n, d = x2d.shape
    tm = min(tm, n)
    nab = w_ab.shape[1]
    full = lambda shape: pl.BlockSpec(shape, lambda i: (0,) * len(shape))
    rowblk = lambda w: pl.BlockSpec((tm, w), lambda i: (i, 0))
    outs = [jax.ShapeDtypeStruct((n, 3 * DN_WIDTH), BF16), jax.ShapeDtypeStruct((n, DN_WIDTH), BF16),
            jax.ShapeDtypeStruct((n, nab), F32), jax.ShapeDtypeStruct((nab, n), F32),
            jax.ShapeDtypeStruct((n, NA_WIDTH), BF16), jax.ShapeDtypeStruct((n, NA_WIDTH), BF16),
            jax.ShapeDtypeStruct((n, NA_WIDTH), BF16), jax.ShapeDtypeStruct((n, FT_WIDTH), BF16)]
    return pl.pallas_call(
        functools.partial(_in_proj_kernel, rows_per_mod, fixed_row),
        out_shape=outs,
        grid=(n // tm,),
        in_specs=[rowblk(d), full(mod.shape), full((1, d)), full(w_main.shape), full(w_ab.shape), full(w_abt.shape),
                  full((1, NA_WIDTH)), full((1, NA_WIDTH)), full((NA_WIDTH, NA_WIDTH))],
        out_specs=[rowblk(3 * DN_WIDTH), rowblk(DN_WIDTH), rowblk(nab), pl.BlockSpec((nab, tm), lambda i: (0, i)),
                   rowblk(NA_WIDTH), rowblk(NA_WIDTH), rowblk(NA_WIDTH), rowblk(FT_WIDTH)],
        compiler_params=_cparams("parallel"),
        name="in_proj",
    )(x2d, mod, g.reshape(1, d), w_main, w_ab, w_abt,
      jnp.tile(q_norm, NA_HEADS).reshape(1, NA_WIDTH), jnp.tile(k_norm, NA_HEADS).reshape(1, NA_WIDTH),
      _group_ones(NA_WIDTH, NA_HEAD_DIM))


QKV_HALO = 16


def _softplus(x):
    return jnp.maximum(x, 0.0) + jnp.log1p(jnp.exp(-jnp.abs(x)))


def _tri(n, lower):
    r = lax.broadcasted_iota(jnp.int32, (n, n), 0)
    c = lax.broadcasted_iota(jnp.int32, (n, n), 1)
    return jnp.where((r >= c) if lower else (r <= c), 1.0, 0.0).astype(F32)


def _dot_hi(a, b):
    return jnp.dot(a, b, preferred_element_type=F32, precision=lax.Precision.HIGHEST)


def _dn_prep_kernel(x_ref, xp_ref, xn_ref, ab_ref, abt_ref, cw_ref, alr_ref, dtr_ref, alc_ref, dtc_ref, ones_ref,
                    q_ref, k_ref, v_ref, gcc_ref, beta_ref, gcr_ref):
    i = pl.program_id(1)
    tl = x_ref.shape[0]
    x = x_ref[...].astype(F32)
    prev = jnp.where(i > 0, xp_ref[QKV_HALO - 1:QKV_HALO, :].astype(F32), 0.0)
    nxt = jnp.where(i < pl.num_programs(1) - 1, xn_ref[0:1, :].astype(F32), 0.0)
    rows = lax.broadcasted_iota(jnp.int32, x.shape, 0)
    xm1 = jnp.where(rows == 0, prev, pltpu.roll(x, 1, axis=0))
    xp1 = jnp.where(rows == tl - 1, nxt, pltpu.roll(x, tl - 1, axis=0))
    y = _silu(xm1 * cw_ref[0:1, :] + x * cw_ref[1:2, :] + xp1 * cw_ref[2:3, :])
    ones = ones_ref[...]
    q = y[:, 0:DN_WIDTH]
    k = y[:, DN_WIDTH:2 * DN_WIDTH]
    q_ref[...] = (q * lax.rsqrt(_dot((q * q).astype(BF16), ones) + EPS) * (DN_HEAD_DIM ** -0.5)).astype(q_ref.dtype)
    k_ref[...] = (k * lax.rsqrt(_dot((k * k).astype(BF16), ones) + EPS)).astype(k_ref.dtype)
    v_ref[...] = y[:, 2 * DN_WIDTH:].astype(v_ref.dtype)

    ab = ab_ref[...]
    g_col = -jnp.exp(alr_ref[...]) * _softplus(ab[:, 0:N_DH] + dtr_ref[...])
    beta = jax.nn.sigmoid(ab[:, N_DH:2 * N_DH])
    beta_ref[0] = beta[:, 0:DN_HEADS]
    beta_ref[1] = beta[:, DN_HEADS:]
    g_row = -jnp.exp(alc_ref[...]) * _softplus(abt_ref[0:N_DH, :] + dtc_ref[...])
    lo, up = _tri(DN_CHUNK, True), _tri(DN_CHUNK, False)
    sub = lax.broadcasted_iota(jnp.int32, (N_DH, DN_CHUNK), 0)
    for c in range(tl // DN_CHUNK):
        sl = slice(c * DN_CHUNK, (c + 1) * DN_CHUNK)
        gc = g_col[sl, :]
        gcc_ref[0, sl, :] = _dot_hi(lo, gc)[:, 0:DN_HEADS]
        gcc_ref[1, sl, :] = _dot_hi(up, gc)[:, DN_HEADS:]
        gr = g_row[:, sl]
        cs = jnp.where(sub < DN_HEADS, _dot_hi(gr, up), _dot_hi(gr, lo))
        gcr_ref[0, 0, c] = cs[0:DN_HEADS]
        gcr_ref[1, 0, c] = cs[DN_HEADS:]


def dn_prep(qkv, ab, abt, conv_w, a_log, dt_bias, *, batch, tl=512):
    n = qkv.shape[0]
    seq = n // batch
    tl = min(tl, seq)
    nblk = seq // tl
    hb = tl // QKV_HALO
    nc_all = seq // DN_CHUNK
    full = lambda shape: pl.BlockSpec(shape, lambda b, i: (0,) * len(shape))
    rowblk = lambda w: pl.BlockSpec((tl, w), lambda b, i: (b * nblk + i, 0))
    outs = [jax.ShapeDtypeStruct((n, DN_WIDTH), BF16)] * 3 + [
        jax.ShapeDtypeStruct((DN_DIRS, n, DN_HEADS), F32), jax.ShapeDtypeStruct((DN_DIRS, n, DN_HEADS), F32),
        jax.ShapeDtypeStruct((DN_DIRS, batch, nc_all, DN_HEADS, DN_CHUNK), F32)]
    a_log = a_log.reshape(N_DH)
    dt_bias = dt_bias.reshape(N_DH)
    return pl.pallas_call(
        _dn_prep_kernel,
        out_shape=outs,
        grid=(batch, nblk),
        in_specs=[rowblk(3 * DN_WIDTH),
                  pl.BlockSpec((QKV_HALO, 3 * DN_WIDTH), lambda b, i: (jnp.maximum((b * nblk + i) * hb - 1, 0), 0)),
                  pl.BlockSpec((QKV_HALO, 3 * DN_WIDTH),
                               lambda b, i: (jnp.minimum((b * nblk + i + 1) * hb, n // QKV_HALO - 1), 0)),
                  rowblk(2 * N_DH),
                  pl.BlockSpec((2 * N_DH, tl), lambda b, i: (0, b * nblk + i)),
                  full((DN_CONV, 3 * DN_WIDTH)), full((1, N_DH)), full((1, N_DH)), full((N_DH, 1)), full((N_DH, 1)),
                  full((DN_WIDTH, DN_WIDTH))],
        out_specs=[rowblk(DN_WIDTH)] * 3 + [
            pl.BlockSpec((DN_DIRS, tl, DN_HEADS), lambda b, i: (0, b * nblk + i, 0)),
            pl.BlockSpec((DN_DIRS, tl, DN_HEADS), lambda b, i: (0, b * nblk + i, 0)),
            pl.BlockSpec((DN_DIRS, 1, tl // DN_CHUNK, DN_HEADS, DN_CHUNK), lambda b, i: (0, b, i, 0, 0))],
        compiler_params=_cparams("parallel", "parallel"),
        name="dn_prep",
    )(qkv, qkv, qkv, ab, abt, conv_w, a_log.reshape(1, N_DH), dt_bias.reshape(1, N_DH),
      a_log.reshape(N_DH, 1), dt_bias.reshape(N_DH, 1), _group_ones(DN_WIDTH, DN_HEAD_DIM))


DN_LOCAL_GROUP = 4


def _unit_tri_inverses(mats, block):
    n = mats[0].shape[0]
    r = lax.broadcasted_iota(jnp.int32, (n, n), 0)
    c = lax.broadcasted_iota(jnp.int32, (n, n), 1)
    eye = jnp.where(r == c, 1.0, 0.0)
    ds = [eye - jnp.where(r // 2 == c // 2, a, 0.0) for a in mats]
    size = 2
    while size < block:
        off = jnp.logical_and(r // (2 * size) == c // (2 * size), r // size != c // size)
        d16s = [d.astype(BF16) for d in ds]
        ts = [_dot(d16, jnp.where(off, a, 0.0).astype(BF16)) for d16, a in zip(d16s, mats)]
        ds = [d - _dot(t.astype(BF16), d16) for d, t, d16 in zip(ds, ts, d16s)]
        size *= 2
    return ds


def _dn_scan_kernel(q_ref, k_ref, v_ref, gcc_ref, beta_ref, gcr_ref, s0_ref, o_ref, s_ref,
                    rg_scr, n_scr, h_scr, eg_scr):
    d = pl.program_id(1)
    i = pl.program_id(2)
    nc = rg_scr.shape[0]
    cs = DN_CHUNK
    hd = DN_HEAD_DIM
    fwd = d == 0

    @pl.when(i == 0)
    def _():
        s_ref[...] = s0_ref[...]

    hc = DN_HEADS * cs
    r = lax.broadcasted_iota(jnp.int32, (hc, hc), 0)
    c = lax.broadcasted_iota(jnp.int32, (hc, hc), 1)
    same_head = (r // cs) == (c // cs)
    rel = (r % cs - c % cs) * jnp.where(fwd, 1, -1)
    incl = jnp.logical_and(same_head, rel >= 0)
    strict = jnp.logical_and(same_head, rel > 0)
    last = jnp.where(fwd, cs - 1, 0)

    def load(ci):
        row0 = pl.multiple_of(ci * cs, cs)
        stack = lambda ref: jnp.concatenate(
            [ref[pl.ds(row0, cs), h * hd:(h + 1) * hd] for h in range(DN_HEADS)], axis=0)
        gate = lambda ref, rows: jnp.concatenate([ref[0, rows, h:h + 1] for h in range(DN_HEADS)], axis=0)
        q, k, v = stack(q_ref), stack(k_ref), stack(v_ref).astype(F32)
        gcol = gate(gcc_ref, pl.ds(row0, cs))
        bcol = gate(beta_ref, pl.ds(row0, cs))
        grows = gcr_ref[0, 0, ci]
        grow = jnp.concatenate([grows[h:h + 1, :] for h in range(DN_HEADS)], axis=1)
        glast_h = [gcc_ref[0, pl.ds(row0 + last, 1), h:h + 1] for h in range(DN_HEADS)]
        glast = jnp.concatenate([jnp.broadcast_to(g, (cs, 1)) for g in glast_h], axis=0)
        decay = jnp.where(incl, jnp.exp(jnp.where(incl, gcol - grow, 0.0)), 0.0)
        kf = k.astype(F32)
        kb = kf * bcol
        eg = jnp.exp(gcol)
        a = jnp.where(strict, _dot_nt(kb.astype(BF16), k) * decay, 0.0)
        rhs = jnp.concatenate([v * bcol, kb * eg], axis=1).astype(BF16)
        qk = jnp.where(incl, _dot_nt(q, k) * decay, 0.0).astype(BF16)
        kd = (kf * jnp.exp(glast - gcol)).astype(BF16)
        return dict(ci=ci, a=a, rhs=rhs, qk=qk, kd=kd, qe=q.astype(F32) * eg, glast_h=glast_h)

    def local_group(j, carry):
        chunks = [load(DN_LOCAL_GROUP * j + g) for g in range(DN_LOCAL_GROUP)]
        ts = _unit_tri_inverses([ch["a"] for ch in chunks], cs)
        uws = [_dot(t.astype(BF16), ch["rhs"]).astype(BF16) for t, ch in zip(ts, chunks)]
        hgs = [_dot(ch["qk"], uw) for ch, uw in zip(chunks, uws)]
        for ch, uw, hg in zip(chunks, uws, hgs):
            ci = ch["ci"]
            g_mat = (ch["qe"] - hg[:, hd:]).astype(BF16)
            for h in range(DN_HEADS):
                rows = slice(h * cs, (h + 1) * cs)
                nr = _dot_tn(ch["kd"][rows], uw[rows])
                n_scr[ci, h] = nr[:, 0:hd]
                h_scr[ci, h] = hg[rows, 0:hd]
                rg_scr[ci, h, 0:hd, :] = nr[:, hd:].astype(BF16)
                rg_scr[ci, h, hd:hd + cs, :] = g_mat[rows]
                eg_scr[ci, h] = jnp.broadcast_to(jnp.exp(ch["glast_h"][h]), (1, hd))
        return carry

    lax.fori_loop(0, nc // DN_LOCAL_GROUP, local_group, 0)

    def step(cc, carry):
        ci = jnp.where(fwd, cc, nc - 1 - cc)
        row0 = pl.multiple_of(ci * cs, cs)
        for h in range(DN_HEADS):
            s = s_ref[0, 0, h]
            y = _dot(rg_scr[ci, h], s.astype(BF16))
            s_ref[0, 0, h] = s * eg_scr[ci, h] - y[0:hd] + n_scr[ci, h]
            o_ref[0, pl.ds(row0, cs), h * hd:(h + 1) * hd] = (y[hd:hd + cs] + h_scr[ci, h]).astype(o_ref.dtype)
        return carry

    lax.fori_loop(0, nc, step, 0)


def dn_scan(q, k, v, gcc, beta, gcr, s0, *, batch, tl=512):
    n = q.shape[0]
    seq = n // batch
    tl = min(tl, seq)
    nblk = seq // tl
    nc = tl // DN_CHUNK
    blk = lambda b, d, i: b * nblk + jnp.where(d == 0, i, nblk - 1 - i)
    chunk_blk = lambda d, i: jnp.where(d == 0, i, nblk - 1 - i)
    rowspec = pl.BlockSpec((tl, DN_WIDTH), lambda b, d, i: (blk(b, d, i), 0))
    gate = pl.BlockSpec((1, tl, DN_HEADS), lambda b, d, i: (d, blk(b, d, i), 0))
    state = pl.BlockSpec((1, 1, DN_HEADS, DN_HEAD_DIM, DN_HEAD_DIM), lambda b, d, i: (b, d, 0, 0, 0))
    return pl.pallas_call(
        _dn_scan_kernel,
        out_shape=[jax.ShapeDtypeStruct((DN_DIRS, n, DN_WIDTH), BF16), jax.ShapeDtypeStruct(s0.shape, F32)],
        grid=(batch, DN_DIRS, nblk),
        in_specs=[rowspec, rowspec, rowspec, gate, gate,
                  pl.BlockSpec((1, 1, nc, DN_HEADS, DN_CHUNK), lambda b, d, i: (d, b, chunk_blk(d, i), 0, 0)),
                  state],
        out_specs=[pl.BlockSpec((1, tl, DN_WIDTH), lambda b, d, i: (d, blk(b, d, i), 0)), state],
        scratch_shapes=[pltpu.VMEM((nc, DN_HEADS, DN_HEAD_DIM + DN_CHUNK, DN_HEAD_DIM), BF16),
                        pltpu.VMEM((nc, DN_HEADS, DN_HEAD_DIM, DN_HEAD_DIM), F32),
                        pltpu.VMEM((nc, DN_HEADS, DN_CHUNK, DN_HEAD_DIM), F32),
                        pltpu.VMEM((nc, DN_HEADS, 1, DN_HEAD_DIM), F32)],
        compiler_params=_cparams("parallel", "parallel", "arbitrary"),
        name="dn_scan",
    )(q, k, v, gcc, beta, gcr, s0)


NA_ROWS_PER_STEP = 8
NA_ROW_GROUP = 2
MASKED = -1e30


def na_bias_table(rpb):
    qc = np.arange(GRID_W)[:, None]
    kc = np.arange(GRID_W)[None, :]
    c0 = np.clip(qc - NA_WIN_W // 2, 0, GRID_W - NA_WIN_W)
    valid = (kc >= c0) & (kc < c0 + NA_WIN_W)
    rel_c = kc - qc + NA_WIN_W - 1
    case = np.arange(NA_WIN_H)[:, None]
    i = np.arange(NA_WIN_H)[None, :]
    rel_r = i + NA_WIN_H - 1 - case
    pick_r = (rel_r[:, :, None] == np.arange(2 * NA_WIN_H - 1)).astype(np.float32)
    pick_c = ((rel_c[:, :, None] == np.arange(2 * NA_WIN_W - 1)) & valid[:, :, None]).astype(np.float32)
    tab = jnp.einsum('hrc,xir,qkc->xhqik', rpb, pick_r, pick_c, precision=lax.Precision.HIGHEST)
    tab = jnp.where(valid[None, None, :, None, :], tab, MASKED)
    return tab.reshape(NA_WIN_H, NA_HEADS * GRID_W, NA_WIN_H * GRID_W)


def _head_masks():
    lane = lax.broadcasted_iota(jnp.int32, (1, NA_WIDTH), 1)
    return [(lane // NA_HEAD_DIM == h) for h in range(NA_HEADS)]


def _na_kernel(grid_rows, q_ref, kp_ref, kc_ref, kn_ref, vp_ref, vc_ref, vn_ref, kx_ref, vx_ref, bias_ref, o_ref,
               kwin, vwin):
    j = pl.program_id(1)
    tq = q_ref.shape[0]
    nkeys = NA_WIN_H * GRID_W
    kwin[0:tq] = kp_ref[...]
    kwin[tq:2 * tq] = kc_ref[...]
    kwin[2 * tq:3 * tq] = kn_ref[...]
    vwin[0:tq] = vp_ref[...]
    vwin[tq:2 * tq] = vc_ref[...]
    vwin[2 * tq:3 * tq] = vn_ref[...]
    kx = kx_ref[...]
    vx = vx_ref[...]
    hq = NA_HEADS * GRID_W
    own = (lax.broadcasted_iota(jnp.int32, (hq, NA_WIDTH), 0) // GRID_W
           == lax.broadcasted_iota(jnp.int32, (hq, NA_WIDTH), 1) // NA_HEAD_DIM)
    for g in range(NA_ROWS_PER_STEP // NA_ROW_GROUP):
        rls = [g * NA_ROW_GROUP + t for t in range(NA_ROW_GROUP)]
        scores = []
        for rl in rls:
            r = j * NA_ROWS_PER_STEP + rl
            r0 = jnp.clip(r - NA_WIN_H // 2, 0, grid_rows - NA_WIN_H)
            start = pl.multiple_of((r0 - (j - 1) * NA_ROWS_PER_STEP) * GRID_W, GRID_W)
            q = q_ref[rl * GRID_W:(rl + 1) * GRID_W, :]
            qs = jnp.where(own, jnp.concatenate([q] * NA_HEADS, axis=0), jnp.zeros((hq, NA_WIDTH), q.dtype))
            s_loc = _dot_nt(qs, kwin[pl.ds(start, nkeys), :]) + bias_ref[r - r0]
            scores.append((s_loc, _dot_nt(qs, kx), start))
        probs = []
        for s_loc, s_ctx, start in scores:
            m = jnp.maximum(jnp.max(s_loc, axis=-1, keepdims=True), jnp.max(s_ctx, axis=-1, keepdims=True))
            p_loc = jnp.exp(s_loc - m)
            p_ctx = jnp.exp(s_ctx - m)
            denom = jnp.sum(p_loc, axis=-1, keepdims=True) + jnp.sum(p_ctx, axis=-1, keepdims=True)
            probs.append((p_loc.astype(BF16), p_ctx.astype(BF16), denom, start))
        for rl, (p_loc, p_ctx, denom, start) in zip(rls, probs):
            pv = _dot(p_loc, vwin[pl.ds(start, nkeys), :]) + _dot(p_ctx, vx)
            pv = jnp.where(own, pv / denom, 0.0)
            out = pv[0:GRID_W]
            for h in range(1, NA_HEADS):
                out = out + pv[h * GRID_W:(h + 1) * GRID_W]
            o_ref[rl * GRID_W:(rl + 1) * GRID_W, :] = out.astype(o_ref.dtype)


def na_attention(q, k, v, k_ctx, v_ctx, bias, *, batch):
    n = q.shape[0]
    seq = n // batch
    ctx_len = k_ctx.shape[0] // batch
    grid_rows = seq // GRID_W
    tq = NA_ROWS_PER_STEP * GRID_W
    nblk = seq // tq
    cur = lambda b, j: (b * nblk + j, 0)
    prev = lambda b, j: (b * nblk + jnp.maximum(j - 1, 0), 0)
    nxt = lambda b, j: (b * nblk + jnp.minimum(j + 1, nblk - 1), 0)
    blk = lambda f: pl.BlockSpec((tq, NA_WIDTH), f)
    ctx = pl.BlockSpec((ctx_len, NA_WIDTH), lambda b, j: (b, 0))
    return pl.pallas_call(
        functools.partial(_na_kernel, grid_rows),
        out_shape=jax.ShapeDtypeStruct((n, NA_WIDTH), BF16),
        grid=(batch, nblk),
        in_specs=[blk(cur), blk(prev), blk(cur), blk(nxt), blk(prev), blk(cur), blk(nxt), ctx, ctx,
                  pl.BlockSpec(bias.shape, lambda b, j: (0, 0, 0))],
        out_specs=blk(cur),
        scratch_shapes=[pltpu.VMEM((3 * tq, NA_WIDTH), BF16), pltpu.VMEM((3 * tq, NA_WIDTH), BF16)],
        compiler_params=_cparams("parallel", "parallel"),
        name="na_attention",
    )(q, k, k, k, v, v, v, k_ctx, v_ctx, bias)


def _ctx_attn_kernel(q_ref, k_ref, v_ref, o_ref):
    q = q_ref[...]
    k = k_ref[...]
    v = v_ref[...]
    masks = _head_masks()
    acc = jnp.zeros(q.shape, F32)
    for h in range(NA_HEADS):
        s = _dot_nt(jnp.where(masks[h], q, jnp.zeros_like(q)), k)
        p = jnp.exp(s - jnp.max(s, axis=-1, keepdims=True))
        pv = _dot(p.astype(BF16), v) / jnp.sum(p, axis=-1, keepdims=True)
        acc = acc + jnp.where(masks[h], pv, 0.0)
    o_ref[...] = acc.astype(o_ref.dtype)


def ctx_attention(q, k, v, *, batch):
    n = q.shape[0]
    blk = pl.BlockSpec((n // batch, NA_WIDTH), lambda b: (b, 0))
    return pl.pallas_call(
        _ctx_attn_kernel,
        out_shape=jax.ShapeDtypeStruct((n, NA_WIDTH), BF16),
        grid=(batch,),
        in_specs=[blk, blk, blk],
        out_specs=blk,
        compiler_params=_cparams("parallel"),
        name="ctx_attention",
    )(q, k, v)


FT_N1 = 128
FT_T2_BLOCK = 8
FT_K1_BLOCK = 8


def _dft_cos_sin(n, scale=1.0):
    ang = 2.0 * np.pi * np.outer(np.arange(n), np.arange(n)) / n
    return np.cos(ang) * scale, np.sin(ang) * scale


def _channel_dft():
    c, s = _dft_cos_sin(FT_GROUP_DIM)
    eye = np.eye(FT_GROUPS)
    return np.concatenate([np.kron(eye, c), -np.kron(eye, s)], axis=1)


def _ft_stage1_kernel(u_ref, cs_ref, ff_ref, yr_ref, yi_ref):
    n1 = u_ref.shape[1]
    for s in range(u_ref.shape[2] // FT_WIDTH):
        lanes = slice(s * FT_WIDTH, (s + 1) * FT_WIDTH)
        z = _dot(u_ref[0, :, lanes], cs_ref[...])
        zz = jnp.concatenate([z[:, 0:FT_WIDTH], z[:, FT_WIDTH:]], axis=0).astype(BF16)
        y = _dot(ff_ref[...], zz)
        yr_ref[0, :, lanes] = y[0:n1]
        yi_ref[0, :, lanes] = y[n1:]


def _ft_stage2_kernel(yr_ref, yi_ref, twr_ref, twi_ref, g_ref, w_ref, o_ref):
    for i in range(yr_ref.shape[1]):
        yr = yr_ref[0, i]
        yi = yi_ref[0, i]
        tr = twr_ref[0, :, i:i + 1]
        ti = twi_ref[0, :, i:i + 1]
        yy = jnp.concatenate([yr * tr - yi * ti, yr * ti + yi * tr], axis=0).astype(BF16)
        xr = _dot(g_ref[...], yy)
        o_ref[0, :, i, :] = _dot(xr.astype(BF16), w_ref[...]).astype(o_ref.dtype)


def fourier_mix(u, fno_w, *, batch):
    n = u.shape[0]
    seq = n // batch
    n1 = FT_N1
    n2 = seq // n1
    tb = min(FT_T2_BLOCK, n2)
    norm = 1.0 / math.sqrt(seq * FT_GROUP_DIM)
    c1, s1 = _dft_cos_sin(n1)
    ff = jnp.asarray(np.block([[c1, s1], [-s1, c1]]), BF16)
    yr, yi = pl.pallas_call(
        _ft_stage1_kernel,
        out_shape=[jax.ShapeDtypeStruct((batch, n1, n2 * FT_WIDTH), F32)] * 2,
        grid=(batch, n2 // tb),
        in_specs=[pl.BlockSpec((1, n1, tb * FT_WIDTH), lambda b, j: (b, 0, j)),
                  pl.BlockSpec((FT_WIDTH, 2 * FT_WIDTH), lambda b, j: (0, 0)),
                  pl.BlockSpec((2 * n1, 2 * n1), lambda b, j: (0, 0))],
        out_specs=[pl.BlockSpec((1, n1, tb * FT_WIDTH), lambda b, j: (b, 0, j))] * 2,
        compiler_params=_cparams("parallel", "parallel"),
        name="fourier_stage1",
    )(u.reshape(batch, n1, n2 * FT_WIDTH), jnp.asarray(_channel_dft(), BF16), ff)
    kb = FT_K1_BLOCK
    ang = 2.0 * np.pi * np.outer(np.arange(n1), np.arange(n2)) / seq
    tw = lambda f: jnp.asarray(f(ang).reshape(n1 // kb, kb, n2).transpose(0, 2, 1), F32)
    c2, s2 = _dft_cos_sin(n2, norm)
    g = jnp.asarray(np.concatenate([c2, s2], axis=1), BF16)
    out = pl.pallas_call(
        _ft_stage2_kernel,
        out_shape=jax.ShapeDtypeStruct((batch, n2, n1, FT_WIDTH), BF16),
        grid=(batch, n1 // kb),
        in_specs=[pl.BlockSpec((1, kb, n2, FT_WIDTH), lambda b, j: (b, j, 0, 0))] * 2 + [
            pl.BlockSpec((1, n2, kb), lambda b, j: (j, 0, 0))] * 2 + [
            pl.BlockSpec((n2, 2 * n2), lambda b, j: (0, 0)), pl.BlockSpec((FT_WIDTH, FT_WIDTH), lambda b, j: (0, 0))],
        out_specs=pl.BlockSpec((1, n2, kb, FT_WIDTH), lambda b, j: (b, 0, j, 0)),
        compiler_params=_cparams("parallel", "parallel"),
        name="fourier_stage2",
    )(yr.reshape(batch, n1, n2, FT_WIDTH), yi.reshape(batch, n1, n2, FT_WIDTH),
      tw(np.cos), tw(lambda a: -np.sin(a)), g, fno_w.astype(BF16))
    return out.reshape(n, FT_WIDTH)


def _ft_direct_kernel(u_ref, cs_ref, g_ref, w_ref, o_ref):
    z = _dot(u_ref[...], cs_ref[...])
    zz = jnp.concatenate([z[:, 0:FT_WIDTH], z[:, FT_WIDTH:]], axis=0).astype(BF16)
    xr = _dot(g_ref[...], zz)
    o_ref[...] = _dot(xr.astype(BF16), w_ref[...]).astype(o_ref.dtype)


def fourier_mix_direct(u, fno_w, *, batch):
    n = u.shape[0]
    seq = n // batch
    c, s = _dft_cos_sin(seq, 1.0 / math.sqrt(seq * FT_GROUP_DIM))
    g = jnp.asarray(np.concatenate([c, s], axis=1), BF16)
    return pl.pallas_call(
        _ft_direct_kernel,
        out_shape=jax.ShapeDtypeStruct((n, FT_WIDTH), BF16),
        grid=(batch,),
        in_specs=[pl.BlockSpec((seq, FT_WIDTH), lambda b: (b, 0)),
                  pl.BlockSpec((FT_WIDTH, 2 * FT_WIDTH), lambda b: (0, 0)),
                  pl.BlockSpec((seq, 2 * seq), lambda b: (0, 0)),
                  pl.BlockSpec((FT_WIDTH, FT_WIDTH), lambda b: (0, 0))],
        out_specs=pl.BlockSpec((seq, FT_WIDTH), lambda b: (b, 0)),
        compiler_params=_cparams("parallel"),
        name="fourier_direct",
    )(u, jnp.asarray(_channel_dft(), BF16), g, fno_w.astype(BF16))


def _mod_row(mod_ref, rows_per_mod, fixed_row, tm):
    row = fixed_row if fixed_row is not None else (pl.program_id(0) * tm) // rows_per_mod
    return mod_ref[pl.ds(row, 1), :]


def _mix_out_kernel(rows_per_mod, fixed_row, route, o_ref, z_ref, ob_ref, oc_ref, x_ref, mod_ref, nw_ref, ones_ref,
                    w_ref, g_ref, *rest):
    if route:
        wr_ref, x_out, h_out, e_out, p_out = rest
    else:
        x_out, h_out = rest
    tm = x_ref.shape[0]
    m = _mod_row(mod_ref, rows_per_mod, fixed_row, tm)
    o = o_ref[0].astype(F32) + o_ref[1].astype(F32)
    oa = _head_rms(o, ones_ref[...], DN_HEAD_DIM, nw_ref[...]) * _silu(z_ref[...].astype(F32))
    y = (_dot(oa.astype(BF16), w_ref[0:DN_WIDTH, :]) + _dot(ob_ref[...], w_ref[DN_WIDTH:DN_WIDTH + NA_WIDTH, :])
         + _dot(oc_ref[...], w_ref[DN_WIDTH + NA_WIDTH:, :]))
    x = x_ref[...] + m[:, 2 * D_MODEL:3 * D_MODEL] * y
    x_out[...] = x
    h = _modulated_norm(x, g_ref[...], m[:, 3 * D_MODEL:4 * D_MODEL], m[:, 4 * D_MODEL:5 * D_MODEL])
    h_out[...] = h.astype(h_out.dtype)
    if route:
        logits = _dot_hi(h, wr_ref[...])
        idx = lax.broadcasted_iota(jnp.int32, logits.shape, 1)
        m1 = jnp.max(logits, axis=-1, keepdims=True)
        e1 = jnp.min(jnp.where(logits == m1, idx, N_EXPERTS), axis=-1, keepdims=True)
        rest_l = jnp.where(idx == e1, -jnp.inf, logits)
        m2 = jnp.max(rest_l, axis=-1, keepdims=True)
        e2 = jnp.min(jnp.where(rest_l == m2, idx, N_EXPERTS), axis=-1, keepdims=True)
        t = jnp.exp(m2 - m1)
        e_out[...] = jnp.concatenate([e1, e2], axis=1)
        p_out[...] = jnp.concatenate([1.0 / (1.0 + t), t / (1.0 + t)], axis=1)


def mix_out(o_dirs, z, ob, oc, x2d, mod, dn_norm_w, w_out, g_ffn, w_router=None, *, rows_per_mod, fixed_row=None,
            tm=512):
    n, d = x2d.shape
    tm = min(tm, n)
    route = w_router is not None
    full = lambda shape: pl.BlockSpec(shape, lambda i: (0,) * len(shape))
    rowblk = lambda w: pl.BlockSpec((tm, w), lambda i: (i, 0))
    ins = [o_dirs, z, ob, oc, x2d, mod, jnp.tile(dn_norm_w, DN_HEADS).reshape(1, DN_WIDTH),
           _group_ones(DN_WIDTH, DN_HEAD_DIM), w_out, g_ffn.reshape(1, d)]
    in_specs = [pl.BlockSpec((DN_DIRS, tm, DN_WIDTH), lambda i: (0, i, 0)), rowblk(DN_WIDTH), rowblk(NA_WIDTH),
                rowblk(FT_WIDTH), rowblk(d), full(mod.shape), full((1, DN_WIDTH)), full((DN_WIDTH, DN_WIDTH)),
                full(w_out.shape), full((1, d))]
    outs = [jax.ShapeDtypeStruct((n, d), F32), jax.ShapeDtypeStruct((n, d), F32 if route else BF16)]
    out_specs = [rowblk(d), rowblk(d)]
    if route:
        ins.append(w_router)
        in_specs.append(full(w_router.shape))
        outs += [jax.ShapeDtypeStruct((n, TOP_K), jnp.int32), jax.ShapeDtypeStruct((n, TOP_K), F32)]
        out_specs += [rowblk(TOP_K), rowblk(TOP_K)]
    return pl.pallas_call(
        functools.partial(_mix_out_kernel, rows_per_mod, fixed_row, route),
        out_shape=outs,
        grid=(n // tm,),
        in_specs=in_specs,
        out_specs=out_specs,
        compiler_params=_cparams("parallel"),
        name="mix_out_route" if route else "mix_out",
    )(*ins)


def _ffn_kernel(rows_per_mod, fixed_row, h_ref, x_ref, mod_ref, w1_ref, w3_ref, w2_ref, o_ref, acc_ref):
    f = pl.program_id(1)
    h = h_ref[...]
    part = _dot((_silu(_dot(h, w1_ref[...])) * _dot(h, w3_ref[...])).astype(BF16), w2_ref[...])

    @pl.when(f == 0)
    def _():
        acc_ref[...] = part

    @pl.when(f > 0)
    def _():
        acc_ref[...] += part

    @pl.when(f == pl.num_programs(1) - 1)
    def _():
        m = _mod_row(mod_ref, rows_per_mod, fixed_row, h_ref.shape[0])
        o_ref[...] = x_ref[...] + m[:, 5 * D_MODEL:6 * D_MODEL] * acc_ref[...]


def dense_ffn(h, x2d, mod, w1, w3, w2, *, rows_per_mod, fixed_row=None, tm=512, tf=1408):
    n, d = x2d.shape
    tm = min(tm, n)
    dff = w1.shape[1]
    return pl.pallas_call(
        functools.partial(_ffn_kernel, rows_per_mod, fixed_row),
        out_shape=jax.ShapeDtypeStruct((n, d), F32),
        grid=(n // tm, dff // tf),
        in_specs=[pl.BlockSpec((tm, d), lambda i, f: (i, 0)), pl.BlockSpec((tm, d), lambda i, f: (i, 0)),
                  pl.BlockSpec(mod.shape, lambda i, f: (0, 0)),
                  pl.BlockSpec((d, tf), lambda i, f: (0, f)), pl.BlockSpec((d, tf), lambda i, f: (0, f)),
                  pl.BlockSpec((tf, d), lambda i, f: (f, 0))],
        out_specs=pl.BlockSpec((tm, d), lambda i, f: (i, 0)),
        scratch_shapes=[pltpu.VMEM((tm, d), F32)],
        compiler_params=_cparams("parallel", "arbitrary"),
        name="dense_ffn",
    )(h, x2d, mod, w1, w3, w2)


MOE_ROWS = 1024
MOE_DMA_ROWS = 256


def _row_copy(src, src_row, dst, dst_row, sem):
    return pltpu.make_async_copy(src.at[pl.ds(src_row, 1)], dst.at[pl.ds(dst_row, 1)], sem)


def _moe_slots_kernel(dest_ref, slot_tok_ref):
    i = pl.program_id(0)
    n_assign = dest_ref.shape[2]

    @pl.when(i == 0)
    def _():
        def clear(s, carry):
            slot_tok_ref[s] = 0
            return carry
        lax.fori_loop(0, slot_tok_ref.shape[0], clear, 0, unroll=8)

    def put(a, carry):
        slot_tok_ref[dest_ref[0, 0, a]] = (i * n_assign + a) // TOP_K
        return carry

    lax.fori_loop(0, n_assign, put, 0, unroll=8)


def _moe_gather_kernel(na_ref, tok_ref, h_hbm, xb_ref, buf, sem):
    rows = buf.shape[0]
    active = pl.program_id(0) * rows < na_ref[0] * MOE_ROWS

    @pl.when(active)
    def _():
        def start(t, carry):
            _row_copy(h_hbm, tok_ref[0, 0, t], buf, t, sem).start()
            return carry

        def wait(t, carry):
            _row_copy(h_hbm, 0, buf, 0, sem).wait()
            return carry

        lax.fori_loop(0, rows, start, 0, unroll=8)
        lax.fori_loop(0, rows, wait, 0, unroll=8)
        xb_ref[...] = buf[...].astype(xb_ref.dtype)

    @pl.when(jnp.logical_not(active))
    def _():
        xb_ref[...] = jnp.zeros_like(xb_ref)


def _moe_ffn_kernel(be_ref, na_ref, x_ref, w1_ref, w3_ref, w2_ref, y_ref, acc_ref):
    i = pl.program_id(0)
    f = pl.program_id(1)
    last = f == pl.num_programs(1) - 1
    active = i < na_ref[0]

    @pl.when(active)
    def _():
        h = x_ref[...]
        part = _dot((_silu(_dot(h, w1_ref[0])) * _dot(h, w3_ref[0])).astype(BF16), w2_ref[0])

        @pl.when(f == 0)
        def _():
            acc_ref[...] = part

        @pl.when(f > 0)
        def _():
            acc_ref[...] += part

        @pl.when(last)
        def _():
            y_ref[...] = acc_ref[...]

    @pl.when(jnp.logical_and(jnp.logical_not(active), last))
    def _():
        y_ref[...] = jnp.zeros_like(y_ref)


def _moe_combine_kernel(rows_per_mod, dest_ref, p_ref, x_ref, mod_ref, yb_hbm, o_ref, buf, sem):
    tm = x_ref.shape[0]

    def start(t, carry):
        for j in range(TOP_K):
            _row_copy(yb_hbm, dest_ref[0, 0, TOP_K * t + j], buf.at[j], t, sem).start()
        return carry

    def wait(t, carry):
        for j in range(TOP_K):
            _row_copy(yb_hbm, 0, buf.at[j], 0, sem).wait()
        return carry

    lax.fori_loop(0, tm, start, 0, unroll=8)
    lax.fori_loop(0, tm, wait, 0, unroll=8)
    p = p_ref[...]
    y = p[:, 0:1] * buf[0] + p[:, 1:2] * buf[1]
    m = _mod_row(mod_ref, rows_per_mod, None, tm)
    o_ref[...] = x_ref[...] + m[:, 5 * D_MODEL:6 * D_MODEL] * y


def moe_ffn(h, x2d, mod, top_e, top_p, w1, w3, w2, *, rows_per_mod, tf=896):
    n, d = x2d.shape
    dff = w1.shape[2]
    n_assign = n * TOP_K
    n_blocks = -(-n_assign // MOE_ROWS) + N_EXPERTS
    n_slots = n_blocks * MOE_ROWS
    e_flat = top_e.reshape(n_assign)
    onehot = (e_flat[:, None] == jnp.arange(N_EXPERTS, dtype=jnp.int32)[None, :]).astype(jnp.int32)
    csum = jnp.cumsum(onehot, axis=0)
    rank = jnp.sum(onehot * csum, axis=1) - 1
    counts = csum[-1]
    padded = (counts + MOE_ROWS - 1) // MOE_ROWS * MOE_ROWS
    pad_end = jnp.cumsum(padded)
    dest = (jnp.sum(onehot * (pad_end - padded)[None, :], axis=1) + rank).astype(jnp.int32)
    n_active = (pad_end[-1] // MOE_ROWS).astype(jnp.int32).reshape(1)
    blk_ids = jnp.minimum(jnp.arange(n_blocks, dtype=jnp.int32), n_active[0] - 1)
    block_e = jnp.minimum(jnp.searchsorted(pad_end, blk_ids * MOE_ROWS, side='right'), N_EXPERTS - 1).astype(jnp.int32)

    tmd = min(MOE_DMA_ROWS, n)
    dest2d = dest.reshape(n // tmd, 1, TOP_K * tmd)
    dest_spec = pl.BlockSpec((1, 1, TOP_K * tmd), lambda i: (i, 0, 0), memory_space=pltpu.SMEM)
    hbm = pl.BlockSpec(memory_space=pl.ANY)
    slot_tok = pl.pallas_call(
        _moe_slots_kernel,
        out_shape=jax.ShapeDtypeStruct((n_slots,), jnp.int32),
        grid=(n // tmd,),
        in_specs=[dest_spec],
        out_specs=pl.BlockSpec(memory_space=pltpu.SMEM),
        compiler_params=_cparams("arbitrary"),
        name="moe_slots",
    )(dest2d)

    xb = pl.pallas_call(
        _moe_gather_kernel,
        out_shape=jax.ShapeDtypeStruct((n_slots, d), BF16),
        grid_spec=pltpu.PrefetchScalarGridSpec(
            num_scalar_prefetch=1,
            grid=(n_slots // tmd,),
            in_specs=[pl.BlockSpec((1, 1, tmd), lambda i, na: (i, 0, 0), memory_space=pltpu.SMEM), hbm],
            out_specs=pl.BlockSpec((tmd, d), lambda i, na: (i, 0)),
            scratch_shapes=[pltpu.VMEM((tmd, d), F32), pltpu.SemaphoreType.DMA]),
        compiler_params=_cparams("arbitrary"),
        name="moe_gather",
    )(n_active, slot_tok.reshape(n_slots // tmd, 1, tmd), h)

    yb = pl.pallas_call(
        _moe_ffn_kernel,
        out_shape=jax.ShapeDtypeStruct((n_slots, d), F32),
        grid_spec=pltpu.PrefetchScalarGridSpec(
            num_scalar_prefetch=2,
            grid=(n_blocks, dff // tf),
            in_specs=[pl.BlockSpec((MOE_ROWS, d), lambda i, f, be, na: (i, 0)),
                      pl.BlockSpec((1, d, tf), lambda i, f, be, na: (be[i], 0, f)),
                      pl.BlockSpec((1, d, tf), lambda i, f, be, na: (be[i], 0, f)),
                      pl.BlockSpec((1, tf, d), lambda i, f, be, na: (be[i], f, 0))],
            out_specs=pl.BlockSpec((MOE_ROWS, d), lambda i, f, be, na: (i, 0)),
            scratch_shapes=[pltpu.VMEM((MOE_ROWS, d), F32)]),
        compiler_params=_cparams("arbitrary", "arbitrary"),
        name="moe_expert_ffn",
    )(block_e, n_active, xb, w1, w3, w2)

    return pl.pallas_call(
        functools.partial(_moe_combine_kernel, rows_per_mod),
        out_shape=jax.ShapeDtypeStruct((n, d), F32),
        grid=(n // tmd,),
        in_specs=[dest_spec, pl.BlockSpec((tmd, TOP_K), lambda i: (i, 0)), pl.BlockSpec((tmd, d), lambda i: (i, 0)),
                  pl.BlockSpec(mod.shape, lambda i: (0, 0)), hbm],
        out_specs=pl.BlockSpec((tmd, d), lambda i: (i, 0)),
        scratch_shapes=[pltpu.VMEM((TOP_K, tmd, d), F32), pltpu.SemaphoreType.DMA],
        compiler_params=_cparams("arbitrary"),
        name="moe_combine",
    )(dest2d, top_p, x2d, mod, yb)


def _split_w_in(w_in_l):
    g0 = 4 * DN_WIDTH
    g1 = g0 + 2 * N_DH
    w_main = jnp.concatenate([w_in_l[:, :g0], w_in_l[:, g1:]], axis=1).astype(BF16)
    w_ab = w_in_l[:, g0:g1].astype(BF16)
    return w_main, w_ab, w_ab.T


def kernel(x, c, ctx, c_ctx, w_mod, b_mod, g_mix, g_ffn, w_in, dn_conv, dn_a_log, dn_dt_bias, dn_norm_w, na_q_norm,
           na_k_norm, na_rpb, fno_w, w_out, ffn_w1, ffn_w3, ffn_w2, moe_router, moe_w1, moe_w3, moe_w2):
    bsz, seq, d = x.shape
    ctx_len = ctx.shape[1]
    ctx_row = bsz
    cond = jnp.zeros((SUBLANES, d), F32).at[:bsz].set(c).at[ctx_row].set(c_ctx)
    mods = adaln_mod(cond, w_mod, b_mod)
    x2 = x.reshape(bsz * seq, d)
    c2 = ctx.reshape(bsz * ctx_len, d)
    zero_state = jnp.zeros((bsz, DN_DIRS, DN_HEADS, DN_HEAD_DIM, DN_HEAD_DIM), F32)

    for layer in range(DEPTH):
        need_ctx = layer < DEPTH - 1
        mod = mods[layer]
        w_main, w_ab, w_abt = _split_w_in(w_in[layer])
        proj = functools.partial(in_proj, mod=mod, g=g_mix[layer], w_main=w_main, w_ab=w_ab, w_abt=w_abt,
                                 q_norm=na_q_norm[layer], k_norm=na_k_norm[layer])
        qkv, z, ab, abt, nq, nk, nv, ft = proj(x2, rows_per_mod=seq)
        qkv_c, z_c, ab_c, abt_c, nq_c, nk_c, nv_c, ft_c = proj(c2, rows_per_mod=ctx_len, fixed_row=ctx_row)

        prep = functools.partial(dn_prep, conv_w=dn_conv[layer], a_log=dn_a_log[layer], dt_bias=dn_dt_bias[layer],
                                 batch=bsz)
        q_c, k_c, v_c, gcc_c, beta_c, gcr_c = prep(qkv_c, ab_c, abt_c)
        o_c, s_ctx = dn_scan(q_c, k_c, v_c, gcc_c, beta_c, gcr_c, zero_state, batch=bsz)
        q_l, k_l, v_l, gcc_l, beta_l, gcr_l = prep(qkv, ab, abt)
        o_l, _ = dn_scan(q_l, k_l, v_l, gcc_l, beta_l, gcr_l, s_ctx, batch=bsz)

        ob = na_attention(nq, nk, nv, nk_c, nv_c, na_bias_table(na_rpb[layer]), batch=bsz)
        oc = fourier_mix(ft, fno_w[layer], batch=bsz)

        w_out_l = w_out[layer].astype(BF16)
        j = layer // 2
        if layer % 2 == 0:
            x2, h2 = mix_out(o_l, z, ob, oc, x2, mod, dn_norm_w[layer], w_out_l, g_ffn[layer], rows_per_mod=seq)
            ffn_w = (ffn_w1[j].astype(BF16), ffn_w3[j].astype(BF16), ffn_w2[j].astype(BF16))
            x2 = dense_ffn(h2, x2, mod, *ffn_w, rows_per_mod=seq)
        else:
            x2, h2, top_e, top_p = mix_out(o_l, z, ob, oc, x2, mod, dn_norm_w[layer], w_out_l, g_ffn[layer],
                                           moe_router[j], rows_per_mod=seq)
            x2 = moe_ffn(h2, x2, mod, top_e, top_p, moe_w1[j].astype(BF16), moe_w3[j].astype(BF16),
                         moe_w2[j].astype(BF16), rows_per_mod=seq)
        if need_ctx:
            ob_c = ctx_attention(nq_c, nk_c, nv_c, batch=bsz)
            oc_c = fourier_mix_direct(ft_c, fno_w[layer], batch=bsz)
            assert layer % 2 == 0, "context tokens only pass through dense layers at this depth"
            c2, hc2 = mix_out(o_c, z_c, ob_c, oc_c, c2, mod, dn_norm_w[layer], w_out_l, g_ffn[layer],
                              rows_per_mod=ctx_len, fixed_row=ctx_row)
            c2 = dense_ffn(hc2, c2, mod, *ffn_w, rows_per_mod=ctx_len, fixed_row=ctx_row)
    return x2.reshape(bsz, seq, d)
```

```python
import functools
import math

import jax
import jax.numpy as jnp
import numpy as np
from jax import lax
from jax.experimental import pallas as pl
from jax.experimental.pallas import tpu as pltpu

D_MODEL = 1024
DEPTH = 2
GRID_W = 64
DN_HEADS = 4
DN_HEAD_DIM = 128
DN_WIDTH = DN_HEADS * DN_HEAD_DIM
DN_CONV = 3
DN_CHUNK = 64
DN_DIRS = 2
NA_HEADS = 4
NA_HEAD_DIM = 64
NA_WIDTH = NA_HEADS * NA_HEAD_DIM
NA_WIN_H = 8
NA_WIN_W = 16
FT_GROUPS = 4
FT_GROUP_DIM = 64
FT_WIDTH = FT_GROUPS * FT_GROUP_DIM
N_EXPERTS = 8
TOP_K = 2
N_MOD = 6
EPS = 1e-6

SUBLANES = 8
LANES = 128
VMEM_LIMIT = 48 * 1024 * 1024

BF16 = jnp.bfloat16
F32 = jnp.float32
N_DH = DN_DIRS * DN_HEADS


def _cparams(*sem):
    return pltpu.CompilerParams(dimension_semantics=sem, vmem_limit_bytes=VMEM_LIMIT)


def _silu(x):
    return x * jax.nn.sigmoid(x)


def _dot(a, b):
    return jnp.dot(a, b, preferred_element_type=F32)


def _dot_nt(a, b):
    return lax.dot_general(a, b, (((1,), (1,)), ((), ())), preferred_element_type=F32)


def _dot_tn(a, b):
    return lax.dot_general(a, b, (((0,), (0,)), ((), ())), preferred_element_type=F32)


def _group_ones(width, group):
    idx = np.arange(width) // group
    return jnp.asarray((idx[:, None] == idx[None, :]).astype(np.float32), BF16)


def _mod_kernel(c_ref, w_ref, b_ref, o_ref):
    s = _silu(c_ref[...])
    o_ref[0] = jnp.dot(s, w_ref[0], preferred_element_type=F32, precision=lax.Precision.HIGHEST) + b_ref[0]


def adaln_mod(cond_rows, w_mod, b_mod, *, tn=1536):
    depth, d, n = w_mod.shape
    rows = cond_rows.shape[0]
    return pl.pallas_call(
        _mod_kernel,
        out_shape=jax.ShapeDtypeStruct((depth, rows, n), F32),
        grid=(depth, n // tn),
        in_specs=[pl.BlockSpec((rows, d), lambda l, j: (0, 0)),
                  pl.BlockSpec((1, d, tn), lambda l, j: (l, 0, j)),
                  pl.BlockSpec((1, 1, tn), lambda l, j: (l, 0, j))],
        out_specs=pl.BlockSpec((1, rows, tn), lambda l, j: (l, 0, j)),
        compiler_params=_cparams("arbitrary", "arbitrary"),
        name="adaln_mod",
    )(cond_rows, w_mod, b_mod.reshape(depth, 1, n))


C_QKV = 0
C_Z = 3 * DN_WIDTH
C_NQ = C_Z + DN_WIDTH
C_NK = C_NQ + NA_WIDTH
C_NV = C_NK + NA_WIDTH
C_FT = C_NV + NA_WIDTH
C_END = C_FT + FT_WIDTH


def _modulated_norm(x, g, shift, scale):
    y = x * lax.rsqrt(jnp.mean(x * x, axis=-1, keepdims=True) + EPS)
    return y * g * (1.0 + scale) + shift


def _head_rms(x, ones, width, gain):
    ss = _dot((x * x).astype(BF16), ones)
    return x * lax.rsqrt(ss * (1.0 / width) + EPS) * gain


def _in_proj_kernel(rows_per_mod, fixed_row, x_ref, mod_ref, g_ref, w_ref, wab_ref, wabt_ref, qn_ref, kn_ref,
                    ones_ref, qkv_ref, z_ref, ab_ref, abt_ref, nq_ref, nk_ref, nv_ref, ft_ref):
    tm = x_ref.shape[0]
    row = fixed_row if fixed_row is not None else (pl.program_id(0) * tm) // rows_per_mod
    m = mod_ref[pl.ds(row, 1), :]
    shift, scale = m[:, 0:D_MODEL], m[:, D_MODEL:2 * D_MODEL]
    h = _modulated_norm(x_ref[...], g_ref[...], shift, scale).astype(BF16)
    qkv_ref[...] = _dot(h, w_ref[:, C_QKV:C_Z]).astype(qkv_ref.dtype)
    z_ref[...] = _dot(h, w_ref[:, C_Z:C_NQ]).astype(z_ref.dtype)
    ones = ones_ref[...]
    nq = _dot(h, w_ref[:, C_NQ:C_NK])
    nq_ref[...] = (_head_rms(nq, ones, NA_HEAD_DIM, qn_ref[...]) * (NA_HEAD_DIM ** -0.5)).astype(nq_ref.dtype)
    nk = _dot(h, w_ref[:, C_NK:C_NV])
    nk_ref[...] = _head_rms(nk, ones, NA_HEAD_DIM, kn_ref[...]).astype(nk_ref.dtype)
    nv_ref[...] = _dot(h, w_ref[:, C_NV:C_FT]).astype(nv_ref.dtype)
    ft_ref[...] = _dot(h, w_ref[:, C_FT:C_END]).astype(ft_ref.dtype)
    ab_ref[...] = _dot(h, wab_ref[...])
    abt_ref[...] = _dot_nt(wabt_ref[...], h)


def in_proj(x2d, mod, g, w_main, w_ab, w_abt, q_norm, k_norm, *, rows_per_mod, fixed_row=None, tm=512):
    n, d = x2d.shape
    tm = min(tm, n)
    nab = w_ab.shape[1]
    full = lambda shape: pl.BlockSpec(shape, lambda i: (0,) * len(shape))
    rowblk = lambda w: pl.BlockSpec((tm, w), lambda i: (i, 0))
    outs = [jax.ShapeDtypeStruct((n, 3 * DN_WIDTH), BF16), jax.ShapeDtypeStruct((n, DN_WIDTH), BF16),
            jax.ShapeDtypeStruct((n, nab), F32), jax.ShapeDtypeStruct((nab, n), F32),
            jax.ShapeDtypeStruct((n, NA_WIDTH), BF16), jax.ShapeDtypeStruct((n, NA_WIDTH), BF16),
            jax.ShapeDtypeStruct((n, NA_WIDTH), BF16), jax.ShapeDtypeStruct((n, FT_WIDTH), BF16)]
    return pl.pallas_call(
        functools.partial(_in_proj_kernel, rows_per_mod, fixed_row),
        out_shape=outs,
        grid=(n // tm,),
        in_specs=[rowblk(d), full(mod.shape), full((1, d)), full(w_main.shape), full(w_ab.shape), full(w_abt.shape),
                  full((1, NA_WIDTH)), full((1, NA_WIDTH)), full((NA_WIDTH, NA_WIDTH))],
        out_specs=[rowblk(3 * DN_WIDTH), rowblk(DN_WIDTH), rowblk(nab), pl.BlockSpec((nab, tm), lambda i: (0, i)),
                   rowblk(NA_WIDTH), rowblk(NA_WIDTH), rowblk(NA_WIDTH), rowblk(FT_WIDTH)],
        compiler_params=_cparams("parallel"),
        name="in_proj",
    )(x2d, mod, g.reshape(1, d), w_main, w_ab, w_abt,
      jnp.tile(q_norm, NA_HEADS).reshape(1, NA_WIDTH), jnp.tile(k_norm, NA_HEADS).reshape(1, NA_WIDTH),
      _group_ones(NA_WIDTH, NA_HEAD_DIM))


QKV_HALO = 16


def _softplus(x):
    return jnp.maximum(x, 0.0) + jnp.log1p(jnp.exp(-jnp.abs(x)))


def _tri(n, lower):
    r = lax.broadcasted_iota(jnp.int32, (n, n), 0)
    c = lax.broadcasted_iota(jnp.int32, (n, n), 1)
    return jnp.where((r >= c) if lower else (r <= c), 1.0, 0.0).astype(F32)


def _dot_hi(a, b):
    return jnp.dot(a, b, preferred_element_type=F32, precision=lax.Precision.HIGHEST)


def _dn_prep_kernel(x_ref, xp_ref, xn_ref, ab_ref, abt_ref, cw_ref, alr_ref, dtr_ref, alc_ref, dtc_ref, ones_ref,
                    q_ref, k_ref, v_ref, gcc_ref, beta_ref, gcr_ref):
    i = pl.program_id(1)
    tl = x_ref.shape[0]
    x = x_ref[...].astype(F32)
    prev = jnp.where(i > 0, xp_ref[QKV_HALO - 1:QKV_HALO, :].astype(F32), 0.0)
    nxt = jnp.where(i < pl.num_programs(1) - 1, xn_ref[0:1, :].astype(F32), 0.0)
    rows = lax.broadcasted_iota(jnp.int32, x.shape, 0)
    xm1 = jnp.where(rows == 0, prev, pltpu.roll(x, 1, axis=0))
    xp1 = jnp.where(rows == tl - 1, nxt, pltpu.roll(x, tl - 1, axis=0))
    y = _silu(xm1 * cw_ref[0:1, :] + x * cw_ref[1:2, :] + xp1 * cw_ref[2:3, :])
    ones = ones_ref[...]
    q = y[:, 0:DN_WIDTH]
    k = y[:, DN_WIDTH:2 * DN_WIDTH]
    q_ref[...] = (q * lax.rsqrt(_dot((q * q).astype(BF16), ones) + EPS) * (DN_HEAD_DIM ** -0.5)).astype(q_ref.dtype)
    k_ref[...] = (k * lax.rsqrt(_dot((k * k).astype(BF16), ones) + EPS)).astype(k_ref.dtype)
    v_ref[...] = y[:, 2 * DN_WIDTH:].astype(v_ref.dtype)

    ab = ab_ref[...]
    g_col = -jnp.exp(alr_ref[...]) * _softplus(ab[:, 0:N_DH] + dtr_ref[...])
    beta = jax.nn.sigmoid(ab[:, N_DH:2 * N_DH])
    beta_ref[0] = beta[:, 0:DN_HEADS]
    beta_ref[1] = beta[:, DN_HEADS:]
    g_row = -jnp.exp(alc_ref[...]) * _softplus(abt_ref[0:N_DH, :] + dtc_ref[...])
    lo, up = _tri(DN_CHUNK, True), _tri(DN_CHUNK, False)
    sub = lax.broadcasted_iota(jnp.int32, (N_DH, DN_CHUNK), 0)
    for c in range(tl // DN_CHUNK):
        sl = slice(c * DN_CHUNK, (c + 1) * DN_CHUNK)
        gc = g_col[sl, :]
        gcc_ref[0, sl, :] = _dot_hi(lo, gc)[:, 0:DN_HEADS]
        gcc_ref[1, sl, :] = _dot_hi(up, gc)[:, DN_HEADS:]
        gr = g_row[:, sl]
        cs = jnp.where(sub < DN_HEADS, _dot_hi(gr, up), _dot_hi(gr, lo))
        gcr_ref[0, 0, c] = cs[0:DN_HEADS]
        gcr_ref[1, 0, c] = cs[DN_HEADS:]


def dn_prep(qkv, ab, abt, conv_w, a_log, dt_bias, *, batch, tl=512):
    n = qkv.shape[0]
    seq = n // batch
    tl = min(tl, seq)
    nblk = seq // tl
    hb = tl // QKV_HALO
    nc_all = seq // DN_CHUNK
    full = lambda shape: pl.BlockSpec(shape, lambda b, i: (0,) * len(shape))
    rowblk = lambda w: pl.BlockSpec((tl, w), lambda b, i: (b * nblk + i, 0))
    outs = [jax.ShapeDtypeStruct((n, DN_WIDTH), BF16)] * 3 + [
        jax.ShapeDtypeStruct((DN_DIRS, n, DN_HEADS), F32), jax.ShapeDtypeStruct((DN_DIRS, n, DN_HEADS), F32),
        jax.ShapeDtypeStruct((DN_DIRS, batch, nc_all, DN_HEADS, DN_CHUNK), F32)]
    a_log = a_log.reshape(N_DH)
    dt_bias = dt_bias.reshape(N_DH)
    return pl.pallas_call(
        _dn_prep_kernel,
        out_shape=outs,
        grid=(batch, nblk),
        in_specs=[rowblk(3 * DN_WIDTH),
                  pl.BlockSpec((QKV_HALO, 3 * DN_WIDTH), lambda b, i: (jnp.maximum((b * nblk + i) * hb - 1, 0), 0)),
                  pl.BlockSpec((QKV_HALO, 3 * DN_WIDTH),
                               lambda b, i: (jnp.minimum((b * nblk + i + 1) * hb, n // QKV_HALO - 1), 0)),
                  rowblk(2 * N_DH),
                  pl.BlockSpec((2 * N_DH, tl), lambda b, i: (0, b * nblk + i)),
                  full((DN_CONV, 3 * DN_WIDTH)), full((1, N_DH)), full((1, N_DH)), full((N_DH, 1)), full((N_DH, 1)),
                  full((DN_WIDTH, DN_WIDTH))],
        out_specs=[rowblk(DN_WIDTH)] * 3 + [
            pl.BlockSpec((DN_DIRS, tl, DN_HEADS), lambda b, i: (0, b * nblk + i, 0)),
            pl.BlockSpec((DN_DIRS, tl, DN_HEADS), lambda b, i: (0, b * nblk + i, 0)),
            pl.BlockSpec((DN_DIRS, 1, tl // DN_CHUNK, DN_HEADS, DN_CHUNK), lambda b, i: (0, b, i, 0, 0))],
        compiler_params=_cparams("parallel", "parallel"),
        name="dn_prep",
    )(qkv, qkv, qkv, ab, abt, conv_w, a_log.reshape(1, N_DH), dt_bias.reshape(1, N_DH),
      a_log.reshape(N_DH, 1), dt_bias.reshape(N_DH, 1), _group_ones(DN_WIDTH, DN_HEAD_DIM))


DN_LOCAL_GROUP = 4


def _unit_tri_inverses(mats, block):
    n = mats[0].shape[0]
    r = lax.broadcasted_iota(jnp.int32, (n, n), 0)
    c = lax.broadcasted_iota(jnp.int32, (n, n), 1)
    eye = jnp.where(r == c, 1.0, 0.0)
    ds = [eye - jnp.where(r // 2 == c // 2, a, 0.0) for a in mats]
    size = 2
    while size < block:
        off = jnp.logical_and(r // (2 * size) == c // (2 * size), r // size != c // size)
        d16s = [d.astype(BF16) for d in ds]
        ts = [_dot(d16, jnp.where(off, a, 0.0).astype(BF16)) for d16, a in zip(d16s, mats)]
        ds = [d - _dot(t.astype(BF16), d16) for d, t, d16 in zip(ds, ts, d16s)]
        size *= 2
    return ds


def _dn_scan_kernel(q_ref, k_ref, v_ref, gcc_ref, beta_ref, gcr_ref, s0_ref, o_ref, s_ref,
                    rg_scr, n_scr, h_scr, eg_scr):
    d = pl.program_id(1)
    i = pl.program_id(2)
    nc = rg_scr.shape[0]
    cs = DN_CHUNK
    hd = DN_HEAD_DIM
    fwd = d == 0

    @pl.when(i == 0)
    def _():
        s_ref[...] = s0_ref[...]

    hc = DN_HEADS * cs
    r = lax.broadcasted_iota(jnp.int32, (hc, hc), 0)
    c = lax.broadcasted_iota(jnp.int32, (hc, hc), 1)
    same_head = (r // cs) == (c // cs)
    rel = (r % cs - c % cs) * jnp.where(fwd, 1, -1)
    incl = jnp.logical_and(same_head, rel >= 0)
    strict = jnp.logical_and(same_head, rel > 0)
    last = jnp.where(fwd, cs - 1, 0)

    def load(ci):
        row0 = pl.multiple_of(ci * cs, cs)
        stack = lambda ref: jnp.concatenate(
            [ref[pl.ds(row0, cs), h * hd:(h + 1) * hd] for h in range(DN_HEADS)], axis=0)
        gate = lambda ref, rows: jnp.concatenate([ref[0, rows, h:h + 1] for h in range(DN_HEADS)], axis=0)
        q, k, v = stack(q_ref), stack(k_ref), stack(v_ref).astype(F32)
        gcol = gate(gcc_ref, pl.ds(row0, cs))
        bcol = gate(beta_ref, pl.ds(row0, cs))
        grows = gcr_ref[0, 0, ci]
        grow = jnp.concatenate([grows[h:h + 1, :] for h in range(DN_HEADS)], axis=1)
        glast_h = [gcc_ref[0, pl.ds(row0 + last, 1), h:h + 1] for h in range(DN_HEADS)]
        glast = jnp.concatenate([jnp.broadcast_to(g, (cs, 1)) for g in glast_h], axis=0)
        decay = jnp.where(incl, jnp.exp(jnp.where(incl, gcol - grow, 0.0)), 0.0)
        kf = k.astype(F32)
        kb = kf * bcol
        eg = jnp.exp(gcol)
        a = jnp.where(strict, _dot_nt(kb.astype(BF16), k) * decay, 0.0)
        rhs = jnp.concatenate([v * bcol, kb * eg], axis=1).astype(BF16)
        qk = jnp.where(incl, _dot_nt(q, k) * decay, 0.0).astype(BF16)
        kd = (kf * jnp.exp(glast - gcol)).astype(BF16)
        return dict(ci=ci, a=a, rhs=rhs, qk=qk, kd=kd, qe=q.astype(F32) * eg, glast_h=glast_h)

    def local_group(j, carry):
        chunks = [load(DN_LOCAL_GROUP * j + g) for g in range(DN_LOCAL_GROUP)]
        ts = _unit_tri_inverses([ch["a"] for ch in chunks], cs)
        uws = [_dot(t.astype(BF16), ch["rhs"]).astype(BF16) for t, ch in zip(ts, chunks)]
        hgs = [_dot(ch["qk"], uw) for ch, uw in zip(chunks, uws)]
        for ch, uw, hg in zip(chunks, uws, hgs):
            ci = ch["ci"]
            g_mat = (ch["qe"] - hg[:, hd:]).astype(BF16)
            for h in range(DN_HEADS):
                rows = slice(h * cs, (h + 1) * cs)
                nr = _dot_tn(ch["kd"][rows], uw[rows])
                n_scr[ci, h] = nr[:, 0:hd]
                h_scr[ci, h] = hg[rows, 0:hd]
                rg_scr[ci, h, 0:hd, :] = nr[:, hd:].astype(BF16)
                rg_scr[ci, h, hd:hd + cs, :] = g_mat[rows]
                eg_scr[ci, h] = jnp.broadcast_to(jnp.exp(ch["glast_h"][h]), (1, hd))
        return carry

    lax.fori_loop(0, nc // DN_LOCAL_GROUP, local_group, 0)

    def step(cc, carry):
        ci = jnp.where(fwd, cc, nc - 1 - cc)
        row0 = pl.multiple_of(ci * cs, cs)
        for h in range(DN_HEADS):
            s = s_ref[0, 0, h]
            y = _dot(rg_scr[ci, h], s.astype(BF16))
            s_ref[0, 0, h] = s * eg_scr[ci, h] - y[0:hd] + n_scr[ci, h]
            o_ref[0, pl.ds(row0, cs), h * hd:(h + 1) * hd] = (y[hd:hd + cs] + h_scr[ci, h]).astype(o_ref.dtype)
        return carry

    lax.fori_loop(0, nc, step, 0)


def dn_scan(q, k, v, gcc, beta, gcr, s0, *, batch, tl=512):
    n = q.shape[0]
    seq = n // batch
    tl = min(tl, seq)
    nblk = seq // tl
    nc = tl // DN_CHUNK
    blk = lambda b, d, i: b * nblk + jnp.where(d == 0, i, nblk - 1 - i)
    chunk_blk = lambda d, i: jnp.where(d == 0, i, nblk - 1 - i)
    rowspec = pl.BlockSpec((tl, DN_WIDTH), lambda b, d, i: (blk(b, d, i), 0))
    gate = pl.BlockSpec((1, tl, DN_HEADS), lambda b, d, i: (d, blk(b, d, i), 0))
    state = pl.BlockSpec((1, 1, DN_HEADS, DN_HEAD_DIM, DN_HEAD_DIM), lambda b, d, i: (b, d, 0, 0, 0))
    return pl.pallas_call(
        _dn_scan_kernel,
        out_shape=[jax.ShapeDtypeStruct((DN_DIRS, n, DN_WIDTH), BF16), jax.ShapeDtypeStruct(s0.shape, F32)],
        grid=(batch, DN_DIRS, nblk),
        in_specs=[rowspec, rowspec, rowspec, gate, gate,
                  pl.BlockSpec((1, 1, nc, DN_HEADS, DN_CHUNK), lambda b, d, i: (d, b, chunk_blk(d, i), 0, 0)),
                  state],
        out_specs=[pl.BlockSpec((1, tl, DN_WIDTH), lambda b, d, i: (d, blk(b, d, i), 0)), state],
        scratch_shapes=[pltpu.VMEM((nc, DN_HEADS, DN_HEAD_DIM + DN_CHUNK, DN_HEAD_DIM), BF16),
                        pltpu.VMEM((nc, DN_HEADS, DN_HEAD_DIM, DN_HEAD_DIM), F32),
                        pltpu.VMEM((nc, DN_HEADS, DN_CHUNK, DN_HEAD_DIM), F32),
                        pltpu.VMEM((nc, DN_HEADS, 1, DN_HEAD_DIM), F32)],
        compiler_params=_cparams("parallel", "parallel", "arbitrary"),
        name="dn_scan",
    )(q, k, v, gcc, beta, gcr, s0)


NA_ROWS_PER_STEP = 8
NA_ROW_GROUP = 2
MASKED = -1e30


def na_bias_table(rpb):
    qc = np.arange(GRID_W)[:, None]
    kc = np.arange(GRID_W)[None, :]
    c0 = np.clip(qc - NA_WIN_W // 2, 0, GRID_W - NA_WIN_W)
    valid = (kc >= c0) & (kc < c0 + NA_WIN_W)
    rel_c = kc - qc + NA_WIN_W - 1
    case = np.arange(NA_WIN_H)[:, None]
    i = np.arange(NA_WIN_H)[None, :]
    rel_r = i + NA_WIN_H - 1 - case
    pick_r = (rel_r[:, :, None] == np.arange(2 * NA_WIN_H - 1)).astype(np.float32)
    pick_c = ((rel_c[:, :, None] == np.arange(2 * NA_WIN_W - 1)) & valid[:, :, None]).astype(np.float32)
    tab = jnp.einsum('hrc,xir,qkc->xhqik', rpb, pick_r, pick_c, precision=lax.Precision.HIGHEST)
    tab = jnp.where(valid[None, None, :, None, :], tab, MASKED)
    return tab.reshape(NA_WIN_H, NA_HEADS * GRID_W, NA_WIN_H * GRID_W)


def _head_masks():
    lane = lax.broadcasted_iota(jnp.int32, (1, NA_WIDTH), 1)
    return [(lane // NA_HEAD_DIM == h) for h in range(NA_HEADS)]


def _na_kernel(grid_rows, q_ref, kp_ref, kc_ref, kn_ref, vp_ref, vc_ref, vn_ref, kx_ref, vx_ref, bias_ref, o_ref,
               kwin, vwin):
    j = pl.program_id(1)
    tq = q_ref.shape[0]
    nkeys = NA_WIN_H * GRID_W
    kwin[0:tq] = kp_ref[...]
    kwin[tq:2 * tq] = kc_ref[...]
    kwin[2 * tq:3 * tq] = kn_ref[...]
    vwin[0:tq] = vp_ref[...]
    vwin[tq:2 * tq] = vc_ref[...]
    vwin[2 * tq:3 * tq] = vn_ref[...]
    kx = kx_ref[...]
    vx = vx_ref[...]
    hq = NA_HEADS * GRID_W
    own = (lax.broadcasted_iota(jnp.int32, (hq, NA_WIDTH), 0) // GRID_W
           == lax.broadcasted_iota(jnp.int32, (hq, NA_WIDTH), 1) // NA_HEAD_DIM)
    for g in range(NA_ROWS_PER_STEP // NA_ROW_GROUP):
        rls = [g * NA_ROW_GROUP + t for t in range(NA_ROW_GROUP)]
        scores = []
        for rl in rls:
            r = j * NA_ROWS_PER_STEP + rl
            r0 = jnp.clip(r - NA_WIN_H // 2, 0, grid_rows - NA_WIN_H)
            start = pl.multiple_of((r0 - (j - 1) * NA_ROWS_PER_STEP) * GRID_W, GRID_W)
            q = q_ref[rl * GRID_W:(rl + 1) * GRID_W, :]
            qs = jnp.where(own, jnp.concatenate([q] * NA_HEADS, axis=0), jnp.zeros((hq, NA_WIDTH), q.dtype))
            s_loc = _dot_nt(qs, kwin[pl.ds(start, nkeys), :]) + bias_ref[r - r0]
            scores.append((s_loc, _dot_nt(qs, kx), start))
        probs = []
        for s_loc, s_ctx, start in scores:
            m = jnp.maximum(jnp.max(s_loc, axis=-1, keepdims=True), jnp.max(s_ctx, axis=-1, keepdims=True))
            p_loc = jnp.exp(s_loc - m)
            p_ctx = jnp.exp(s_ctx - m)
            denom = jnp.sum(p_loc, axis=-1, keepdims=True) + jnp.sum(p_ctx, axis=-1, keepdims=True)
            probs.append((p_loc.astype(BF16), p_ctx.astype(BF16), denom, start))
        for rl, (p_loc, p_ctx, denom, start) in zip(rls, probs):
            pv = _dot(p_loc, vwin[pl.ds(start, nkeys), :]) + _dot(p_ctx, vx)
            pv = jnp.where(own, pv / denom, 0.0)
            out = pv[0:GRID_W]
            for h in range(1, NA_HEADS):
                out = out + pv[h * GRID_W:(h + 1) * GRID_W]
            o_ref[rl * GRID_W:(rl + 1) * GRID_W, :] = out.astype(o_ref.dtype)


def na_attention(q, k, v, k_ctx, v_ctx, bias, *, batch):
    n = q.shape[0]
    seq = n // batch
    ctx_len = k_ctx.shape[0] // batch
    grid_rows = seq // GRID_W
    tq = NA_ROWS_PER_STEP * GRID_W
    nblk = seq // tq
    cur = lambda b, j: (b * nblk + j, 0)
    prev = lambda b, j: (b * nblk + jnp.maximum(j - 1, 0), 0)
    nxt = lambda b, j: (b * nblk + jnp.minimum(j + 1, nblk - 1), 0)
    blk = lambda f: pl.BlockSpec((tq, NA_WIDTH), f)
    ctx = pl.BlockSpec((ctx_len, NA_WIDTH), lambda b, j: (b, 0))
    return pl.pallas_call(
        functools.partial(_na_kernel, grid_rows),
        out_shape=jax.ShapeDtypeStruct((n, NA_WIDTH), BF16),
        grid=(batch, nblk),
        in_specs=[blk(cur), blk(prev), blk(cur), blk(nxt), blk(prev), blk(cur), blk(nxt), ctx, ctx,
                  pl.BlockSpec(bias.shape, lambda b, j: (0, 0, 0))],
        out_specs=blk(cur),
        scratch_shapes=[pltpu.VMEM((3 * tq, NA_WIDTH), BF16), pltpu.VMEM((3 * tq, NA_WIDTH), BF16)],
        compiler_params=_cparams("parallel", "parallel"),
        name="na_attention",
    )(q, k, k, k, v, v, v, k_ctx, v_ctx, bias)


def _ctx_attn_kernel(q_ref, k_ref, v_ref, o_ref):
    q = q_ref[...]
    k = k_ref[...]
    v = v_ref[...]
    masks = _head_masks()
    acc = jnp.zeros(q.shape, F32)
    for h in range(NA_HEADS):
        s = _dot_nt(jnp.where(masks[h], q, jnp.zeros_like(q)), k)
        p = jnp.exp(s - jnp.max(s, axis=-1, keepdims=True))
        pv = _dot(p.astype(BF16), v) / jnp.sum(p, axis=-1, keepdims=True)
        acc = acc + jnp.where(masks[h], pv, 0.0)
    o_ref[...] = acc.astype(o_ref.dtype)


def ctx_attention(q, k, v, *, batch):
    n = q.shape[0]
    blk = pl.BlockSpec((n // batch, NA_WIDTH), lambda b: (b, 0))
    return pl.pallas_call(
        _ctx_attn_kernel,
        out_shape=jax.ShapeDtypeStruct((n, NA_WIDTH), BF16),
        grid=(batch,),
        in_specs=[blk, blk, blk],
        out_specs=blk,
        compiler_params=_cparams("parallel"),
        name="ctx_attention",
    )(q, k, v)


FT_N1 = 128
FT_T2_BLOCK = 8
FT_K1_BLOCK = 8


def _dft_cos_sin(n, scale=1.0):
    ang = 2.0 * np.pi * np.outer(np.arange(n), np.arange(n)) / n
    return np.cos(ang) * scale, np.sin(ang) * scale


def _channel_dft():
    c, s = _dft_cos_sin(FT_GROUP_DIM)
    eye = np.eye(FT_GROUPS)
    return np.concatenate([np.kron(eye, c), -np.kron(eye, s)], axis=1)


def _ft_stage1_kernel(u_ref, cs_ref, ff_ref, yr_ref, yi_ref):
    n1 = u_ref.shape[1]
    for s in range(u_ref.shape[2] // FT_WIDTH):
        lanes = slice(s * FT_WIDTH, (s + 1) * FT_WIDTH)
        z = _dot(u_ref[0, :, lanes], cs_ref[...])
        zz = jnp.concatenate([z[:, 0:FT_WIDTH], z[:, FT_WIDTH:]], axis=0).astype(BF16)
        y = _dot(ff_ref[...], zz)
        yr_ref[0, :, lanes] = y[0:n1]
        yi_ref[0, :, lanes] = y[n1:]


def _ft_stage2_kernel(yr_ref, yi_ref, twr_ref, twi_ref, g_ref, w_ref, o_ref):
    for i in range(yr_ref.shape[1]):
        yr = yr_ref[0, i]
        yi = yi_ref[0, i]
        tr = twr_ref[0, :, i:i + 1]
        ti = twi_ref[0, :, i:i + 1]
        yy = jnp.concatenate([yr * tr - yi * ti, yr * ti + yi * tr], axis=0).astype(BF16)
        xr = _dot(g_ref[...], yy)
        o_ref[0, :, i, :] = _dot(xr.astype(BF16), w_ref[...]).astype(o_ref.dtype)


def fourier_mix(u, fno_w, *, batch):
    n = u.shape[0]
    seq = n // batch
    n1 = FT_N1
    n2 = seq // n1
    tb = min(FT_T2_BLOCK, n2)
    norm = 1.0 / math.sqrt(seq * FT_GROUP_DIM)
    c1, s1 = _dft_cos_sin(n1)
    ff = jnp.asarray(np.block([[c1, s1], [-s1, c1]]), BF16)
    yr, yi = pl.pallas_call(
        _ft_stage1_kernel,
        out_shape=[jax.ShapeDtypeStruct((batch, n1, n2 * FT_WIDTH), F32)] * 2,
        grid=(batch, n2 // tb),
        in_specs=[pl.BlockSpec((1, n1, tb * FT_WIDTH), lambda b, j: (b, 0, j)),
                  pl.BlockSpec((FT_WIDTH, 2 * FT_WIDTH), lambda b, j: (0, 0)),
                  pl.BlockSpec((2 * n1, 2 * n1), lambda b, j: (0, 0))],
        out_specs=[pl.BlockSpec((1, n1, tb * FT_WIDTH), lambda b, j: (b, 0, j))] * 2,
        compiler_params=_cparams("parallel", "parallel"),
        name="fourier_stage1",
    )(u.reshape(batch, n1, n2 * FT_WIDTH), jnp.asarray(_channel_dft(), BF16), ff)
    kb = FT_K1_BLOCK
    ang = 2.0 * np.pi * np.outer(np.arange(n1), np.arange(n2)) / seq
    tw = lambda f: jnp.asarray(f(ang).reshape(n1 // kb, kb, n2).transpose(0, 2, 1), F32)
    c2, s2 = _dft_cos_sin(n2, norm)
    g = jnp.asarray(np.concatenate([c2, s2], axis=1), BF16)
    out = pl.pallas_call(
        _ft_stage2_kernel,
        out_shape=jax.ShapeDtypeStruct((batch, n2, n1, FT_WIDTH), BF16),
        grid=(batch, n1 // kb),
        in_specs=[pl.BlockSpec((1, kb, n2, FT_WIDTH), lambda b, j: (b, j, 0, 0))] * 2 + [
            pl.BlockSpec((1, n2, kb), lambda b, j: (j, 0, 0))] * 2 + [
            pl.BlockSpec((n2, 2 * n2), lambda b, j: (0, 0)), pl.BlockSpec((FT_WIDTH, FT_WIDTH), lambda b, j: (0, 0))],
        out_specs=pl.BlockSpec((1, n2, kb, FT_WIDTH), lambda b, j: (b, 0, j, 0)),
        compiler_params=_cparams("parallel", "parallel"),
        name="fourier_stage2",
    )(yr.reshape(batch, n1, n2, FT_WIDTH), yi.reshape(batch, n1, n2, FT_WIDTH),
      tw(np.cos), tw(lambda a: -np.sin(a)), g, fno_w.astype(BF16))
    return out.reshape(n, FT_WIDTH)


def _ft_direct_kernel(u_ref, cs_ref, g_ref, w_ref, o_ref):
    z = _dot(u_ref[...], cs_ref[...])
    zz = jnp.concatenate([z[:, 0:FT_WIDTH], z[:, FT_WIDTH:]], axis=0).astype(BF16)
    xr = _dot(g_ref[...], zz)
    o_ref[...] = _dot(xr.astype(BF16), w_ref[...]).astype(o_ref.dtype)


def fourier_mix_direct(u, fno_w, *, batch):
    n = u.shape[0]
    seq = n // batch
    c, s = _dft_cos_sin(seq, 1.0 / math.sqrt(seq * FT_GROUP_DIM))
    g = jnp.asarray(np.concatenate([c, s], axis=1), BF16)
    return pl.pallas_call(
        _ft_direct_kernel,
        out_shape=jax.ShapeDtypeStruct((n, FT_WIDTH), BF16),
        grid=(batch,),
        in_specs=[pl.BlockSpec((seq, FT_WIDTH), lambda b: (b, 0)),
                  pl.BlockSpec((FT_WIDTH, 2 * FT_WIDTH), lambda b: (0, 0)),
                  pl.BlockSpec((seq, 2 * seq), lambda b: (0, 0)),
                  pl.BlockSpec((FT_WIDTH, FT_WIDTH), lambda b: (0, 0))],
        out_specs=pl.BlockSpec((seq, FT_WIDTH), lambda b: (b, 0)),
        compiler_params=_cparams("parallel"),
        name="fourier_direct",
    )(u, jnp.asarray(_channel_dft(), BF16), g, fno_w.astype(BF16))


def _mod_row(mod_ref, rows_per_mod, fixed_row, tm):
    row = fixed_row if fixed_row is not None else (pl.program_id(0) * tm) // rows_per_mod
    return mod_ref[pl.ds(row, 1), :]


def _mix_out_kernel(rows_per_mod, fixed_row, route, o_ref, z_ref, ob_ref, oc_ref, x_ref, mod_ref, nw_ref, ones_ref,
                    w_ref, g_ref, *rest):
    if route:
        wr_ref, x_out, h_out, e_out, p_out = rest
    else:
        x_out, h_out = rest
    tm = x_ref.shape[0]
    m = _mod_row(mod_ref, rows_per_mod, fixed_row, tm)
    o = o_ref[0].astype(F32) + o_ref[1].astype(F32)
    oa = _head_rms(o, ones_ref[...], DN_HEAD_DIM, nw_ref[...]) * _silu(z_ref[...].astype(F32))
    y = (_dot(oa.astype(BF16), w_ref[0:DN_WIDTH, :]) + _dot(ob_ref[...], w_ref[DN_WIDTH:DN_WIDTH + NA_WIDTH, :])
         + _dot(oc_ref[...], w_ref[DN_WIDTH + NA_WIDTH:, :]))
    x = x_ref[...] + m[:, 2 * D_MODEL:3 * D_MODEL] * y
    x_out[...] = x
    h = _modulated_norm(x, g_ref[...], m[:, 3 * D_MODEL:4 * D_MODEL], m[:, 4 * D_MODEL:5 * D_MODEL])
    h_out[...] = h.astype(h_out.dtype)
    if route:
        h_hi = h.astype(BF16)
        h_lo = (h - h_hi.astype(F32)).astype(BF16)
        both = _dot(h_hi, wr_ref[...]) + _dot(h_lo, wr_ref[...])
        logits = both[:, 0:N_EXPERTS] + both[:, N_EXPERTS:]
        idx = lax.broadcasted_iota(jnp.int32, logits.shape, 1)
        m1 = jnp.max(logits, axis=-1, keepdims=True)
        e1 = jnp.min(jnp.where(logits == m1, idx, N_EXPERTS), axis=-1, keepdims=True)
        rest_l = jnp.where(idx == e1, -jnp.inf, logits)
        m2 = jnp.max(rest_l, axis=-1, keepdims=True)
        e2 = jnp.min(jnp.where(rest_l == m2, idx, N_EXPERTS), axis=-1, keepdims=True)
        t = jnp.exp(m2 - m1)
        e_out[...] = jnp.concatenate([e1, e2], axis=1)
        p_out[...] = jnp.concatenate([1.0 / (1.0 + t), t / (1.0 + t)], axis=1)


def mix_out(o_dirs, z, ob, oc, x2d, mod, dn_norm_w, w_out, g_ffn, w_router=None, *, rows_per_mod, fixed_row=None,
            tm=512):
    n, d = x2d.shape
    tm = min(tm, n)
    route = w_router is not None
    full = lambda shape: pl.BlockSpec(shape, lambda i: (0,) * len(shape))
    rowblk = lambda w: pl.BlockSpec((tm, w), lambda i: (i, 0))
    ins = [o_dirs, z, ob, oc, x2d, mod, jnp.tile(dn_norm_w, DN_HEADS).reshape(1, DN_WIDTH),
           _group_ones(DN_WIDTH, DN_HEAD_DIM), w_out, g_ffn.reshape(1, d)]
    in_specs = [pl.BlockSpec((DN_DIRS, tm, DN_WIDTH), lambda i: (0, i, 0)), rowblk(DN_WIDTH), rowblk(NA_WIDTH),
                rowblk(FT_WIDTH), rowblk(d), full(mod.shape), full((1, DN_WIDTH)), full((DN_WIDTH, DN_WIDTH)),
                full(w_out.shape), full((1, d))]
    outs = [jax.ShapeDtypeStruct((n, d), F32), jax.ShapeDtypeStruct((n, d), F32 if route else BF16)]
    out_specs = [rowblk(d), rowblk(d)]
    if route:
        w_hi = w_router.astype(BF16)
        w_lo = (w_router - w_hi.astype(F32)).astype(BF16)
        ins.append(jnp.concatenate([w_hi, w_lo], axis=1))
        in_specs.append(full((d, 2 * N_EXPERTS)))
        outs += [jax.ShapeDtypeStruct((n, TOP_K), jnp.int32), jax.ShapeDtypeStruct((n, TOP_K), F32)]
        out_specs += [rowblk(TOP_K), rowblk(TOP_K)]
    return pl.pallas_call(
        functools.partial(_mix_out_kernel, rows_per_mod, fixed_row, route),
        out_shape=outs,
        grid=(n // tm,),
        in_specs=in_specs,
        out_specs=out_specs,
        compiler_params=_cparams("parallel"),
        name="mix_out_route" if route else "mix_out",
    )(*ins)


def _ffn_kernel(rows_per_mod, fixed_row, h_ref, x_ref, mod_ref, w1_ref, w3_ref, w2_ref, o_ref, acc_ref):
    f = pl.program_id(1)
    h = h_ref[...]
    part = _dot((_silu(_dot(h, w1_ref[...])) * _dot(h, w3_ref[...])).astype(BF16), w2_ref[...])

    last = f == pl.num_programs(1) - 1

    @pl.when(f == 0)
    def _():
        acc_ref[...] = part

    @pl.when(jnp.logical_and(f > 0, jnp.logical_not(last)))
    def _():
        acc_ref[...] += part

    @pl.when(last)
    def _():
        m = _mod_row(mod_ref, rows_per_mod, fixed_row, h_ref.shape[0])
        o_ref[...] = x_ref[...] + m[:, 5 * D_MODEL:6 * D_MODEL] * (acc_ref[...] + part)


def dense_ffn(h, x2d, mod, w1, w3, w2, *, rows_per_mod, fixed_row=None, tm=512, tf=1408):
    n, d = x2d.shape
    tm = min(tm, n)
    dff = w1.shape[1]
    assert dff // tf >= 2, "the first and last hidden tiles are handled by different branches"
    return pl.pallas_call(
        functools.partial(_ffn_kernel, rows_per_mod, fixed_row),
        out_shape=jax.ShapeDtypeStruct((n, d), F32),
        grid=(n // tm, dff // tf),
        in_specs=[pl.BlockSpec((tm, d), lambda i, f: (i, 0)), pl.BlockSpec((tm, d), lambda i, f: (i, 0)),
                  pl.BlockSpec(mod.shape, lambda i, f: (0, 0)),
                  pl.BlockSpec((d, tf), lambda i, f: (0, f)), pl.BlockSpec((d, tf), lambda i, f: (0, f)),
                  pl.BlockSpec((tf, d), lambda i, f: (f, 0))],
        out_specs=pl.BlockSpec((tm, d), lambda i, f: (i, 0)),
        scratch_shapes=[pltpu.VMEM((tm, d), F32)],
        compiler_params=_cparams("parallel", "arbitrary"),
        name="dense_ffn",
    )(h, x2d, mod, w1, w3, w2)


MOE_ROWS = 1024
MOE_DMA_ROWS = 256


def _row_copy(src, src_row, dst, dst_row, sem):
    return pltpu.make_async_copy(src.at[pl.ds(src_row, 1)], dst.at[pl.ds(dst_row, 1)], sem)


def _moe_scatter_kernel(dest_ref, h_ref, xb_in, xb_out, sem):
    del xb_in
    tm = h_ref.shape[0]

    def start(t, carry):
        for j in range(TOP_K):
            _row_copy(h_ref, t, xb_out, dest_ref[0, 0, TOP_K * t + j], sem).start()
        return carry

    def wait(t, carry):
        for j in range(TOP_K):
            _row_copy(h_ref, 0, xb_out, 0, sem).wait()
        return carry

    lax.fori_loop(0, tm, start, 0, unroll=8)
    lax.fori_loop(0, tm, wait, 0, unroll=8)


def _moe_ffn_kernel(be_ref, na_ref, x_ref, w1_ref, w3_ref, w2_ref, y_ref, acc_ref):
    i = pl.program_id(0)
    f = pl.program_id(1)
    last = f == pl.num_programs(1) - 1
    active = i < na_ref[0]

    @pl.when(active)
    def _():
        h = x_ref[...].astype(BF16)
        part = _dot((_silu(_dot(h, w1_ref[0])) * _dot(h, w3_ref[0])).astype(BF16), w2_ref[0])

        @pl.when(f == 0)
        def _():
            acc_ref[...] = part

        @pl.when(jnp.logical_and(f > 0, jnp.logical_not(last)))
        def _():
            acc_ref[...] += part

        @pl.when(last)
        def _():
            y_ref[...] = acc_ref[...] + part

    @pl.when(jnp.logical_and(jnp.logical_not(active), last))
    def _():
        y_ref[...] = jnp.zeros_like(y_ref)


def _moe_combine_kernel(rows_per_mod, dest_ref, p_ref, x_ref, mod_ref, yb_hbm, o_ref, buf, sem):
    tm = x_ref.shape[0]

    def start(t, carry):
        for j in range(TOP_K):
            _row_copy(yb_hbm, dest_ref[0, 0, TOP_K * t + j], buf.at[j], t, sem).start()
        return carry

    def wait(t, carry):
        for j in range(TOP_K):
            _row_copy(yb_hbm, 0, buf.at[j], 0, sem).wait()
        return carry

    lax.fori_loop(0, tm, start, 0, unroll=8)
    lax.fori_loop(0, tm, wait, 0, unroll=8)
    p = p_ref[...]
    y = p[:, 0:1] * buf[0] + p[:, 1:2] * buf[1]
    m = _mod_row(mod_ref, rows_per_mod, None, tm)
    o_ref[...] = x_ref[...] + m[:, 5 * D_MODEL:6 * D_MODEL] * y


def moe_ffn(h, x2d, mod, top_e, top_p, w1, w3, w2, *, rows_per_mod, tf=896):
    n, d = x2d.shape
    dff = w1.shape[2]
    assert dff // tf >= 2, "the first and last hidden tiles are handled by different branches"
    n_assign = n * TOP_K
    n_blocks = -(-n_assign // MOE_ROWS) + N_EXPERTS
    n_slots = n_blocks * MOE_ROWS
    e_flat = top_e.reshape(n_assign)
    onehot = (e_flat[:, None] == jnp.arange(N_EXPERTS, dtype=jnp.int32)[None, :]).astype(jnp.int32)
    csum = jnp.cumsum(onehot, axis=0)
    rank = jnp.sum(onehot * csum, axis=1) - 1
    counts = csum[-1]
    padded = (counts + MOE_ROWS - 1) // MOE_ROWS * MOE_ROWS
    pad_end = jnp.cumsum(padded)
    dest = (jnp.sum(onehot * (pad_end - padded)[None, :], axis=1) + rank).astype(jnp.int32)
    n_active = (pad_end[-1] // MOE_ROWS).astype(jnp.int32).reshape(1)
    blk_ids = jnp.minimum(jnp.arange(n_blocks, dtype=jnp.int32), n_active[0] - 1)
    block_e = jnp.minimum(jnp.searchsorted(pad_end, blk_ids * MOE_ROWS, side='right'), N_EXPERTS - 1).astype(jnp.int32)

    tmd = min(MOE_DMA_ROWS, n)
    dest2d = dest.reshape(n // tmd, 1, TOP_K * tmd)
    dest_spec = pl.BlockSpec((1, 1, TOP_K * tmd), lambda i: (i, 0, 0), memory_space=pltpu.SMEM)
    hbm = pl.BlockSpec(memory_space=pl.ANY)
    xb = pl.pallas_call(
        _moe_scatter_kernel,
        out_shape=jax.ShapeDtypeStruct((n_slots, d), F32),
        grid=(n // tmd,),
        in_specs=[dest_spec, pl.BlockSpec((tmd, d), lambda i: (i, 0)), hbm],
        out_specs=hbm,
        scratch_shapes=[pltpu.SemaphoreType.DMA],
        input_output_aliases={2: 0},
        compiler_params=_cparams("arbitrary"),
        name="moe_scatter",
    )(dest2d, h, jnp.zeros((n_slots, d), F32))

    yb = pl.pallas_call(
        _moe_ffn_kernel,
        out_shape=jax.ShapeDtypeStruct((n_slots, d), F32),
        grid_spec=pltpu.PrefetchScalarGridSpec(
            num_scalar_prefetch=2,
            grid=(n_blocks, dff // tf),
            in_specs=[pl.BlockSpec((MOE_ROWS, d), lambda i, f, be, na: (i, 0)),
                      pl.BlockSpec((1, d, tf), lambda i, f, be, na: (be[i], 0, f)),
                      pl.BlockSpec((1, d, tf), lambda i, f, be, na: (be[i], 0, f)),
                      pl.BlockSpec((1, tf, d), lambda i, f, be, na: (be[i], f, 0))],
            out_specs=pl.BlockSpec((MOE_ROWS, d), lambda i, f, be, na: (i, 0)),
            scratch_shapes=[pltpu.VMEM((MOE_ROWS, d), F32)]),
        compiler_params=_cparams("arbitrary", "arbitrary"),
        name="moe_expert_ffn",
    )(block_e, n_active, xb, w1, w3, w2)

    return pl.pallas_call(
        functools.partial(_moe_combine_kernel, rows_per_mod),
        out_shape=jax.ShapeDtypeStruct((n, d), F32),
        grid=(n // tmd,),
        in_specs=[dest_spec, pl.BlockSpec((tmd, TOP_K), lambda i: (i, 0)), pl.BlockSpec((tmd, d), lambda i: (i, 0)),
                  pl.BlockSpec(mod.shape, lambda i: (0, 0)), hbm],
        out_specs=pl.BlockSpec((tmd, d), lambda i: (i, 0)),
        scratch_shapes=[pltpu.VMEM((TOP_K, tmd, d), F32), pltpu.SemaphoreType.DMA],
        compiler_params=_cparams("arbitrary"),
        name="moe_combine",
    )(dest2d, top_p, x2d, mod, yb)


def _split_w_in(w_in_l):
    g0 = 4 * DN_WIDTH
    g1 = g0 + 2 * N_DH
    w_main = jnp.concatenate([w_in_l[:, :g0], w_in_l[:, g1:]], axis=1).astype(BF16)
    w_ab = w_in_l[:, g0:g1].astype(BF16)
    return w_main, w_ab, w_ab.T


def kernel(x, c, ctx, c_ctx, w_mod, b_mod, g_mix, g_ffn, w_in, dn_conv, dn_a_log, dn_dt_bias, dn_norm_w, na_q_norm,
           na_k_norm, na_rpb, fno_w, w_out, ffn_w1, ffn_w3, ffn_w2, moe_router, moe_w1, moe_w3, moe_w2):
    bsz, seq, d = x.shape
    ctx_len = ctx.shape[1]
    ctx_row = bsz
    cond = jnp.zeros((SUBLANES, d), F32).at[:bsz].set(c).at[ctx_row].set(c_ctx)
    mods = adaln_mod(cond, w_mod, b_mod)
    x2 = x.reshape(bsz * seq, d)
    c2 = ctx.reshape(bsz * ctx_len, d)
    zero_state = jnp.zeros((bsz, DN_DIRS, DN_HEADS, DN_HEAD_DIM, DN_HEAD_DIM), F32)

    for layer in range(DEPTH):
        need_ctx = layer < DEPTH - 1
        mod = mods[layer]
        w_main, w_ab, w_abt = _split_w_in(w_in[layer])
        proj = functools.partial(in_proj, mod=mod, g=g_mix[layer], w_main=w_main, w_ab=w_ab, w_abt=w_abt,
                                 q_norm=na_q_norm[layer], k_norm=na_k_norm[layer])
        qkv, z, ab, abt, nq, nk, nv, ft = proj(x2, rows_per_mod=seq)
        qkv_c, z_c, ab_c, abt_c, nq_c, nk_c, nv_c, ft_c = proj(c2, rows_per_mod=ctx_len, fixed_row=ctx_row)

        prep = functools.partial(dn_prep, conv_w=dn_conv[layer], a_log=dn_a_log[layer], dt_bias=dn_dt_bias[layer],
                                 batch=bsz)
        q_c, k_c, v_c, gcc_c, beta_c, gcr_c = prep(qkv_c, ab_c, abt_c)
        o_c, s_ctx = dn_scan(q_c, k_c, v_c, gcc_c, beta_c, gcr_c, zero_state, batch=bsz)
        q_l, k_l, v_l, gcc_l, beta_l, gcr_l = prep(qkv, ab, abt)
        o_l, _ = dn_scan(q_l, k_l, v_l, gcc_l, beta_l, gcr_l, s_ctx, batch=bsz)

        ob = na_attention(nq, nk, nv, nk_c, nv_c, na_bias_table(na_rpb[layer]), batch=bsz)
        oc = fourier_mix(ft, fno_w[layer], batch=bsz)

        w_out_l = w_out[layer].astype(BF16)
        j = layer // 2
        if layer % 2 == 0:
            x2, h2 = mix_out(o_l, z, ob, oc, x2, mod, dn_norm_w[layer], w_out_l, g_ffn[layer], rows_per_mod=seq)
            ffn_w = (ffn_w1[j].astype(BF16), ffn_w3[j].astype(BF16), ffn_w2[j].astype(BF16))
            x2 = dense_ffn(h2, x2, mod, *ffn_w, rows_per_mod=seq)
        else:
            x2, h2, top_e, top_p = mix_out(o_l, z, ob, oc, x2, mod, dn_norm_w[layer], w_out_l, g_ffn[layer],
                                           moe_router[j], rows_per_mod=seq)
            x2 = moe_ffn(h2, x2, mod, top_e, top_p, moe_w1[j].astype(BF16), moe_w3[j].astype(BF16),
                         moe_w2[j].astype(BF16), rows_per_mod=seq)
        if need_ctx:
            ob_c = ctx_attention(nq_c, nk_c, nv_c, batch=bsz)
            oc_c = fourier_mix_direct(ft_c, fno_w[layer], batch=bsz)
            assert layer % 2 == 0, "context tokens only pass through dense layers at this depth"
            c2, hc2 = mix_out(o_c, z_c, ob_c, oc_c, c2, mod, dn_norm_w[layer], w_out_l, g_ffn[layer],
                              rows_per_mod=ctx_len, fixed_row=ctx_row)
            c2 = dense_ffn(hc2, c2, mod, *ffn_w, rows_per_mod=ctx_len, fixed_row=ctx_row)
    return x2.reshape(bsz, seq, d)
```

```python
import functools
import math

import jax
import jax.numpy as jnp
import numpy as np
from jax import lax
from jax.experimental import pallas as pl
from jax.experimental.pallas import tpu as pltpu

D_MODEL = 1024
DEPTH = 2
GRID_W = 64
DN_HEADS = 4
DN_HEAD_DIM = 128
DN_WIDTH = DN_HEADS * DN_HEAD_DIM
DN_CONV = 3
DN_CHUNK = 64
DN_DIRS = 2
NA_HEADS = 4
NA_HEAD_DIM = 64
NA_WIDTH = NA_HEADS * NA_HEAD_DIM
NA_WIN_H = 8
NA_WIN_W = 16
FT_GROUPS = 4
FT_GROUP_DIM = 64
FT_WIDTH = FT_GROUPS * FT_GROUP_DIM
N_EXPERTS = 8
TOP_K = 2
N_MOD = 6
EPS = 1e-6

SUBLANES = 8
LANES = 128
VMEM_LIMIT = 48 * 1024 * 1024

BF16 = jnp.bfloat16
F32 = jnp.float32
N_DH = DN_DIRS * DN_HEADS


def _cparams(*sem):
    return pltpu.CompilerParams(dimension_semantics=sem, vmem_limit_bytes=VMEM_LIMIT)


def _silu(x):
    return x * jax.nn.sigmoid(x)


def _dot(a, b):
    return jnp.dot(a, b, preferred_element_type=F32)


def _dot_nt(a, b):
    return lax.dot_general(a, b, (((1,), (1,)), ((), ())), preferred_element_type=F32)


def _dot_tn(a, b):
    return lax.dot_general(a, b, (((0,), (0,)), ((), ())), preferred_element_type=F32)


def _group_ones(width, group):
    idx = np.arange(width) // group
    return jnp.asarray((idx[:, None] == idx[None, :]).astype(np.float32), BF16)


def _mod_kernel(c_ref, w_ref, b_ref, o_ref):
    s = _silu(c_ref[...])
    o_ref[0] = jnp.dot(s, w_ref[0], preferred_element_type=F32, precision=lax.Precision.HIGHEST) + b_ref[0]


def adaln_mod(cond_rows, w_mod, b_mod, *, tn=1536):
    depth, d, n = w_mod.shape
    rows = cond_rows.shape[0]
    return pl.pallas_call(
        _mod_kernel,
        out_shape=jax.ShapeDtypeStruct((depth, rows, n), F32),
        grid=(depth, n // tn),
        in_specs=[pl.BlockSpec((rows, d), lambda l, j: (0, 0)),
                  pl.BlockSpec((1, d, tn), lambda l, j: (l, 0, j)),
                  pl.BlockSpec((1, 1, tn), lambda l, j: (l, 0, j))],
        out_specs=pl.BlockSpec((1, rows, tn), lambda l, j: (l, 0, j)),
        compiler_params=_cparams("arbitrary", "arbitrary"),
        name="adaln_mod",
    )(cond_rows, w_mod, b_mod.reshape(depth, 1, n))


C_QKV = 0
C_Z = 3 * DN_WIDTH
C_NQ = C_Z + DN_WIDTH
C_NK = C_NQ + NA_WIDTH
C_NV = C_NK + NA_WIDTH
C_FT = C_NV + NA_WIDTH
C_END = C_FT + FT_WIDTH


def _modulated_norm(x, g, shift, scale):
    y = x * lax.rsqrt(jnp.mean(x * x, axis=-1, keepdims=True) + EPS)
    return y * g * (1.0 + scale) + shift


def _head_rms(x, ones, width, gain):
    ss = _dot((x * x).astype(BF16), ones)
    return x * lax.rsqrt(ss * (1.0 / width) + EPS) * gain


def _in_proj_kernel(rows_per_mod, fixed_row, x_ref, mod_ref, g_ref, w_ref, wab_ref, wabt_ref, qn_ref, kn_ref,
                    ones_ref, qkv_ref, z_ref, ab_ref, abt_ref, nq_ref, nk_ref, nv_ref, ft_ref):
    tm = x_ref.shape[0]
    row = fixed_row if fixed_row is not None else (pl.program_id(0) * tm) // rows_per_mod
    m = mod_ref[pl.ds(row, 1), :]
    shift, scale = m[:, 0:D_MODEL], m[:, D_MODEL:2 * D_MODEL]
    h = _modulated_norm(x_ref[...], g_ref[...], shift, scale).astype(BF16)
    qkv_ref[...] = _dot(h, w_ref[:, C_QKV:C_Z]).astype(qkv_ref.dtype)
    z_ref[...] = _dot(h, w_ref[:, C_Z:C_NQ]).astype(z_ref.dtype)
    ones = ones_ref[...]
    nq = _dot(h, w_ref[:, C_NQ:C_NK])
    nq_ref[...] = (_head_rms(nq, ones, NA_HEAD_DIM, qn_ref[...]) * (NA_HEAD_DIM ** -0.5)).astype(nq_ref.dtype)
    nk = _dot(h, w_ref[:, C_NK:C_NV])
    nk_ref[...] = _head_rms(nk, ones, NA_HEAD_DIM, kn_ref[...]).astype(nk_ref.dtype)
    nv_ref[...] = _dot(h, w_ref[:, C_NV:C_FT]).astype(nv_ref.dtype)
    ft_ref[...] = _dot(h, w_ref[:, C_FT:C_END]).astype(ft_ref.dtype)
    ab_ref[...] = _dot(h, wab_ref[...])
    abt_ref[...] = _dot_nt(wabt_ref[...], h)


def in_proj(x2d, mod, g, w_main, w_ab, w_abt, q_norm, k_norm, *, rows_per_mod, fixed_row=None, tm=512):
    n, d = x2d.shape
    tm = min(tm, n)
    nab = w_ab.shape[1]
    full = lambda shape: pl.BlockSpec(shape, lambda i: (0,) * len(shape))
    rowblk = lambda w: pl.BlockSpec((tm, w), lambda i: (i, 0))
    outs = [jax.ShapeDtypeStruct((n, 3 * DN_WIDTH), BF16), jax.ShapeDtypeStruct((n, DN_WIDTH), BF16),
            jax.ShapeDtypeStruct((n, nab), F32), jax.ShapeDtypeStruct((nab, n), F32),
            jax.ShapeDtypeStruct((n, NA_WIDTH), BF16), jax.ShapeDtypeStruct((n, NA_WIDTH), BF16),
            jax.ShapeDtypeStruct((n, NA_WIDTH), BF16), jax.ShapeDtypeStruct((n, FT_WIDTH), BF16)]
    return pl.pallas_call(
        functools.partial(_in_proj_kernel, rows_per_mod, fixed_row),
        out_shape=outs,
        grid=(n // tm,),
        in_specs=[rowblk(d), full(mod.shape), full((1, d)), full(w_main.shape), full(w_ab.shape), full(w_abt.shape),
                  full((1, NA_WIDTH)), full((1, NA_WIDTH)), full((NA_WIDTH, NA_WIDTH))],
        out_specs=[rowblk(3 * DN_WIDTH), rowblk(DN_WIDTH), rowblk(nab), pl.BlockSpec((nab, tm), lambda i: (0, i)),
                   rowblk(NA_WIDTH), rowblk(NA_WIDTH), rowblk(NA_WIDTH), rowblk(FT_WIDTH)],
        compiler_params=_cparams("parallel"),
        name="in_proj",
    )(x2d, mod, g.reshape(1, d), w_main, w_ab, w_abt,
      jnp.tile(q_norm, NA_HEADS).reshape(1, NA_WIDTH), jnp.tile(k_norm, NA_HEADS).reshape(1, NA_WIDTH),
      _group_ones(NA_WIDTH, NA_HEAD_DIM))


QKV_HALO = 16


def _softplus(x):
    return jnp.maximum(x, 0.0) + jnp.log1p(jnp.exp(-jnp.abs(x)))


def _tri(n, lower):
    r = lax.broadcasted_iota(jnp.int32, (n, n), 0)
    c = lax.broadcasted_iota(jnp.int32, (n, n), 1)
    return jnp.where((r >= c) if lower else (r <= c), 1.0, 0.0).astype(F32)


def _dot_hi(a, b):
    return jnp.dot(a, b, preferred_element_type=F32, precision=lax.Precision.HIGHEST)


def _dn_prep_kernel(x_ref, xp_ref, xn_ref, ab_ref, abt_ref, cw_ref, alr_ref, dtr_ref, alc_ref, dtc_ref, ones_ref,
                    q_ref, k_ref, v_ref, gcc_ref, beta_ref, gcr_ref):
    i = pl.program_id(1)
    tl = x_ref.shape[0]
    x = x_ref[...].astype(F32)
    prev = jnp.where(i > 0, xp_ref[QKV_HALO - 1:QKV_HALO, :].astype(F32), 0.0)
    nxt = jnp.where(i < pl.num_programs(1) - 1, xn_ref[0:1, :].astype(F32), 0.0)
    rows = lax.broadcasted_iota(jnp.int32, x.shape, 0)
    xm1 = jnp.where(rows == 0, prev, pltpu.roll(x, 1, axis=0))
    xp1 = jnp.where(rows == tl - 1, nxt, pltpu.roll(x, tl - 1, axis=0))
    y = _silu(xm1 * cw_ref[0:1, :] + x * cw_ref[1:2, :] + xp1 * cw_ref[2:3, :])
    ones = ones_ref[...]
    q = y[:, 0:DN_WIDTH]
    k = y[:, DN_WIDTH:2 * DN_WIDTH]
    q_ref[...] = (q * lax.rsqrt(_dot((q * q).astype(BF16), ones) + EPS) * (DN_HEAD_DIM ** -0.5)).astype(q_ref.dtype)
    k_ref[...] = (k * lax.rsqrt(_dot((k * k).astype(BF16), ones) + EPS)).astype(k_ref.dtype)
    v_ref[...] = y[:, 2 * DN_WIDTH:].astype(v_ref.dtype)

    ab = ab_ref[...]
    g_col = -jnp.exp(alr_ref[...]) * _softplus(ab[:, 0:N_DH] + dtr_ref[...])
    beta = jax.nn.sigmoid(ab[:, N_DH:2 * N_DH])
    beta_ref[0] = beta[:, 0:DN_HEADS]
    beta_ref[1] = beta[:, DN_HEADS:]
    g_row = -jnp.exp(alc_ref[...]) * _softplus(abt_ref[0:N_DH, :] + dtc_ref[...])
    lo, up = _tri(DN_CHUNK, True), _tri(DN_CHUNK, False)
    sub = lax.broadcasted_iota(jnp.int32, (N_DH, DN_CHUNK), 0)
    for c in range(tl // DN_CHUNK):
        sl = slice(c * DN_CHUNK, (c + 1) * DN_CHUNK)
        gc = g_col[sl, :]
        gcc_ref[0, sl, :] = _dot_hi(lo, gc)[:, 0:DN_HEADS]
        gcc_ref[1, sl, :] = _dot_hi(up, gc)[:, DN_HEADS:]
        gr = g_row[:, sl]
        cs = jnp.where(sub < DN_HEADS, _dot_hi(gr, up), _dot_hi(gr, lo))
        gcr_ref[0, 0, c] = cs[0:DN_HEADS]
        gcr_ref[1, 0, c] = cs[DN_HEADS:]


def dn_prep(qkv, ab, abt, conv_w, a_log, dt_bias, *, batch, tl=512):
    n = qkv.shape[0]
    seq = n // batch
    tl = min(tl, seq)
    nblk = seq // tl
    hb = tl // QKV_HALO
    nc_all = seq // DN_CHUNK
    full = lambda shape: pl.BlockSpec(shape, lambda b, i: (0,) * len(shape))
    rowblk = lambda w: pl.BlockSpec((tl, w), lambda b, i: (b * nblk + i, 0))
    outs = [jax.ShapeDtypeStruct((n, DN_WIDTH), BF16)] * 3 + [
        jax.ShapeDtypeStruct((DN_DIRS, n, DN_HEADS), F32), jax.ShapeDtypeStruct((DN_DIRS, n, DN_HEADS), F32),
        jax.ShapeDtypeStruct((DN_DIRS, batch, nc_all, DN_HEADS, DN_CHUNK), F32)]
    a_log = a_log.reshape(N_DH)
    dt_bias = dt_bias.reshape(N_DH)
    return pl.pallas_call(
        _dn_prep_kernel,
        out_shape=outs,
        grid=(batch, nblk),
        in_specs=[rowblk(3 * DN_WIDTH),
                  pl.BlockSpec((QKV_HALO, 3 * DN_WIDTH), lambda b, i: (jnp.maximum((b * nblk + i) * hb - 1, 0), 0)),
                  pl.BlockSpec((QKV_HALO, 3 * DN_WIDTH),
                               lambda b, i: (jnp.minimum((b * nblk + i + 1) * hb, n // QKV_HALO - 1), 0)),
                  rowblk(2 * N_DH),
                  pl.BlockSpec((2 * N_DH, tl), lambda b, i: (0, b * nblk + i)),
                  full((DN_CONV, 3 * DN_WIDTH)), full((1, N_DH)), full((1, N_DH)), full((N_DH, 1)), full((N_DH, 1)),
                  full((DN_WIDTH, DN_WIDTH))],
        out_specs=[rowblk(DN_WIDTH)] * 3 + [
            pl.BlockSpec((DN_DIRS, tl, DN_HEADS), lambda b, i: (0, b * nblk + i, 0)),
            pl.BlockSpec((DN_DIRS, tl, DN_HEADS), lambda b, i: (0, b * nblk + i, 0)),
            pl.BlockSpec((DN_DIRS, 1, tl // DN_CHUNK, DN_HEADS, DN_CHUNK), lambda b, i: (0, b, i, 0, 0))],
        compiler_params=_cparams("parallel", "parallel"),
        name="dn_prep",
    )(qkv, qkv, qkv, ab, abt, conv_w, a_log.reshape(1, N_DH), dt_bias.reshape(1, N_DH),
      a_log.reshape(N_DH, 1), dt_bias.reshape(N_DH, 1), _group_ones(DN_WIDTH, DN_HEAD_DIM))


DN_LOCAL_GROUP = 4


def _unit_tri_inverses(mats, block):
    n = mats[0].shape[0]
    r = lax.broadcasted_iota(jnp.int32, (n, n), 0)
    c = lax.broadcasted_iota(jnp.int32, (n, n), 1)
    eye = jnp.where(r == c, 1.0, 0.0)
    ds = [eye - jnp.where(r // 2 == c // 2, a, 0.0) for a in mats]
    size = 2
    while size < block:
        off = jnp.logical_and(r // (2 * size) == c // (2 * size), r // size != c // size)
        d16s = [d.astype(BF16) for d in ds]
        ts = [_dot(d16, jnp.where(off, a, 0.0).astype(BF16)) for d16, a in zip(d16s, mats)]
        ds = [d - _dot(t.astype(BF16), d16) for d, t, d16 in zip(ds, ts, d16s)]
        size *= 2
    return ds


def _dn_scan_kernel(q_ref, k_ref, v_ref, gcc_ref, beta_ref, gcr_ref, s0_ref, o_ref, s_ref,
                    rg_scr, n_scr, h_scr, eg_scr):
    d = pl.program_id(0)
    i = pl.program_id(1)
    nb, nc = rg_scr.shape[0], rg_scr.shape[1]
    per_batch = max(DN_LOCAL_GROUP // nb, 1)
    cs = DN_CHUNK
    hd = DN_HEAD_DIM
    fwd = d == 0

    @pl.when(i == 0)
    def _():
        s_ref[...] = s0_ref[...]

    hc = DN_HEADS * cs
    r = lax.broadcasted_iota(jnp.int32, (hc, hc), 0)
    c = lax.broadcasted_iota(jnp.int32, (hc, hc), 1)
    same_head = (r // cs) == (c // cs)
    rel = (r % cs - c % cs) * jnp.where(fwd, 1, -1)
    incl = jnp.logical_and(same_head, rel >= 0)
    strict = jnp.logical_and(same_head, rel > 0)
    last = jnp.where(fwd, cs - 1, 0)

    def load(b, ci):
        row0 = pl.multiple_of(ci * cs, cs)
        stack = lambda ref: jnp.concatenate(
            [ref[b, pl.ds(row0, cs), h * hd:(h + 1) * hd] for h in range(DN_HEADS)], axis=0)
        gate = lambda ref, rows: jnp.concatenate([ref[0, b, rows, h:h + 1] for h in range(DN_HEADS)], axis=0)
        q, k, v = stack(q_ref), stack(k_ref), stack(v_ref).astype(F32)
        gcol = gate(gcc_ref, pl.ds(row0, cs))
        bcol = gate(beta_ref, pl.ds(row0, cs))
        grows = gcr_ref[0, b, ci]
        grow = jnp.concatenate([grows[h:h + 1, :] for h in range(DN_HEADS)], axis=1)
        glast_h = [gcc_ref[0, b, pl.ds(row0 + last, 1), h:h + 1] for h in range(DN_HEADS)]
        glast = jnp.concatenate([jnp.broadcast_to(g, (cs, 1)) for g in glast_h], axis=0)
        decay = jnp.where(incl, jnp.exp(jnp.where(incl, gcol - grow, 0.0)), 0.0)
        kf = k.astype(F32)
        kb = kf * bcol
        eg = jnp.exp(gcol)
        a = jnp.where(strict, _dot_nt(kb.astype(BF16), k) * decay, 0.0)
        rhs = jnp.concatenate([v * bcol, kb * eg], axis=1).astype(BF16)
        qk = jnp.where(incl, _dot_nt(q, k) * decay, 0.0).astype(BF16)
        kd = (kf * jnp.exp(glast - gcol)).astype(BF16)
        return dict(b=b, ci=ci, a=a, rhs=rhs, qk=qk, kd=kd, qe=q.astype(F32) * eg, glast_h=glast_h)

    def local_group(j, carry):
        chunks = [load(b, per_batch * j + g) for b in range(nb) for g in range(per_batch)]
        ts = _unit_tri_inverses([ch["a"] for ch in chunks], cs)
        uws = [_dot(t.astype(BF16), ch["rhs"]).astype(BF16) for t, ch in zip(ts, chunks)]
        hgs = [_dot(ch["qk"], uw) for ch, uw in zip(chunks, uws)]
        for ch, uw, hg in zip(chunks, uws, hgs):
            b, ci = ch["b"], ch["ci"]
            g_mat = (ch["qe"] - hg[:, hd:]).astype(BF16)
            for h in range(DN_HEADS):
                rows = slice(h * cs, (h + 1) * cs)
                nr = _dot_tn(ch["kd"][rows], uw[rows])
                n_scr[b, ci, h] = nr[:, 0:hd]
                h_scr[b, ci, h] = hg[rows, 0:hd]
                rg_scr[b, ci, h, 0:hd, :] = nr[:, hd:].astype(BF16)
                rg_scr[b, ci, h, hd:hd + cs, :] = g_mat[rows]
                eg_scr[b, ci, h] = jnp.broadcast_to(jnp.exp(ch["glast_h"][h]), (1, hd))
        return carry

    lax.fori_loop(0, nc // per_batch, local_group, 0)

    def step(cc, carry):
        ci = jnp.where(fwd, cc, nc - 1 - cc)
        row0 = pl.multiple_of(ci * cs, cs)
        for b in range(nb):
            for h in range(DN_HEADS):
                s = s_ref[b, 0, h]
                y = _dot(rg_scr[b, ci, h], s.astype(BF16))
                s_ref[b, 0, h] = s * eg_scr[b, ci, h] - y[0:hd] + n_scr[b, ci, h]
                o_ref[0, b, pl.ds(row0, cs), h * hd:(h + 1) * hd] = (
                    y[hd:hd + cs] + h_scr[b, ci, h]).astype(o_ref.dtype)
        return carry

    lax.fori_loop(0, nc, step, 0)


def dn_scan(q, k, v, gcc, beta, gcr, s0, *, batch, tl=512):
    n = q.shape[0]
    seq = n // batch
    tl = min(tl, seq)
    nblk = seq // tl
    nc = tl // DN_CHUNK
    blk = lambda d, i: jnp.where(d == 0, i, nblk - 1 - i)
    rowspec = pl.BlockSpec((batch, tl, DN_WIDTH), lambda d, i: (0, blk(d, i), 0))
    gate = pl.BlockSpec((1, batch, tl, DN_HEADS), lambda d, i: (d, 0, blk(d, i), 0))
    state = pl.BlockSpec((batch, 1, DN_HEADS, DN_HEAD_DIM, DN_HEAD_DIM), lambda d, i: (0, d, 0, 0, 0))
    per_chunk = (batch, nc, DN_HEADS)
    rows3 = lambda t: t.reshape(batch, seq, DN_WIDTH)
    gates4 = lambda t: t.reshape(DN_DIRS, batch, seq, DN_HEADS)
    o, s_fin = pl.pallas_call(
        _dn_scan_kernel,
        out_shape=[jax.ShapeDtypeStruct((DN_DIRS, batch, seq, DN_WIDTH), BF16), jax.ShapeDtypeStruct(s0.shape, F32)],
        grid=(DN_DIRS, nblk),
        in_specs=[rowspec, rowspec, rowspec, gate, gate,
                  pl.BlockSpec((1, batch, nc, DN_HEADS, DN_CHUNK), lambda d, i: (d, 0, blk(d, i), 0, 0)),
                  state],
        out_specs=[pl.BlockSpec((1, batch, tl, DN_WIDTH), lambda d, i: (d, 0, blk(d, i), 0)), state],
        scratch_shapes=[pltpu.VMEM(per_chunk + (DN_HEAD_DIM + DN_CHUNK, DN_HEAD_DIM), BF16),
                        pltpu.VMEM(per_chunk + (DN_HEAD_DIM, DN_HEAD_DIM), F32),
                        pltpu.VMEM(per_chunk + (DN_CHUNK, DN_HEAD_DIM), F32),
                        pltpu.VMEM(per_chunk + (1, DN_HEAD_DIM), F32)],
        compiler_params=_cparams("parallel", "arbitrary"),
        name="dn_scan",
    )(rows3(q), rows3(k), rows3(v), gates4(gcc), gates4(beta), gcr, s0)
    return o.reshape(DN_DIRS, n, DN_WIDTH), s_fin


NA_ROWS_PER_STEP = 8
NA_ROW_GROUP = 4
MASKED = -1e30


def na_bias_table(rpb):
    qc = np.arange(GRID_W)[:, None]
    kc = np.arange(GRID_W)[None, :]
    c0 = np.clip(qc - NA_WIN_W // 2, 0, GRID_W - NA_WIN_W)
    valid = (kc >= c0) & (kc < c0 + NA_WIN_W)
    rel_c = kc - qc + NA_WIN_W - 1
    case = np.arange(NA_WIN_H)[:, None]
    i = np.arange(NA_WIN_H)[None, :]
    rel_r = i + NA_WIN_H - 1 - case
    pick_r = (rel_r[:, :, None] == np.arange(2 * NA_WIN_H - 1)).astype(np.float32)
    pick_c = ((rel_c[:, :, None] == np.arange(2 * NA_WIN_W - 1)) & valid[:, :, None]).astype(np.float32)
    tab = jnp.einsum('hrc,xir,qkc->xhqik', rpb, pick_r, pick_c, precision=lax.Precision.HIGHEST)
    tab = jnp.where(valid[None, None, :, None, :], tab, MASKED)
    return tab.reshape(NA_WIN_H, NA_HEADS * GRID_W, NA_WIN_H * GRID_W)


def _head_masks():
    lane = lax.broadcasted_iota(jnp.int32, (1, NA_WIDTH), 1)
    return [(lane // NA_HEAD_DIM == h) for h in range(NA_HEADS)]


def _na_kernel(grid_rows, q_ref, kp_ref, kc_ref, kn_ref, vp_ref, vc_ref, vn_ref, kx_ref, vx_ref, bias_ref, o_ref,
               kwin, vwin):
    j = pl.program_id(1)
    tq = q_ref.shape[0]
    nkeys = NA_WIN_H * GRID_W
    kwin[0:tq] = kp_ref[...]
    kwin[tq:2 * tq] = kc_ref[...]
    kwin[2 * tq:3 * tq] = kn_ref[...]
    vwin[0:tq] = vp_ref[...]
    vwin[tq:2 * tq] = vc_ref[...]
    vwin[2 * tq:3 * tq] = vn_ref[...]
    kx = kx_ref[...]
    vx = vx_ref[...]
    hq = NA_HEADS * GRID_W
    own = (lax.broadcasted_iota(jnp.int32, (hq, NA_WIDTH), 0) // GRID_W
           == lax.broadcasted_iota(jnp.int32, (hq, NA_WIDTH), 1) // NA_HEAD_DIM)
    for g in range(NA_ROWS_PER_STEP // NA_ROW_GROUP):
        rls = [g * NA_ROW_GROUP + t for t in range(NA_ROW_GROUP)]
        scores = []
        for rl in rls:
            r = j * NA_ROWS_PER_STEP + rl
            r0 = jnp.clip(r - NA_WIN_H // 2, 0, grid_rows - NA_WIN_H)
            start = pl.multiple_of((r0 - (j - 1) * NA_ROWS_PER_STEP) * GRID_W, GRID_W)
            q = q_ref[rl * GRID_W:(rl + 1) * GRID_W, :]
            qs = jnp.where(own, jnp.concatenate([q] * NA_HEADS, axis=0), jnp.zeros((hq, NA_WIDTH), q.dtype))
            s_loc = _dot_nt(qs, kwin[pl.ds(start, nkeys), :]) + bias_ref[r - r0]
            scores.append((s_loc, _dot_nt(qs, kx), start))
        probs = []
        for s_loc, s_ctx, start in scores:
            m = jnp.maximum(jnp.max(s_loc, axis=-1, keepdims=True), jnp.max(s_ctx, axis=-1, keepdims=True))
            p_loc = jnp.exp(s_loc - m)
            p_ctx = jnp.exp(s_ctx - m)
            denom = jnp.sum(p_loc, axis=-1, keepdims=True) + jnp.sum(p_ctx, axis=-1, keepdims=True)
            probs.append((p_loc.astype(BF16), p_ctx.astype(BF16), denom, start))
        for rl, (p_loc, p_ctx, denom, start) in zip(rls, probs):
            pv = _dot(p_loc, vwin[pl.ds(start, nkeys), :]) + _dot(p_ctx, vx)
            pv = jnp.where(own, pv / denom, 0.0)
            out = pv[0:GRID_W]
            for h in range(1, NA_HEADS):
                out = out + pv[h * GRID_W:(h + 1) * GRID_W]
            o_ref[rl * GRID_W:(rl + 1) * GRID_W, :] = out.astype(o_ref.dtype)


def na_attention(q, k, v, k_ctx, v_ctx, bias, *, batch):
    n = q.shape[0]
    seq = n // batch
    ctx_len = k_ctx.shape[0] // batch
    grid_rows = seq // GRID_W
    tq = NA_ROWS_PER_STEP * GRID_W
    nblk = seq // tq
    cur = lambda b, j: (b * nblk + j, 0)
    prev = lambda b, j: (b * nblk + jnp.maximum(j - 1, 0), 0)
    nxt = lambda b, j: (b * nblk + jnp.minimum(j + 1, nblk - 1), 0)
    blk = lambda f: pl.BlockSpec((tq, NA_WIDTH), f)
    ctx = pl.BlockSpec((ctx_len, NA_WIDTH), lambda b, j: (b, 0))
    return pl.pallas_call(
        functools.partial(_na_kernel, grid_rows),
        out_shape=jax.ShapeDtypeStruct((n, NA_WIDTH), BF16),
        grid=(batch, nblk),
        in_specs=[blk(cur), blk(prev), blk(cur), blk(nxt), blk(prev), blk(cur), blk(nxt), ctx, ctx,
                  pl.BlockSpec(bias.shape, lambda b, j: (0, 0, 0))],
        out_specs=blk(cur),
        scratch_shapes=[pltpu.VMEM((3 * tq, NA_WIDTH), BF16), pltpu.VMEM((3 * tq, NA_WIDTH), BF16)],
        compiler_params=_cparams("parallel", "parallel"),
        name="na_attention",
    )(q, k, k, k, v, v, v, k_ctx, v_ctx, bias)


def _ctx_attn_kernel(q_ref, k_ref, v_ref, o_ref):
    q = q_ref[...]
    k = k_ref[...]
    v = v_ref[...]
    masks = _head_masks()
    acc = jnp.zeros(q.shape, F32)
    for h in range(NA_HEADS):
        s = _dot_nt(jnp.where(masks[h], q, jnp.zeros_like(q)), k)
        p = jnp.exp(s - jnp.max(s, axis=-1, keepdims=True))
        pv = _dot(p.astype(BF16), v) / jnp.sum(p, axis=-1, keepdims=True)
        acc = acc + jnp.where(masks[h], pv, 0.0)
    o_ref[...] = acc.astype(o_ref.dtype)


def ctx_attention(q, k, v, *, batch):
    n = q.shape[0]
    blk = pl.BlockSpec((n // batch, NA_WIDTH), lambda b: (b, 0))
    return pl.pallas_call(
        _ctx_attn_kernel,
        out_shape=jax.ShapeDtypeStruct((n, NA_WIDTH), BF16),
        grid=(batch,),
        in_specs=[blk, blk, blk],
        out_specs=blk,
        compiler_params=_cparams("parallel"),
        name="ctx_attention",
    )(q, k, v)


FT_N1 = 128
FT_T2_BLOCK = 8
FT_K1_BLOCK = 8


def _dft_cos_sin(n, scale=1.0):
    ang = 2.0 * np.pi * np.outer(np.arange(n), np.arange(n)) / n
    return np.cos(ang) * scale, np.sin(ang) * scale


def _channel_dft():
    c, s = _dft_cos_sin(FT_GROUP_DIM)
    eye = np.eye(FT_GROUPS)
    return np.concatenate([np.kron(eye, c), -np.kron(eye, s)], axis=1)


def _ft_stage1_kernel(u_ref, cs_ref, ff_ref, yr_ref, yi_ref):
    n1 = u_ref.shape[1]
    for s in range(u_ref.shape[2] // FT_WIDTH):
        lanes = slice(s * FT_WIDTH, (s + 1) * FT_WIDTH)
        z = _dot(u_ref[0, :, lanes], cs_ref[...])
        zz = jnp.concatenate([z[:, 0:FT_WIDTH], z[:, FT_WIDTH:]], axis=0).astype(BF16)
        y = _dot(ff_ref[...], zz)
        yr_ref[0, :, lanes] = y[0:n1]
        yi_ref[0, :, lanes] = y[n1:]


def _ft_stage2_kernel(yr_ref, yi_ref, twr_ref, twi_ref, g_ref, w_ref, o_ref):
    for i in range(yr_ref.shape[1]):
        yr = yr_ref[0, i]
        yi = yi_ref[0, i]
        tr = twr_ref[0, :, i:i + 1]
        ti = twi_ref[0, :, i:i + 1]
        yy = jnp.concatenate([yr * tr - yi * ti, yr * ti + yi * tr], axis=0).astype(BF16)
        xr = _dot(g_ref[...], yy)
        o_ref[0, :, i, :] = _dot(xr.astype(BF16), w_ref[...]).astype(o_ref.dtype)


def fourier_mix(u, fno_w, *, batch):
    n = u.shape[0]
    seq = n // batch
    n1 = FT_N1
    n2 = seq // n1
    tb = min(FT_T2_BLOCK, n2)
    norm = 1.0 / math.sqrt(seq * FT_GROUP_DIM)
    c1, s1 = _dft_cos_sin(n1)
    ff = jnp.asarray(np.block([[c1, s1], [-s1, c1]]), BF16)
    yr, yi = pl.pallas_call(
        _ft_stage1_kernel,
        out_shape=[jax.ShapeDtypeStruct((batch, n1, n2 * FT_WIDTH), F32)] * 2,
        grid=(batch, n2 // tb),
        in_specs=[pl.BlockSpec((1, n1, tb * FT_WIDTH), lambda b, j: (b, 0, j)),
                  pl.BlockSpec((FT_WIDTH, 2 * FT_WIDTH), lambda b, j: (0, 0)),
                  pl.BlockSpec((2 * n1, 2 * n1), lambda b, j: (0, 0))],
        out_specs=[pl.BlockSpec((1, n1, tb * FT_WIDTH), lambda b, j: (b, 0, j))] * 2,
        compiler_params=_cparams("parallel", "parallel"),
        name="fourier_stage1",
    )(u.reshape(batch, n1, n2 * FT_WIDTH), jnp.asarray(_channel_dft(), BF16), ff)
    kb = FT_K1_BLOCK
    ang = 2.0 * np.pi * np.outer(np.arange(n1), np.arange(n2)) / seq
    tw = lambda f: jnp.asarray(f(ang).reshape(n1 // kb, kb, n2).transpose(0, 2, 1), F32)
    c2, s2 = _dft_cos_sin(n2, norm)
    g = jnp.asarray(np.concatenate([c2, s2], axis=1), BF16)
    out = pl.pallas_call(
        _ft_stage2_kernel,
        out_shape=jax.ShapeDtypeStruct((batch, n2, n1, FT_WIDTH), BF16),
        grid=(batch, n1 // kb),
        in_specs=[pl.BlockSpec((1, kb, n2, FT_WIDTH), lambda b, j: (b, j, 0, 0))] * 2 + [
            pl.BlockSpec((1, n2, kb), lambda b, j: (j, 0, 0))] * 2 + [
            pl.BlockSpec((n2, 2 * n2), lambda b, j: (0, 0)), pl.BlockSpec((FT_WIDTH, FT_WIDTH), lambda b, j: (0, 0))],
        out_specs=pl.BlockSpec((1, n2, kb, FT_WIDTH), lambda b, j: (b, 0, j, 0)),
        compiler_params=_cparams("parallel", "parallel"),
        name="fourier_stage2",
    )(yr.reshape(batch, n1, n2, FT_WIDTH), yi.reshape(batch, n1, n2, FT_WIDTH),
      tw(np.cos), tw(lambda a: -np.sin(a)), g, fno_w.astype(BF16))
    return out.reshape(n, FT_WIDTH)


def _ft_direct_kernel(u_ref, cs_ref, g_ref, w_ref, o_ref):
    z = _dot(u_ref[...], cs_ref[...])
    zz = jnp.concatenate([z[:, 0:FT_WIDTH], z[:, FT_WIDTH:]], axis=0).astype(BF16)
    xr = _dot(g_ref[...], zz)
    o_ref[...] = _dot(xr.astype(BF16), w_ref[...]).astype(o_ref.dtype)


def fourier_mix_direct(u, fno_w, *, batch):
    n = u.shape[0]
    seq = n // batch
    c, s = _dft_cos_sin(seq, 1.0 / math.sqrt(seq * FT_GROUP_DIM))
    g = jnp.asarray(np.concatenate([c, s], axis=1), BF16)
    return pl.pallas_call(
        _ft_direct_kernel,
        out_shape=jax.ShapeDtypeStruct((n, FT_WIDTH), BF16),
        grid=(batch,),
        in_specs=[pl.BlockSpec((seq, FT_WIDTH), lambda b: (b, 0)),
                  pl.BlockSpec((FT_WIDTH, 2 * FT_WIDTH), lambda b: (0, 0)),
                  pl.BlockSpec((seq, 2 * seq), lambda b: (0, 0)),
                  pl.BlockSpec((FT_WIDTH, FT_WIDTH), lambda b: (0, 0))],
        out_specs=pl.BlockSpec((seq, FT_WIDTH), lambda b: (b, 0)),
        compiler_params=_cparams("parallel"),
        name="fourier_direct",
    )(u, jnp.asarray(_channel_dft(), BF16), g, fno_w.astype(BF16))


def _mod_row(mod_ref, rows_per_mod, fixed_row, tm):
    row = fixed_row if fixed_row is not None else (pl.program_id(0) * tm) // rows_per_mod
    return mod_ref[pl.ds(row, 1), :]


def _mix_out_kernel(rows_per_mod, fixed_row, route, o_ref, z_ref, ob_ref, oc_ref, x_ref, mod_ref, nw_ref, ones_ref,
                    w_ref, g_ref, *rest):
    if route:
        wr_ref, x_out, h_out, e_out, p_out = rest
    else:
        x_out, h_out = rest
    tm = x_ref.shape[0]
    m = _mod_row(mod_ref, rows_per_mod, fixed_row, tm)
    o = o_ref[0].astype(F32) + o_ref[1].astype(F32)
    oa = _head_rms(o, ones_ref[...], DN_HEAD_DIM, nw_ref[...]) * _silu(z_ref[...].astype(F32))
    y = (_dot(oa.astype(BF16), w_ref[0:DN_WIDTH, :]) + _dot(ob_ref[...], w_ref[DN_WIDTH:DN_WIDTH + NA_WIDTH, :])
         + _dot(oc_ref[...], w_ref[DN_WIDTH + NA_WIDTH:, :]))
    x = x_ref[...] + m[:, 2 * D_MODEL:3 * D_MODEL] * y
    x_out[...] = x
    h = _modulated_norm(x, g_ref[...], m[:, 3 * D_MODEL:4 * D_MODEL], m[:, 4 * D_MODEL:5 * D_MODEL])
    h_out[...] = h.astype(h_out.dtype)
    if route:
        h_hi = h.astype(BF16)
        h_lo = (h - h_hi.astype(F32)).astype(BF16)
        both = _dot(h_hi, wr_ref[...]) + _dot(h_lo, wr_ref[...])
        logits = both[:, 0:N_EXPERTS] + both[:, N_EXPERTS:]
        idx = lax.broadcasted_iota(jnp.int32, logits.shape, 1)
        m1 = jnp.max(logits, axis=-1, keepdims=True)
        e1 = jnp.min(jnp.where(logits == m1, idx, N_EXPERTS), axis=-1, keepdims=True)
        rest_l = jnp.where(idx == e1, -jnp.inf, logits)
        m2 = jnp.max(rest_l, axis=-1, keepdims=True)
        e2 = jnp.min(jnp.where(rest_l == m2, idx, N_EXPERTS), axis=-1, keepdims=True)
        t = jnp.exp(m2 - m1)
        e_out[...] = jnp.concatenate([e1, e2], axis=1)
        p_out[...] = jnp.concatenate([1.0 / (1.0 + t), t / (1.0 + t)], axis=1)


def mix_out(o_dirs, z, ob, oc, x2d, mod, dn_norm_w, w_out, g_ffn, w_router=None, *, rows_per_mod, fixed_row=None,
            tm=512):
    n, d = x2d.shape
    tm = min(tm, n)
    route = w_router is not None
    full = lambda shape: pl.BlockSpec(shape, lambda i: (0,) * len(shape))
    rowblk = lambda w: pl.BlockSpec((tm, w), lambda i: (i, 0))
    ins = [o_dirs, z, ob, oc, x2d, mod, jnp.tile(dn_norm_w, DN_HEADS).reshape(1, DN_WIDTH),
           _group_ones(DN_WIDTH, DN_HEAD_DIM), w_out, g_ffn.reshape(1, d)]
    in_specs = [pl.BlockSpec((DN_DIRS, tm, DN_WIDTH), lambda i: (0, i, 0)), rowblk(DN_WIDTH), rowblk(NA_WIDTH),
                rowblk(FT_WIDTH), rowblk(d), full(mod.shape), full((1, DN_WIDTH)), full((DN_WIDTH, DN_WIDTH)),
                full(w_out.shape), full((1, d))]
    outs = [jax.ShapeDtypeStruct((n, d), F32), jax.ShapeDtypeStruct((n, d), F32 if route else BF16)]
    out_specs = [rowblk(d), rowblk(d)]
    if route:
        w_hi = w_router.astype(BF16)
        w_lo = (w_router - w_hi.astype(F32)).astype(BF16)
        ins.append(jnp.concatenate([w_hi, w_lo], axis=1))
        in_specs.append(full((d, 2 * N_EXPERTS)))
        outs += [jax.ShapeDtypeStruct((n, TOP_K), jnp.int32), jax.ShapeDtypeStruct((n, TOP_K), F32)]
        out_specs += [rowblk(TOP_K), rowblk(TOP_K)]
    return pl.pallas_call(
        functools.partial(_mix_out_kernel, rows_per_mod, fixed_row, route),
        out_shape=outs,
        grid=(n // tm,),
        in_specs=in_specs,
        out_specs=out_specs,
        compiler_params=_cparams("parallel"),
        name="mix_out_route" if route else "mix_out",
    )(*ins)


def _ffn_kernel(rows_per_mod, fixed_row, h_ref, x_ref, mod_ref, w1_ref, w3_ref, w2_ref, o_ref, acc_ref):
    f = pl.program_id(1)
    h = h_ref[...]
    part = _dot((_silu(_dot(h, w1_ref[...])) * _dot(h, w3_ref[...])).astype(BF16), w2_ref[...])

    last = f == pl.num_programs(1) - 1

    @pl.when(f == 0)
    def _():
        acc_ref[...] = part

    @pl.when(jnp.logical_and(f > 0, jnp.logical_not(last)))
    def _():
        acc_ref[...] += part

    @pl.when(last)
    def _():
        m = _mod_row(mod_ref, rows_per_mod, fixed_row, h_ref.shape[0])
        o_ref[...] = x_ref[...] + m[:, 5 * D_MODEL:6 * D_MODEL] * (acc_ref[...] + part)


def dense_ffn(h, x2d, mod, w1, w3, w2, *, rows_per_mod, fixed_row=None, tm=512, tf=1408):
    n, d = x2d.shape
    tm = min(tm, n)
    dff = w1.shape[1]
    assert dff // tf >= 2, "the first and last hidden tiles are handled by different branches"
    return pl.pallas_call(
        functools.partial(_ffn_kernel, rows_per_mod, fixed_row),
        out_shape=jax.ShapeDtypeStruct((n, d), F32),
        grid=(n // tm, dff // tf),
        in_specs=[pl.BlockSpec((tm, d), lambda i, f: (i, 0)), pl.BlockSpec((tm, d), lambda i, f: (i, 0)),
                  pl.BlockSpec(mod.shape, lambda i, f: (0, 0)),
                  pl.BlockSpec((d, tf), lambda i, f: (0, f)), pl.BlockSpec((d, tf), lambda i, f: (0, f)),
                  pl.BlockSpec((tf, d), lambda i, f: (f, 0))],
        out_specs=pl.BlockSpec((tm, d), lambda i, f: (i, 0)),
        scratch_shapes=[pltpu.VMEM((tm, d), F32)],
        compiler_params=_cparams("parallel", "arbitrary"),
        name="dense_ffn",
    )(h, x2d, mod, w1, w3, w2)


MOE_ROWS = 512
MOE_DMA_ROWS = 256


def _row_copy(src, src_row, dst, dst_row, sem):
    return pltpu.make_async_copy(src.at[pl.ds(src_row, 1)], dst.at[pl.ds(dst_row, 1)], sem)


def _moe_scatter_kernel(dest_ref, h_ref, xb_in, xb_out, sem):
    del xb_in
    tm = h_ref.shape[0]

    def start(t, carry):
        for j in range(TOP_K):
            _row_copy(h_ref, t, xb_out, dest_ref[0, 0, TOP_K * t + j], sem).start()
        return carry

    def wait(t, carry):
        for j in range(TOP_K):
            _row_copy(h_ref, 0, xb_out, 0, sem).wait()
        return carry

    lax.fori_loop(0, tm, start, 0, unroll=8)
    lax.fori_loop(0, tm, wait, 0, unroll=8)


def _moe_ffn_kernel(be_ref, na_ref, x_ref, w1_ref, w3_ref, w2_ref, y_ref, acc_ref):
    i = pl.program_id(0)
    f = pl.program_id(1)
    last = f == pl.num_programs(1) - 1
    active = i < na_ref[0]

    @pl.when(active)
    def _():
        h = x_ref[...].astype(BF16)
        part = _dot((_silu(_dot(h, w1_ref[0])) * _dot(h, w3_ref[0])).astype(BF16), w2_ref[0])

        @pl.when(f == 0)
        def _():
            acc_ref[...] = part

        @pl.when(jnp.logical_and(f > 0, jnp.logical_not(last)))
        def _():
            acc_ref[...] += part

        @pl.when(last)
        def _():
            y_ref[...] = acc_ref[...] + part

    @pl.when(jnp.logical_and(jnp.logical_not(active), last))
    def _():
        y_ref[...] = jnp.zeros_like(y_ref)


def _moe_combine_kernel(rows_per_mod, dest_ref, p_ref, x_ref, mod_ref, yb_hbm, o_ref, buf, sem):
    tm = x_ref.shape[0]

    def start(t, carry):
        for j in range(TOP_K):
            _row_copy(yb_hbm, dest_ref[0, 0, TOP_K * t + j], buf.at[j], t, sem).start()
        return carry

    def wait(t, carry):
        for j in range(TOP_K):
            _row_copy(yb_hbm, 0, buf.at[j], 0, sem).wait()
        return carry

    lax.fori_loop(0, tm, start, 0, unroll=8)
    lax.fori_loop(0, tm, wait, 0, unroll=8)
    p = p_ref[...]
    y = p[:, 0:1] * buf[0] + p[:, 1:2] * buf[1]
    m = _mod_row(mod_ref, rows_per_mod, None, tm)
    o_ref[...] = x_ref[...] + m[:, 5 * D_MODEL:6 * D_MODEL] * y


def moe_ffn(h, x2d, mod, top_e, top_p, w1, w3, w2, *, rows_per_mod, tf=1792):
    n, d = x2d.shape
    dff = w1.shape[2]
    assert dff // tf >= 2, "the first and last hidden tiles are handled by different branches"
    n_assign = n * TOP_K
    n_blocks = -(-n_assign // MOE_ROWS) + N_EXPERTS
    n_slots = n_blocks * MOE_ROWS
    e_flat = top_e.reshape(n_assign)
    onehot = (e_flat[:, None] == jnp.arange(N_EXPERTS, dtype=jnp.int32)[None, :]).astype(jnp.int32)
    csum = jnp.cumsum(onehot, axis=0)
    rank = jnp.sum(onehot * csum, axis=1) - 1
    counts = csum[-1]
    padded = (counts + MOE_ROWS - 1) // MOE_ROWS * MOE_ROWS
    pad_end = jnp.cumsum(padded)
    dest = (jnp.sum(onehot * (pad_end - padded)[None, :], axis=1) + rank).astype(jnp.int32)
    n_active = (pad_end[-1] // MOE_ROWS).astype(jnp.int32).reshape(1)
    blk_ids = jnp.minimum(jnp.arange(n_blocks, dtype=jnp.int32), n_active[0] - 1)
    block_e = jnp.minimum(jnp.searchsorted(pad_end, blk_ids * MOE_ROWS, side='right'), N_EXPERTS - 1).astype(jnp.int32)

    tmd = min(MOE_DMA_ROWS, n)
    dest2d = dest.reshape(n // tmd, 1, TOP_K * tmd)
    dest_spec = pl.BlockSpec((1, 1, TOP_K * tmd), lambda i: (i, 0, 0), memory_space=pltpu.SMEM)
    hbm = pl.BlockSpec(memory_space=pl.ANY)
    xb = pl.pallas_call(
        _moe_scatter_kernel,
        out_shape=jax.ShapeDtypeStruct((n_slots, d), F32),
        grid=(n // tmd,),
        in_specs=[dest_spec, pl.BlockSpec((tmd, d), lambda i: (i, 0)), hbm],
        out_specs=hbm,
        scratch_shapes=[pltpu.SemaphoreType.DMA],
        input_output_aliases={2: 0},
        compiler_params=_cparams("arbitrary"),
        name="moe_scatter",
    )(dest2d, h, jnp.zeros((n_slots, d), F32))

    yb = pl.pallas_call(
        _moe_ffn_kernel,
        out_shape=jax.ShapeDtypeStruct((n_slots, d), F32),
        grid_spec=pltpu.PrefetchScalarGridSpec(
            num_scalar_prefetch=2,
            grid=(n_blocks, dff // tf),
            in_specs=[pl.BlockSpec((MOE_ROWS, d), lambda i, f, be, na: (i, 0)),
                      pl.BlockSpec((1, d, tf), lambda i, f, be, na: (be[i], 0, f)),
                      pl.BlockSpec((1, d, tf), lambda i, f, be, na: (be[i], 0, f)),
                      pl.BlockSpec((1, tf, d), lambda i, f, be, na: (be[i], f, 0))],
            out_specs=pl.BlockSpec((MOE_ROWS, d), lambda i, f, be, na: (i, 0)),
            scratch_shapes=[pltpu.VMEM((MOE_ROWS, d), F32)]),
        compiler_params=_cparams("arbitrary", "arbitrary"),
        name="moe_expert_ffn",
    )(block_e, n_active, xb, w1, w3, w2)

    return pl.pallas_call(
        functools.partial(_moe_combine_kernel, rows_per_mod),
        out_shape=jax.ShapeDtypeStruct((n, d), F32),
        grid=(n // tmd,),
        in_specs=[dest_spec, pl.BlockSpec((tmd, TOP_K), lambda i: (i, 0)), pl.BlockSpec((tmd, d), lambda i: (i, 0)),
                  pl.BlockSpec(mod.shape, lambda i: (0, 0)), hbm],
        out_specs=pl.BlockSpec((tmd, d), lambda i: (i, 0)),
        scratch_shapes=[pltpu.VMEM((TOP_K, tmd, d), F32), pltpu.SemaphoreType.DMA],
        compiler_params=_cparams("arbitrary"),
        name="moe_combine",
    )(dest2d, top_p, x2d, mod, yb)


def _split_w_in(w_in_l):
    g0 = 4 * DN_WIDTH
    g1 = g0 + 2 * N_DH
    w_main = jnp.concatenate([w_in_l[:, :g0], w_in_l[:, g1:]], axis=1).astype(BF16)
    w_ab = w_in_l[:, g0:g1].astype(BF16)
    return w_main, w_ab, w_ab.T


def kernel(x, c, ctx, c_ctx, w_mod, b_mod, g_mix, g_ffn, w_in, dn_conv, dn_a_log, dn_dt_bias, dn_norm_w, na_q_norm,
           na_k_norm, na_rpb, fno_w, w_out, ffn_w1, ffn_w3, ffn_w2, moe_router, moe_w1, moe_w3, moe_w2):
    bsz, seq, d = x.shape
    ctx_len = ctx.shape[1]
    ctx_row = bsz
    cond = jnp.zeros((SUBLANES, d), F32).at[:bsz].set(c).at[ctx_row].set(c_ctx)
    mods = adaln_mod(cond, w_mod, b_mod)
    x2 = x.reshape(bsz * seq, d)
    c2 = ctx.reshape(bsz * ctx_len, d)
    zero_state = jnp.zeros((bsz, DN_DIRS, DN_HEADS, DN_HEAD_DIM, DN_HEAD_DIM), F32)

    for layer in range(DEPTH):
        need_ctx = layer < DEPTH - 1
        mod = mods[layer]
        w_main, w_ab, w_abt = _split_w_in(w_in[layer])
        proj = functools.partial(in_proj, mod=mod, g=g_mix[layer], w_main=w_main, w_ab=w_ab, w_abt=w_abt,
                                 q_norm=na_q_norm[layer], k_norm=na_k_norm[layer])
        qkv, z, ab, abt, nq, nk, nv, ft = proj(x2, rows_per_mod=seq)
        qkv_c, z_c, ab_c, abt_c, nq_c, nk_c, nv_c, ft_c = proj(c2, rows_per_mod=ctx_len, fixed_row=ctx_row)

        prep = functools.partial(dn_prep, conv_w=dn_conv[layer], a_log=dn_a_log[layer], dt_bias=dn_dt_bias[layer],
                                 batch=bsz)
        q_c, k_c, v_c, gcc_c, beta_c, gcr_c = prep(qkv_c, ab_c, abt_c)
        o_c, s_ctx = dn_scan(q_c, k_c, v_c, gcc_c, beta_c, gcr_c, zero_state, batch=bsz)
        q_l, k_l, v_l, gcc_l, beta_l, gcr_l = prep(qkv, ab, abt)
        o_l, _ = dn_scan(q_l, k_l, v_l, gcc_l, beta_l, gcr_l, s_ctx, batch=bsz)

        ob = na_attention(nq, nk, nv, nk_c, nv_c, na_bias_table(na_rpb[layer]), batch=bsz)
        oc = fourier_mix(ft, fno_w[layer], batch=bsz)

        w_out_l = w_out[layer].astype(BF16)
        j = layer // 2
        if layer % 2 == 0:
            x2, h2 = mix_out(o_l, z, ob, oc, x2, mod, dn_norm_w[layer], w_out_l, g_ffn[layer], rows_per_mod=seq)
            ffn_w = (ffn_w1[j].astype(BF16), ffn_w3[j].astype(BF16), ffn_w2[j].astype(BF16))
            x2 = dense_ffn(h2, x2, mod, *ffn_w, rows_per_mod=seq)
        else:
            x2, h2, top_e, top_p = mix_out(o_l, z, ob, oc, x2, mod, dn_norm_w[layer], w_out_l, g_ffn[layer],
                                           moe_router[j], rows_per_mod=seq)
            x2 = moe_ffn(h2, x2, mod, top_e, top_p, moe_w1[j].astype(BF16), moe_w3[j].astype(BF16),
                         moe_w2[j].astype(BF16), rows_per_mod=seq)
        if need_ctx:
            ob_c = ctx_attention(nq_c, nk_c, nv_c, batch=bsz)
            oc_c = fourier_mix_direct(ft_c, fno_w[layer], batch=bsz)
            assert layer % 2 == 0, "context tokens only pass through dense layers at this depth"
            c2, hc2 = mix_out(o_c, z_c, ob_c, oc_c, c2, mod, dn_norm_w[layer], w_out_l, g_ffn[layer],
                              rows_per_mod=ctx_len, fixed_row=ctx_row)
            c2 = dense_ffn(hc2, c2, mod, *ffn_w, rows_per_mod=ctx_len, fixed_row=ctx_row)
    return x2.reshape(bsz, seq, d)
```

```python
import functools
import math

import jax
import jax.numpy as jnp
import numpy as np
from jax import lax
from jax.experimental import pallas as pl
from jax.experimental.pallas import tpu as pltpu

D_MODEL = 1024
DEPTH = 2
GRID_W = 64
DN_HEADS = 4
DN_HEAD_DIM = 128
DN_WIDTH = DN_HEADS * DN_HEAD_DIM
DN_CONV = 3
DN_CHUNK = 64
DN_DIRS = 2
NA_HEADS = 4
NA_HEAD_DIM = 64
NA_WIDTH = NA_HEADS * NA_HEAD_DIM
NA_WIN_H = 8
NA_WIN_W = 16
FT_GROUPS = 4
FT_GROUP_DIM = 64
FT_WIDTH = FT_GROUPS * FT_GROUP_DIM
N_EXPERTS = 8
TOP_K = 2
N_MOD = 6
EPS = 1e-6

SUBLANES = 8
LANES = 128
VMEM_LIMIT = 48 * 1024 * 1024

BF16 = jnp.bfloat16
F32 = jnp.float32
N_DH = DN_DIRS * DN_HEADS


def _cparams(*sem):
    return pltpu.CompilerParams(dimension_semantics=sem, vmem_limit_bytes=VMEM_LIMIT)


def _silu(x):
    return x * jax.nn.sigmoid(x)


def _dot(a, b):
    return jnp.dot(a, b, preferred_element_type=F32)


def _dot_nt(a, b):
    return lax.dot_general(a, b, (((1,), (1,)), ((), ())), preferred_element_type=F32)


def _dot_tn(a, b):
    return lax.dot_general(a, b, (((0,), (0,)), ((), ())), preferred_element_type=F32)


def _group_ones(width, group):
    idx = np.arange(width) // group
    return jnp.asarray((idx[:, None] == idx[None, :]).astype(np.float32), BF16)


def _mod_kernel(c_ref, w_ref, b_ref, o_ref):
    s = _silu(c_ref[...])
    o_ref[0] = jnp.dot(s, w_ref[0], preferred_element_type=F32, precision=lax.Precision.HIGHEST) + b_ref[0]


def adaln_mod(cond_rows, w_mod, b_mod, *, tn=1536):
    depth, d, n = w_mod.shape
    rows = cond_rows.shape[0]
    return pl.pallas_call(
        _mod_kernel,
        out_shape=jax.ShapeDtypeStruct((depth, rows, n), F32),
        grid=(depth, n // tn),
        in_specs=[pl.BlockSpec((rows, d), lambda l, j: (0, 0)),
                  pl.BlockSpec((1, d, tn), lambda l, j: (l, 0, j)),
                  pl.BlockSpec((1, 1, tn), lambda l, j: (l, 0, j))],
        out_specs=pl.BlockSpec((1, rows, tn), lambda l, j: (l, 0, j)),
        compiler_params=_cparams("arbitrary", "arbitrary"),
        name="adaln_mod",
    )(cond_rows, w_mod, b_mod.reshape(depth, 1, n))


C_QKV = 0
C_Z = 3 * DN_WIDTH
C_NQ = C_Z + DN_WIDTH
C_NK = C_NQ + NA_WIDTH
C_NV = C_NK + NA_WIDTH
C_FT = C_NV + NA_WIDTH
C_END = C_FT + FT_WIDTH


def _modulated_norm(x, g, shift, scale):
    y = x * lax.rsqrt(jnp.mean(x * x, axis=-1, keepdims=True) + EPS)
    return y * g * (1.0 + scale) + shift


ROW_PART = 128


def _row_parts(tm):
    part = min(ROW_PART, tm)
    return [slice(r, r + part) for r in range(0, tm, part)]


def _head_rms(x, ones, width, gain):
    ss = _dot((x * x).astype(BF16), ones)
    return x * lax.rsqrt(ss * (1.0 / width) + EPS) * gain


def _in_proj_kernel(rows_per_mod, fixed_row, x_ref, mod_ref, g_ref, w_ref, wab_ref, wabt_ref, qn_ref, kn_ref,
                    ones_ref, qkv_ref, z_ref, ab_ref, abt_ref, nq_ref, nk_ref, nv_ref, ft_ref):
    tm = x_ref.shape[0]
    row = fixed_row if fixed_row is not None else (pl.program_id(0) * tm) // rows_per_mod
    m = mod_ref[pl.ds(row, 1), :]
    shift, scale = m[:, 0:D_MODEL], m[:, D_MODEL:2 * D_MODEL]
    ones = ones_ref[...]
    for rows in _row_parts(tm):
        h = _modulated_norm(x_ref[rows, :], g_ref[...], shift, scale).astype(BF16)
        qkv_ref[rows, :] = _dot(h, w_ref[:, C_QKV:C_Z]).astype(qkv_ref.dtype)
        z_ref[rows, :] = _dot(h, w_ref[:, C_Z:C_NQ]).astype(z_ref.dtype)
        nq = _dot(h, w_ref[:, C_NQ:C_NK])
        nq_ref[rows, :] = (_head_rms(nq, ones, NA_HEAD_DIM, qn_ref[...]) * (NA_HEAD_DIM ** -0.5)).astype(nq_ref.dtype)
        nk = _dot(h, w_ref[:, C_NK:C_NV])
        nk_ref[rows, :] = _head_rms(nk, ones, NA_HEAD_DIM, kn_ref[...]).astype(nk_ref.dtype)
        nv_ref[rows, :] = _dot(h, w_ref[:, C_NV:C_FT]).astype(nv_ref.dtype)
        ft_ref[rows, :] = _dot(h, w_ref[:, C_FT:C_END]).astype(ft_ref.dtype)
        ab_ref[rows, :] = _dot(h, wab_ref[...])
        abt_ref[:, rows] = _dot_nt(wabt_ref[...], h)


def in_proj(x2d, mod, g, w_main, w_ab, w_abt, q_norm, k_norm, *, rows_per_mod, fixed_row=None, tm=512):
    n, d = x2d.shape
    tm = min(tm, n)
    nab = w_ab.shape[1]
    full = lambda shape: pl.BlockSpec(shape, lambda i: (0,) * len(shape))
    rowblk = lambda w: pl.BlockSpec((tm, w), lambda i: (i, 0))
    outs = [jax.ShapeDtypeStruct((n, 3 * DN_WIDTH), BF16), jax.ShapeDtypeStruct((n, DN_WIDTH), BF16),
            jax.ShapeDtypeStruct((n, nab), F32), jax.ShapeDtypeStruct((nab, n), F32),
            jax.ShapeDtypeStruct((n, NA_WIDTH), BF16), jax.ShapeDtypeStruct((n, NA_WIDTH), BF16),
            jax.ShapeDtypeStruct((n, NA_WIDTH), BF16), jax.ShapeDtypeStruct((n, FT_WIDTH), BF16)]
    return pl.pallas_call(
        functools.partial(_in_proj_kernel, rows_per_mod, fixed_row),
        out_shape=outs,
        grid=(n // tm,),
        in_specs=[rowblk(d), full(mod.shape), full((1, d)), full(w_main.shape), full(w_ab.shape), full(w_abt.shape),
                  full((1, NA_WIDTH)), full((1, NA_WIDTH)), full((NA_WIDTH, NA_WIDTH))],
        out_specs=[rowblk(3 * DN_WIDTH), rowblk(DN_WIDTH), rowblk(nab), pl.BlockSpec((nab, tm), lambda i: (0, i)),
                   rowblk(NA_WIDTH), rowblk(NA_WIDTH), rowblk(NA_WIDTH), rowblk(FT_WIDTH)],
        compiler_params=_cparams("parallel"),
        name="in_proj",
    )(x2d, mod, g.reshape(1, d), w_main, w_ab, w_abt,
      jnp.tile(q_norm, NA_HEADS).reshape(1, NA_WIDTH), jnp.tile(k_norm, NA_HEADS).reshape(1, NA_WIDTH),
      _group_ones(NA_WIDTH, NA_HEAD_DIM))


QKV_HALO = 16


def _softplus(x):
    return jnp.maximum(x, 0.0) + jnp.log1p(jnp.exp(-jnp.abs(x)))


def _tri(n, lower):
    r = lax.broadcasted_iota(jnp.int32, (n, n), 0)
    c = lax.broadcasted_iota(jnp.int32, (n, n), 1)
    return jnp.where((r >= c) if lower else (r <= c), 1.0, 0.0).astype(F32)


def _dot_hi(a, b):
    return jnp.dot(a, b, preferred_element_type=F32, precision=lax.Precision.HIGHEST)


def _dn_prep_kernel(x_ref, xp_ref, xn_ref, ab_ref, abt_ref, cw_ref, alr_ref, dtr_ref, alc_ref, dtc_ref, ones_ref,
                    q_ref, k_ref, v_ref, gcc_ref, beta_ref, gcr_ref):
    i = pl.program_id(1)
    tl = x_ref.shape[0]
    x = x_ref[...].astype(F32)
    prev = jnp.where(i > 0, xp_ref[QKV_HALO - 1:QKV_HALO, :].astype(F32), 0.0)
    nxt = jnp.where(i < pl.num_programs(1) - 1, xn_ref[0:1, :].astype(F32), 0.0)
    rows = lax.broadcasted_iota(jnp.int32, x.shape, 0)
    xm1 = jnp.where(rows == 0, prev, pltpu.roll(x, 1, axis=0))
    xp1 = jnp.where(rows == tl - 1, nxt, pltpu.roll(x, tl - 1, axis=0))
    y = _silu(xm1 * cw_ref[0:1, :] + x * cw_ref[1:2, :] + xp1 * cw_ref[2:3, :])
    ones = ones_ref[...]
    q = y[:, 0:DN_WIDTH]
    k = y[:, DN_WIDTH:2 * DN_WIDTH]
    q_ref[...] = (q * lax.rsqrt(_dot((q * q).astype(BF16), ones) + EPS) * (DN_HEAD_DIM ** -0.5)).astype(q_ref.dtype)
    k_ref[...] = (k * lax.rsqrt(_dot((k * k).astype(BF16), ones) + EPS)).astype(k_ref.dtype)
    v_ref[...] = y[:, 2 * DN_WIDTH:].astype(v_ref.dtype)

    ab = ab_ref[...]
    g_col = -jnp.exp(alr_ref[...]) * _softplus(ab[:, 0:N_DH] + dtr_ref[...])
    beta = jax.nn.sigmoid(ab[:, N_DH:2 * N_DH])
    beta_ref[0] = beta[:, 0:DN_HEADS]
    beta_ref[1] = beta[:, DN_HEADS:]
    g_row = -jnp.exp(alc_ref[...]) * _softplus(abt_ref[0:N_DH, :] + dtc_ref[...])
    lo, up = _tri(DN_CHUNK, True), _tri(DN_CHUNK, False)
    sub = lax.broadcasted_iota(jnp.int32, (N_DH, DN_CHUNK), 0)
    for c in range(tl // DN_CHUNK):
        sl = slice(c * DN_CHUNK, (c + 1) * DN_CHUNK)
        gc = g_col[sl, :]
        gcc_ref[0, sl, :] = _dot_hi(lo, gc)[:, 0:DN_HEADS]
        gcc_ref[1, sl, :] = _dot_hi(up, gc)[:, DN_HEADS:]
        gr = g_row[:, sl]
        cs = jnp.where(sub < DN_HEADS, _dot_hi(gr, up), _dot_hi(gr, lo))
        gcr_ref[0, 0, c] = cs[0:DN_HEADS]
        gcr_ref[1, 0, c] = cs[DN_HEADS:]


def dn_prep(qkv, ab, abt, conv_w, a_log, dt_bias, *, batch, tl=512):
    n = qkv.shape[0]
    seq = n // batch
    tl = min(tl, seq)
    nblk = seq // tl
    hb = tl // QKV_HALO
    nc_all = seq // DN_CHUNK
    full = lambda shape: pl.BlockSpec(shape, lambda b, i: (0,) * len(shape))
    rowblk = lambda w: pl.BlockSpec((tl, w), lambda b, i: (b * nblk + i, 0))
    outs = [jax.ShapeDtypeStruct((n, DN_WIDTH), BF16)] * 3 + [
        jax.ShapeDtypeStruct((DN_DIRS, n, DN_HEADS), F32), jax.ShapeDtypeStruct((DN_DIRS, n, DN_HEADS), F32),
        jax.ShapeDtypeStruct((DN_DIRS, batch, nc_all, DN_HEADS, DN_CHUNK), F32)]
    a_log = a_log.reshape(N_DH)
    dt_bias = dt_bias.reshape(N_DH)
    return pl.pallas_call(
        _dn_prep_kernel,
        out_shape=outs,
        grid=(batch, nblk),
        in_specs=[rowblk(3 * DN_WIDTH),
                  pl.BlockSpec((QKV_HALO, 3 * DN_WIDTH), lambda b, i: (jnp.maximum((b * nblk + i) * hb - 1, 0), 0)),
                  pl.BlockSpec((QKV_HALO, 3 * DN_WIDTH),
                               lambda b, i: (jnp.minimum((b * nblk + i + 1) * hb, n // QKV_HALO - 1), 0)),
                  rowblk(2 * N_DH),
                  pl.BlockSpec((2 * N_DH, tl), lambda b, i: (0, b * nblk + i)),
                  full((DN_CONV, 3 * DN_WIDTH)), full((1, N_DH)), full((1, N_DH)), full((N_DH, 1)), full((N_DH, 1)),
                  full((DN_WIDTH, DN_WIDTH))],
        out_specs=[rowblk(DN_WIDTH)] * 3 + [
            pl.BlockSpec((DN_DIRS, tl, DN_HEADS), lambda b, i: (0, b * nblk + i, 0)),
            pl.BlockSpec((DN_DIRS, tl, DN_HEADS), lambda b, i: (0, b * nblk + i, 0)),
            pl.BlockSpec((DN_DIRS, 1, tl // DN_CHUNK, DN_HEADS, DN_CHUNK), lambda b, i: (0, b, i, 0, 0))],
        compiler_params=_cparams("parallel", "parallel"),
        name="dn_prep",
    )(qkv, qkv, qkv, ab, abt, conv_w, a_log.reshape(1, N_DH), dt_bias.reshape(1, N_DH),
      a_log.reshape(N_DH, 1), dt_bias.reshape(N_DH, 1), _group_ones(DN_WIDTH, DN_HEAD_DIM))


DN_LOCAL_GROUP = 4


def _unit_tri_inverses(mats, block):
    n = mats[0].shape[0]
    r = lax.broadcasted_iota(jnp.int32, (n, n), 0)
    c = lax.broadcasted_iota(jnp.int32, (n, n), 1)
    eye = jnp.where(r == c, 1.0, 0.0)
    ds = [eye - jnp.where(r // 2 == c // 2, a, 0.0) for a in mats]
    size = 2
    while size < block:
        off = jnp.logical_and(r // (2 * size) == c // (2 * size), r // size != c // size)
        d16s = [d.astype(BF16) for d in ds]
        ts = [_dot(d16, jnp.where(off, a, 0.0).astype(BF16)) for d16, a in zip(d16s, mats)]
        ds = [d - _dot(t.astype(BF16), d16) for d, t, d16 in zip(ds, ts, d16s)]
        size *= 2
    return ds


def _dn_scan_kernel(q_ref, k_ref, v_ref, gcc_ref, beta_ref, gcr_ref, s0_ref, o_ref, s_ref,
                    rg_scr, n_scr, h_scr, eg_scr):
    d = pl.program_id(0)
    i = pl.program_id(1)
    nb, nc = rg_scr.shape[0], rg_scr.shape[1]
    per_batch = max(DN_LOCAL_GROUP // nb, 1)
    cs = DN_CHUNK
    hd = DN_HEAD_DIM
    fwd = d == 0

    @pl.when(i == 0)
    def _():
        s_ref[...] = s0_ref[...]

    hc = DN_HEADS * cs
    r = lax.broadcasted_iota(jnp.int32, (hc, hc), 0)
    c = lax.broadcasted_iota(jnp.int32, (hc, hc), 1)
    same_head = (r // cs) == (c // cs)
    rel = (r % cs - c % cs) * jnp.where(fwd, 1, -1)
    incl = jnp.logical_and(same_head, rel >= 0)
    strict = jnp.logical_and(same_head, rel > 0)
    last = jnp.where(fwd, cs - 1, 0)

    def load(b, ci):
        row0 = pl.multiple_of(ci * cs, cs)
        stack = lambda ref: jnp.concatenate(
            [ref[b, pl.ds(row0, cs), h * hd:(h + 1) * hd] for h in range(DN_HEADS)], axis=0)
        gate = lambda ref, rows: jnp.concatenate([ref[0, b, rows, h:h + 1] for h in range(DN_HEADS)], axis=0)
        q, k, v = stack(q_ref), stack(k_ref), stack(v_ref).astype(F32)
        gcol = gate(gcc_ref, pl.ds(row0, cs))
        bcol = gate(beta_ref, pl.ds(row0, cs))
        grows = gcr_ref[0, b, ci]
        grow = jnp.concatenate([grows[h:h + 1, :] for h in range(DN_HEADS)], axis=1)
        glast_h = [gcc_ref[0, b, pl.ds(row0 + last, 1), h:h + 1] for h in range(DN_HEADS)]
        glast = jnp.concatenate([jnp.broadcast_to(g, (cs, 1)) for g in glast_h], axis=0)
        decay = jnp.where(incl, jnp.exp(jnp.where(incl, gcol - grow, 0.0)), 0.0)
        kf = k.astype(F32)
        kb = kf * bcol
        eg = jnp.exp(gcol)
        gram = _dot_nt(jnp.concatenate([kb.astype(BF16), q], axis=0), k)
        a = jnp.where(strict, gram[0:hc] * decay, 0.0)
        rhs = jnp.concatenate([v * bcol, kb * eg], axis=1).astype(BF16)
        qk = jnp.where(incl, gram[hc:] * decay, 0.0).astype(BF16)
        kd = (kf * jnp.exp(glast - gcol)).astype(BF16)
        return dict(b=b, ci=ci, a=a, rhs=rhs, qk=qk, kd=kd, qe=q.astype(F32) * eg, glast_h=glast_h)

    def local_group(j, carry):
        chunks = [load(b, per_batch * j + g) for b in range(nb) for g in range(per_batch)]
        ts = _unit_tri_inverses([ch["a"] for ch in chunks], cs)
        uws = [_dot(t.astype(BF16), ch["rhs"]).astype(BF16) for t, ch in zip(ts, chunks)]
        hgs = [_dot(ch["qk"], uw) for ch, uw in zip(chunks, uws)]
        for ch, uw, hg in zip(chunks, uws, hgs):
            b, ci = ch["b"], ch["ci"]
            g_mat = (ch["qe"] - hg[:, hd:]).astype(BF16)
            for h in range(DN_HEADS):
                rows = slice(h * cs, (h + 1) * cs)
                nr = _dot_tn(ch["kd"][rows], uw[rows])
                n_scr[b, ci, h] = nr[:, 0:hd]
                h_scr[b, ci, h] = hg[rows, 0:hd]
                rg_scr[b, ci, h, 0:hd, :] = nr[:, hd:].astype(BF16)
                rg_scr[b, ci, h, hd:hd + cs, :] = g_mat[rows]
                eg_scr[b, ci, h] = jnp.broadcast_to(jnp.exp(ch["glast_h"][h]), (1, hd))
        return carry

    lax.fori_loop(0, nc // per_batch, local_group, 0)

    def step(cc, carry):
        ci = jnp.where(fwd, cc, nc - 1 - cc)
        row0 = pl.multiple_of(ci * cs, cs)
        for b in range(nb):
            for h in range(DN_HEADS):
                s = s_ref[b, 0, h]
                y = _dot(rg_scr[b, ci, h], s.astype(BF16))
                s_ref[b, 0, h] = s * eg_scr[b, ci, h] - y[0:hd] + n_scr[b, ci, h]
                o_ref[0, b, pl.ds(row0, cs), h * hd:(h + 1) * hd] = (
                    y[hd:hd + cs] + h_scr[b, ci, h]).astype(o_ref.dtype)
        return carry

    lax.fori_loop(0, nc, step, 0)


def dn_scan(q, k, v, gcc, beta, gcr, s0, *, batch, tl=512):
    n = q.shape[0]
    seq = n // batch
    tl = min(tl, seq)
    nblk = seq // tl
    nc = tl // DN_CHUNK
    blk = lambda d, i: jnp.where(d == 0, i, nblk - 1 - i)
    rowspec = pl.BlockSpec((batch, tl, DN_WIDTH), lambda d, i: (0, blk(d, i), 0))
    gate = pl.BlockSpec((1, batch, tl, DN_HEADS), lambda d, i: (d, 0, blk(d, i), 0))
    state = pl.BlockSpec((batch, 1, DN_HEADS, DN_HEAD_DIM, DN_HEAD_DIM), lambda d, i: (0, d, 0, 0, 0))
    per_chunk = (batch, nc, DN_HEADS)
    rows3 = lambda t: t.reshape(batch, seq, DN_WIDTH)
    gates4 = lambda t: t.reshape(DN_DIRS, batch, seq, DN_HEADS)
    o, s_fin = pl.pallas_call(
        _dn_scan_kernel,
        out_shape=[jax.ShapeDtypeStruct((DN_DIRS, batch, seq, DN_WIDTH), BF16), jax.ShapeDtypeStruct(s0.shape, F32)],
        grid=(DN_DIRS, nblk),
        in_specs=[rowspec, rowspec, rowspec, gate, gate,
                  pl.BlockSpec((1, batch, nc, DN_HEADS, DN_CHUNK), lambda d, i: (d, 0, blk(d, i), 0, 0)),
                  state],
        out_specs=[pl.BlockSpec((1, batch, tl, DN_WIDTH), lambda d, i: (d, 0, blk(d, i), 0)), state],
        scratch_shapes=[pltpu.VMEM(per_chunk + (DN_HEAD_DIM + DN_CHUNK, DN_HEAD_DIM), BF16),
                        pltpu.VMEM(per_chunk + (DN_HEAD_DIM, DN_HEAD_DIM), F32),
                        pltpu.VMEM(per_chunk + (DN_CHUNK, DN_HEAD_DIM), F32),
                        pltpu.VMEM(per_chunk + (1, DN_HEAD_DIM), F32)],
        compiler_params=_cparams("parallel", "arbitrary"),
        name="dn_scan",
    )(rows3(q), rows3(k), rows3(v), gates4(gcc), gates4(beta), gcr, s0)
    return o.reshape(DN_DIRS, n, DN_WIDTH), s_fin


NA_ROWS_PER_STEP = 8
NA_ROW_GROUP = 4
MASKED = -1e30


def na_bias_table(rpb):
    qc = np.arange(GRID_W)[:, None]
    kc = np.arange(GRID_W)[None, :]
    c0 = np.clip(qc - NA_WIN_W // 2, 0, GRID_W - NA_WIN_W)
    valid = (kc >= c0) & (kc < c0 + NA_WIN_W)
    rel_c = kc - qc + NA_WIN_W - 1
    case = np.arange(NA_WIN_H)[:, None]
    i = np.arange(NA_WIN_H)[None, :]
    rel_r = i + NA_WIN_H - 1 - case
    pick_r = (rel_r[:, :, None] == np.arange(2 * NA_WIN_H - 1)).astype(np.float32)
    pick_c = ((rel_c[:, :, None] == np.arange(2 * NA_WIN_W - 1)) & valid[:, :, None]).astype(np.float32)
    tab = jnp.einsum('hrc,xir,qkc->xhqik', rpb, pick_r, pick_c, precision=lax.Precision.HIGHEST)
    tab = jnp.where(valid[None, None, :, None, :], tab, MASKED)
    return tab.reshape(NA_WIN_H, NA_HEADS * GRID_W, NA_WIN_H * GRID_W)


def _head_masks():
    lane = lax.broadcasted_iota(jnp.int32, (1, NA_WIDTH), 1)
    return [(lane // NA_HEAD_DIM == h) for h in range(NA_HEADS)]


def _na_kernel(grid_rows, q_ref, kp_ref, kc_ref, kn_ref, vp_ref, vc_ref, vn_ref, kx_ref, vx_ref, bias_ref, o_ref,
               kwin, vwin):
    j = pl.program_id(1)
    tq = q_ref.shape[0]
    nkeys = NA_WIN_H * GRID_W
    kwin[0:tq] = kp_ref[...]
    kwin[tq:2 * tq] = kc_ref[...]
    kwin[2 * tq:3 * tq] = kn_ref[...]
    vwin[0:tq] = vp_ref[...]
    vwin[tq:2 * tq] = vc_ref[...]
    vwin[2 * tq:3 * tq] = vn_ref[...]
    kx = kx_ref[...]
    vx = vx_ref[...]
    hq = NA_HEADS * GRID_W
    own = (lax.broadcasted_iota(jnp.int32, (hq, NA_WIDTH), 0) // GRID_W
           == lax.broadcasted_iota(jnp.int32, (hq, NA_WIDTH), 1) // NA_HEAD_DIM)
    for g in range(NA_ROWS_PER_STEP // NA_ROW_GROUP):
        rls = [g * NA_ROW_GROUP + t for t in range(NA_ROW_GROUP)]
        scores = []
        for rl in rls:
            r = j * NA_ROWS_PER_STEP + rl
            r0 = jnp.clip(r - NA_WIN_H // 2, 0, grid_rows - NA_WIN_H)
            start = pl.multiple_of((r0 - (j - 1) * NA_ROWS_PER_STEP) * GRID_W, GRID_W)
            q = q_ref[rl * GRID_W:(rl + 1) * GRID_W, :]
            qs = jnp.where(own, jnp.concatenate([q] * NA_HEADS, axis=0), jnp.zeros((hq, NA_WIDTH), q.dtype))
            s_loc = _dot_nt(qs, kwin[pl.ds(start, nkeys), :]) + bias_ref[r - r0]
            scores.append((s_loc, _dot_nt(qs, kx), start))
        probs = []
        for s_loc, s_ctx, start in scores:
            m = jnp.maximum(jnp.max(s_loc, axis=-1, keepdims=True), jnp.max(s_ctx, axis=-1, keepdims=True))
            p_loc = jnp.exp(s_loc - m)
            p_ctx = jnp.exp(s_ctx - m)
            denom = jnp.sum(p_loc, axis=-1, keepdims=True) + jnp.sum(p_ctx, axis=-1, keepdims=True)
            probs.append((p_loc.astype(BF16), p_ctx.astype(BF16), denom, start))
        for rl, (p_loc, p_ctx, denom, start) in zip(rls, probs):
            pv = _dot(p_loc, vwin[pl.ds(start, nkeys), :]) + _dot(p_ctx, vx)
            pv = jnp.where(own, pv / denom, 0.0)
            out = pv[0:GRID_W]
            for h in range(1, NA_HEADS):
                out = out + pv[h * GRID_W:(h + 1) * GRID_W]
            o_ref[rl * GRID_W:(rl + 1) * GRID_W, :] = out.astype(o_ref.dtype)


def na_attention(q, k, v, k_ctx, v_ctx, bias, *, batch):
    n = q.shape[0]
    seq = n // batch
    ctx_len = k_ctx.shape[0] // batch
    grid_rows = seq // GRID_W
    tq = NA_ROWS_PER_STEP * GRID_W
    nblk = seq // tq
    cur = lambda b, j: (b * nblk + j, 0)
    prev = lambda b, j: (b * nblk + jnp.maximum(j - 1, 0), 0)
    nxt = lambda b, j: (b * nblk + jnp.minimum(j + 1, nblk - 1), 0)
    blk = lambda f: pl.BlockSpec((tq, NA_WIDTH), f)
    ctx = pl.BlockSpec((ctx_len, NA_WIDTH), lambda b, j: (b, 0))
    return pl.pallas_call(
        functools.partial(_na_kernel, grid_rows),
        out_shape=jax.ShapeDtypeStruct((n, NA_WIDTH), BF16),
        grid=(batch, nblk),
        in_specs=[blk(cur), blk(prev), blk(cur), blk(nxt), blk(prev), blk(cur), blk(nxt), ctx, ctx,
                  pl.BlockSpec(bias.shape, lambda b, j: (0, 0, 0))],
        out_specs=blk(cur),
        scratch_shapes=[pltpu.VMEM((3 * tq, NA_WIDTH), BF16), pltpu.VMEM((3 * tq, NA_WIDTH), BF16)],
        compiler_params=_cparams("parallel", "parallel"),
        name="na_attention",
    )(q, k, k, k, v, v, v, k_ctx, v_ctx, bias)


def _ctx_attn_kernel(q_ref, k_ref, v_ref, o_ref):
    q = q_ref[...]
    k = k_ref[...]
    v = v_ref[...]
    masks = _head_masks()
    acc = jnp.zeros(q.shape, F32)
    for h in range(NA_HEADS):
        s = _dot_nt(jnp.where(masks[h], q, jnp.zeros_like(q)), k)
        p = jnp.exp(s - jnp.max(s, axis=-1, keepdims=True))
        pv = _dot(p.astype(BF16), v) / jnp.sum(p, axis=-1, keepdims=True)
        acc = acc + jnp.where(masks[h], pv, 0.0)
    o_ref[...] = acc.astype(o_ref.dtype)


def ctx_attention(q, k, v, *, batch):
    n = q.shape[0]
    blk = pl.BlockSpec((n // batch, NA_WIDTH), lambda b: (b, 0))
    return pl.pallas_call(
        _ctx_attn_kernel,
        out_shape=jax.ShapeDtypeStruct((n, NA_WIDTH), BF16),
        grid=(batch,),
        in_specs=[blk, blk, blk],
        out_specs=blk,
        compiler_params=_cparams("parallel"),
        name="ctx_attention",
    )(q, k, v)


FT_N1 = 128
FT_T2_BLOCK = 8
FT_K1_BLOCK = 8


def _dft_cos_sin(n, scale=1.0):
    ang = 2.0 * np.pi * np.outer(np.arange(n), np.arange(n)) / n
    return np.cos(ang) * scale, np.sin(ang) * scale


def _channel_dft():
    c, s = _dft_cos_sin(FT_GROUP_DIM)
    eye = np.eye(FT_GROUPS)
    return np.concatenate([np.kron(eye, c), -np.kron(eye, s)], axis=1)


def _ft_stage1_kernel(u_ref, cs_ref, ff_ref, yr_ref, yi_ref):
    n1 = u_ref.shape[1]
    for s in range(u_ref.shape[2] // FT_WIDTH):
        lanes = slice(s * FT_WIDTH, (s + 1) * FT_WIDTH)
        z = _dot(u_ref[0, :, lanes], cs_ref[...])
        zz = jnp.concatenate([z[:, 0:FT_WIDTH], z[:, FT_WIDTH:]], axis=0).astype(BF16)
        y = _dot(ff_ref[...], zz)
        yr_ref[0, :, s, :] = y[0:n1]
        yi_ref[0, :, s, :] = y[n1:]


def _ft_stage2_kernel(yr_ref, yi_ref, twr_ref, twi_ref, g_ref, w_ref, o_ref):
    for i in range(yr_ref.shape[1]):
        yr = yr_ref[0, i]
        yi = yi_ref[0, i]
        tr = twr_ref[0, :, i:i + 1]
        ti = twi_ref[0, :, i:i + 1]
        yy = jnp.concatenate([yr * tr - yi * ti, yr * ti + yi * tr], axis=0).astype(BF16)
        xr = _dot(g_ref[...], yy)
        o_ref[0, :, i, :] = _dot(xr.astype(BF16), w_ref[...]).astype(o_ref.dtype)


def fourier_mix(u, fno_w, *, batch):
    n = u.shape[0]
    seq = n // batch
    n1 = FT_N1
    n2 = seq // n1
    tb = min(FT_T2_BLOCK, n2)
    norm = 1.0 / math.sqrt(seq * FT_GROUP_DIM)
    c1, s1 = _dft_cos_sin(n1)
    ff = jnp.asarray(np.block([[c1, s1], [-s1, c1]]), BF16)
    yr, yi = pl.pallas_call(
        _ft_stage1_kernel,
        out_shape=[jax.ShapeDtypeStruct((batch, n1, n2, FT_WIDTH), F32)] * 2,
        grid=(batch, n2 // tb),
        in_specs=[pl.BlockSpec((1, n1, tb * FT_WIDTH), lambda b, j: (b, 0, j)),
                  pl.BlockSpec((FT_WIDTH, 2 * FT_WIDTH), lambda b, j: (0, 0)),
                  pl.BlockSpec((2 * n1, 2 * n1), lambda b, j: (0, 0))],
        out_specs=[pl.BlockSpec((1, n1, tb, FT_WIDTH), lambda b, j: (b, 0, j, 0))] * 2,
        compiler_params=_cparams("parallel", "parallel"),
        name="fourier_stage1",
    )(u.reshape(batch, n1, n2 * FT_WIDTH), jnp.asarray(_channel_dft(), BF16), ff)
    kb = FT_K1_BLOCK
    ang = 2.0 * np.pi * np.outer(np.arange(n1), np.arange(n2)) / seq
    tw = lambda f: jnp.asarray(f(ang).reshape(n1 // kb, kb, n2).transpose(0, 2, 1), F32)
    c2, s2 = _dft_cos_sin(n2, norm)
    g = jnp.asarray(np.concatenate([c2, s2], axis=1), BF16)
    out = pl.pallas_call(
        _ft_stage2_kernel,
        out_shape=jax.ShapeDtypeStruct((batch, n2, n1, FT_WIDTH), BF16),
        grid=(batch, n1 // kb),
        in_specs=[pl.BlockSpec((1, kb, n2, FT_WIDTH), lambda b, j: (b, j, 0, 0))] * 2 + [
            pl.BlockSpec((1, n2, kb), lambda b, j: (j, 0, 0))] * 2 + [
            pl.BlockSpec((n2, 2 * n2), lambda b, j: (0, 0)), pl.BlockSpec((FT_WIDTH, FT_WIDTH), lambda b, j: (0, 0))],
        out_specs=pl.BlockSpec((1, n2, kb, FT_WIDTH), lambda b, j: (b, 0, j, 0)),
        compiler_params=_cparams("parallel", "parallel"),
        name="fourier_stage2",
    )(yr, yi,
      tw(np.cos), tw(lambda a: -np.sin(a)), g, fno_w.astype(BF16))
    return out.reshape(n, FT_WIDTH)


def _ft_direct_kernel(u_ref, cs_ref, g_ref, w_ref, o_ref):
    z = _dot(u_ref[...], cs_ref[...])
    zz = jnp.concatenate([z[:, 0:FT_WIDTH], z[:, FT_WIDTH:]], axis=0).astype(BF16)
    xr = _dot(g_ref[...], zz)
    o_ref[...] = _dot(xr.astype(BF16), w_ref[...]).astype(o_ref.dtype)


def fourier_mix_direct(u, fno_w, *, batch):
    n = u.shape[0]
    seq = n // batch
    c, s = _dft_cos_sin(seq, 1.0 / math.sqrt(seq * FT_GROUP_DIM))
    g = jnp.asarray(np.concatenate([c, s], axis=1), BF16)
    return pl.pallas_call(
        _ft_direct_kernel,
        out_shape=jax.ShapeDtypeStruct((n, FT_WIDTH), BF16),
        grid=(batch,),
        in_specs=[pl.BlockSpec((seq, FT_WIDTH), lambda b: (b, 0)),
                  pl.BlockSpec((FT_WIDTH, 2 * FT_WIDTH), lambda b: (0, 0)),
                  pl.BlockSpec((seq, 2 * seq), lambda b: (0, 0)),
                  pl.BlockSpec((FT_WIDTH, FT_WIDTH), lambda b: (0, 0))],
        out_specs=pl.BlockSpec((seq, FT_WIDTH), lambda b: (b, 0)),
        compiler_params=_cparams("parallel"),
        name="fourier_direct",
    )(u, jnp.asarray(_channel_dft(), BF16), g, fno_w.astype(BF16))


def _mod_row(mod_ref, rows_per_mod, fixed_row, tm):
    row = fixed_row if fixed_row is not None else (pl.program_id(0) * tm) // rows_per_mod
    return mod_ref[pl.ds(row, 1), :]


def _mix_out_kernel(rows_per_mod, fixed_row, route, o_ref, z_ref, ob_ref, oc_ref, x_ref, mod_ref, nw_ref, ones_ref,
                    w_ref, g_ref, *rest):
    if route:
        wr_ref, x_out, h_out, e_out, p_out = rest
    else:
        x_out, h_out = rest
    tm = x_ref.shape[0]
    m = _mod_row(mod_ref, rows_per_mod, fixed_row, tm)
    for rows in _row_parts(tm):
        o = o_ref[0, rows, :].astype(F32) + o_ref[1, rows, :].astype(F32)
        oa = _head_rms(o, ones_ref[...], DN_HEAD_DIM, nw_ref[...]) * _silu(z_ref[rows, :].astype(F32))
        y = (_dot(oa.astype(BF16), w_ref[0:DN_WIDTH, :])
             + _dot(ob_ref[rows, :], w_ref[DN_WIDTH:DN_WIDTH + NA_WIDTH, :])
             + _dot(oc_ref[rows, :], w_ref[DN_WIDTH + NA_WIDTH:, :]))
        x = x_ref[rows, :] + m[:, 2 * D_MODEL:3 * D_MODEL] * y
        x_out[rows, :] = x
        h = _modulated_norm(x, g_ref[...], m[:, 3 * D_MODEL:4 * D_MODEL], m[:, 4 * D_MODEL:5 * D_MODEL])
        h_out[rows, :] = h.astype(h_out.dtype)
        if route:
            h_hi = h.astype(BF16)
            h_lo = (h - h_hi.astype(F32)).astype(BF16)
            both = _dot_nt(wr_ref[...], h_hi) + _dot_nt(wr_ref[...], h_lo)
            logits = both[0:N_EXPERTS] + both[N_EXPERTS:]
            idx = lax.broadcasted_iota(jnp.int32, logits.shape, 0)
            m1 = jnp.max(logits, axis=0, keepdims=True)
            e1 = jnp.min(jnp.where(logits == m1, idx, N_EXPERTS), axis=0, keepdims=True)
            rest_l = jnp.where(idx == e1, -jnp.inf, logits)
            m2 = jnp.max(rest_l, axis=0, keepdims=True)
            e2 = jnp.min(jnp.where(rest_l == m2, idx, N_EXPERTS), axis=0, keepdims=True)
            t = jnp.exp(m2 - m1)
            e_out[:, rows] = jnp.concatenate([e1, e2], axis=0)
            p_out[:, rows] = jnp.concatenate([1.0 / (1.0 + t), t / (1.0 + t)], axis=0)


def mix_out(o_dirs, z, ob, oc, x2d, mod, dn_norm_w, w_out, g_ffn, w_router=None, *, rows_per_mod, fixed_row=None,
            tm=512):
    n, d = x2d.shape
    tm = min(tm, n)
    route = w_router is not None
    full = lambda shape: pl.BlockSpec(shape, lambda i: (0,) * len(shape))
    rowblk = lambda w: pl.BlockSpec((tm, w), lambda i: (i, 0))
    ins = [o_dirs, z, ob, oc, x2d, mod, jnp.tile(dn_norm_w, DN_HEADS).reshape(1, DN_WIDTH),
           _group_ones(DN_WIDTH, DN_HEAD_DIM), w_out, g_ffn.reshape(1, d)]
    in_specs = [pl.BlockSpec((DN_DIRS, tm, DN_WIDTH), lambda i: (0, i, 0)), rowblk(DN_WIDTH), rowblk(NA_WIDTH),
                rowblk(FT_WIDTH), rowblk(d), full(mod.shape), full((1, DN_WIDTH)), full((DN_WIDTH, DN_WIDTH)),
                full(w_out.shape), full((1, d))]
    outs = [jax.ShapeDtypeStruct((n, d), F32), jax.ShapeDtypeStruct((n, d), F32 if route else BF16)]
    out_specs = [rowblk(d), rowblk(d)]
    if route:
        w_hi = w_router.astype(BF16)
        w_lo = (w_router - w_hi.astype(F32)).astype(BF16)
        ins.append(jnp.concatenate([w_hi, w_lo], axis=1).T)
        in_specs.append(full((2 * N_EXPERTS, d)))
        outs += [jax.ShapeDtypeStruct((TOP_K, n), jnp.int32), jax.ShapeDtypeStruct((TOP_K, n), F32)]
        out_specs += [pl.BlockSpec((TOP_K, tm), lambda i: (0, i))] * 2
    res = pl.pallas_call(
        functools.partial(_mix_out_kernel, rows_per_mod, fixed_row, route),
        out_shape=outs,
        grid=(n // tm,),
        in_specs=in_specs,
        out_specs=out_specs,
        compiler_params=_cparams("parallel"),
        name="mix_out_route" if route else "mix_out",
    )(*ins)
    if route:
        x_new, h_new, top_e, top_p = res
        return x_new, h_new, top_e.T, top_p.T
    return res


def _ffn_kernel(rows_per_mod, fixed_row, h_ref, x_ref, mod_ref, w1_ref, w3_ref, w2_ref, o_ref, acc_ref):
    f = pl.program_id(1)
    h = h_ref[...]
    part = _dot((_silu(_dot(h, w1_ref[...])) * _dot(h, w3_ref[...])).astype(BF16), w2_ref[...])

    last = f == pl.num_programs(1) - 1

    @pl.when(f == 0)
    def _():
        acc_ref[...] = part

    @pl.when(jnp.logical_and(f > 0, jnp.logical_not(last)))
    def _():
        acc_ref[...] += part

    @pl.when(last)
    def _():
        m = _mod_row(mod_ref, rows_per_mod, fixed_row, h_ref.shape[0])
        o_ref[...] = x_ref[...] + m[:, 5 * D_MODEL:6 * D_MODEL] * (acc_ref[...] + part)


def dense_ffn(h, x2d, mod, w1, w3, w2, *, rows_per_mod, fixed_row=None, tm=512, tf=1408):
    n, d = x2d.shape
    tm = min(tm, n)
    dff = w1.shape[1]
    assert dff // tf >= 2, "the first and last hidden tiles are handled by different branches"
    return pl.pallas_call(
        functools.partial(_ffn_kernel, rows_per_mod, fixed_row),
        out_shape=jax.ShapeDtypeStruct((n, d), F32),
        grid=(n // tm, dff // tf),
        in_specs=[pl.BlockSpec((tm, d), lambda i, f: (i, 0)), pl.BlockSpec((tm, d), lambda i, f: (i, 0)),
                  pl.BlockSpec(mod.shape, lambda i, f: (0, 0)),
                  pl.BlockSpec((d, tf), lambda i, f: (0, f)), pl.BlockSpec((d, tf), lambda i, f: (0, f)),
                  pl.BlockSpec((tf, d), lambda i, f: (f, 0))],
        out_specs=pl.BlockSpec((tm, d), lambda i, f: (i, 0)),
        scratch_shapes=[pltpu.VMEM((tm, d), F32)],
        compiler_params=_cparams("parallel", "arbitrary"),
        name="dense_ffn",
    )(h, x2d, mod, w1, w3, w2)


MOE_ROWS = 512
MOE_DMA_ROWS = 256


def _row_copy(src, src_row, dst, dst_row, sem):
    return pltpu.make_async_copy(src.at[pl.ds(src_row, 1)], dst.at[pl.ds(dst_row, 1)], sem)


def _moe_scatter_kernel(dest_ref, h_ref, xb_in, xb_out, sem):
    del xb_in
    tm = h_ref.shape[0]

    def start(t, carry):
        for j in range(TOP_K):
            _row_copy(h_ref, t, xb_out, dest_ref[0, 0, TOP_K * t + j], sem).start()
        return carry

    def wait(t, carry):
        for j in range(TOP_K):
            _row_copy(h_ref, 0, xb_out, 0, sem).wait()
        return carry

    lax.fori_loop(0, tm, start, 0, unroll=8)
    lax.fori_loop(0, tm, wait, 0, unroll=8)


def _moe_ffn_kernel(be_ref, na_ref, x_ref, w1_ref, w3_ref, w2_ref, y_ref, acc_ref):
    i = pl.program_id(0)
    f = pl.program_id(1)
    last = f == pl.num_programs(1) - 1
    active = i < na_ref[0]

    @pl.when(active)
    def _():
        h = x_ref[...].astype(BF16)
        part = _dot((_silu(_dot(h, w1_ref[0])) * _dot(h, w3_ref[0])).astype(BF16), w2_ref[0])

        @pl.when(f == 0)
        def _():
            acc_ref[...] = part

        @pl.when(jnp.logical_and(f > 0, jnp.logical_not(last)))
        def _():
            acc_ref[...] += part

        @pl.when(last)
        def _():
            y_ref[...] = acc_ref[...] + part

    @pl.when(jnp.logical_and(jnp.logical_not(active), last))
    def _():
        y_ref[...] = jnp.zeros_like(y_ref)


def _moe_combine_kernel(rows_per_mod, dest_ref, p_ref, x_ref, mod_ref, yb_hbm, o_ref, buf, sem):
    tm = x_ref.shape[0]

    def start(t, carry):
        for j in range(TOP_K):
            _row_copy(yb_hbm, dest_ref[0, 0, TOP_K * t + j], buf.at[j], t, sem).start()
        return carry

    def wait(t, carry):
        for j in range(TOP_K):
            _row_copy(yb_hbm, 0, buf.at[j], 0, sem).wait()
        return carry

    lax.fori_loop(0, tm, start, 0, unroll=8)
    lax.fori_loop(0, tm, wait, 0, unroll=8)
    p = p_ref[...]
    y = p[:, 0:1] * buf[0] + p[:, 1:2] * buf[1]
    m = _mod_row(mod_ref, rows_per_mod, None, tm)
    o_ref[...] = x_ref[...] + m[:, 5 * D_MODEL:6 * D_MODEL] * y


def moe_ffn(h, x2d, mod, top_e, top_p, w1, w3, w2, *, rows_per_mod, tf=1792):
    n, d = x2d.shape
    dff = w1.shape[2]
    assert dff // tf >= 2, "the first and last hidden tiles are handled by different branches"
    n_assign = n * TOP_K
    n_blocks = -(-n_assign // MOE_ROWS) + N_EXPERTS
    n_slots = n_blocks * MOE_ROWS
    e_flat = top_e.reshape(n_assign)
    onehot = (e_flat[:, None] == jnp.arange(N_EXPERTS, dtype=jnp.int32)[None, :]).astype(jnp.int32)
    csum = jnp.cumsum(onehot, axis=0)
    rank = jnp.sum(onehot * csum, axis=1) - 1
    counts = csum[-1]
    padded = (counts + MOE_ROWS - 1) // MOE_ROWS * MOE_ROWS
    pad_end = jnp.cumsum(padded)
    dest = (jnp.sum(onehot * (pad_end - padded)[None, :], axis=1) + rank).astype(jnp.int32)
    n_active = (pad_end[-1] // MOE_ROWS).astype(jnp.int32).reshape(1)
    blk_ids = jnp.minimum(jnp.arange(n_blocks, dtype=jnp.int32), n_active[0] - 1)
    block_e = jnp.minimum(jnp.searchsorted(pad_end, blk_ids * MOE_ROWS, side='right'), N_EXPERTS - 1).astype(jnp.int32)

    tmd = min(MOE_DMA_ROWS, n)
    dest2d = dest.reshape(n // tmd, 1, TOP_K * tmd)
    dest_spec = pl.BlockSpec((1, 1, TOP_K * tmd), lambda i: (i, 0, 0), memory_space=pltpu.SMEM)
    hbm = pl.BlockSpec(memory_space=pl.ANY)
    xb = pl.pallas_call(
        _moe_scatter_kernel,
        out_shape=jax.ShapeDtypeStruct((n_slots, d), F32),
        grid=(n // tmd,),
        in_specs=[dest_spec, pl.BlockSpec((tmd, d), lambda i: (i, 0)), hbm],
        out_specs=hbm,
        scratch_shapes=[pltpu.SemaphoreType.DMA],
        input_output_aliases={2: 0},
        compiler_params=_cparams("arbitrary"),
        name="moe_scatter",
    )(dest2d, h, jnp.zeros((n_slots, d), F32))

    yb = pl.pallas_call(
        _moe_ffn_kernel,
        out_shape=jax.ShapeDtypeStruct((n_slots, d), F32),
        grid_spec=pltpu.PrefetchScalarGridSpec(
            num_scalar_prefetch=2,
            grid=(n_blocks, dff // tf),
            in_specs=[pl.BlockSpec((MOE_ROWS, d), lambda i, f, be, na: (i, 0)),
                      pl.BlockSpec((1, d, tf), lambda i, f, be, na: (be[i], 0, f)),
                      pl.BlockSpec((1, d, tf), lambda i, f, be, na: (be[i], 0, f)),
                      pl.BlockSpec((1, tf, d), lambda i, f, be, na: (be[i], f, 0))],
            out_specs=pl.BlockSpec((MOE_ROWS, d), lambda i, f, be, na: (i, 0)),
            scratch_shapes=[pltpu.VMEM((MOE_ROWS, d), F32)]),
        compiler_params=_cparams("arbitrary", "arbitrary"),
        name="moe_expert_ffn",
    )(block_e, n_active, xb, w1, w3, w2)

    return pl.pallas_call(
        functools.partial(_moe_combine_kernel, rows_per_mod),
        out_shape=jax.ShapeDtypeStruct((n, d), F32),
        grid=(n // tmd,),
        in_specs=[dest_spec, pl.BlockSpec((tmd, TOP_K), lambda i: (i, 0)), pl.BlockSpec((tmd, d), lambda i: (i, 0)),
                  pl.BlockSpec(mod.shape, lambda i: (0, 0)), hbm],
        out_specs=pl.BlockSpec((tmd, d), lambda i: (i, 0)),
        scratch_shapes=[pltpu.VMEM((TOP_K, tmd, d), F32), pltpu.SemaphoreType.DMA],
        compiler_params=_cparams("arbitrary"),
        name="moe_combine",
    )(dest2d, top_p, x2d, mod, yb)


def _split_w_in(w_in_l):
    g0 = 4 * DN_WIDTH
    g1 = g0 + 2 * N_DH
    w_main = jnp.concatenate([w_in_l[:, :g0], w_in_l[:, g1:]], axis=1).astype(BF16)
    w_ab = w_in_l[:, g0:g1].astype(BF16)
    return w_main, w_ab, w_ab.T


def kernel(x, c, ctx, c_ctx, w_mod, b_mod, g_mix, g_ffn, w_in, dn_conv, dn_a_log, dn_dt_bias, dn_norm_w, na_q_norm,
           na_k_norm, na_rpb, fno_w, w_out, ffn_w1, ffn_w3, ffn_w2, moe_router, moe_w1, moe_w3, moe_w2):
    bsz, seq, d = x.shape
    ctx_len = ctx.shape[1]
    ctx_row = bsz
    cond = jnp.zeros((SUBLANES, d), F32).at[:bsz].set(c).at[ctx_row].set(c_ctx)
    mods = adaln_mod(cond, w_mod, b_mod)
    x2 = x.reshape(bsz * seq, d)
    c2 = ctx.reshape(bsz * ctx_len, d)
    zero_state = jnp.zeros((bsz, DN_DIRS, DN_HEADS, DN_HEAD_DIM, DN_HEAD_DIM), F32)

    for layer in range(DEPTH):
        need_ctx = layer < DEPTH - 1
        mod = mods[layer]
        w_main, w_ab, w_abt = _split_w_in(w_in[layer])
        proj = functools.partial(in_proj, mod=mod, g=g_mix[layer], w_main=w_main, w_ab=w_ab, w_abt=w_abt,
                                 q_norm=na_q_norm[layer], k_norm=na_k_norm[layer])
        qkv, z, ab, abt, nq, nk, nv, ft = proj(x2, rows_per_mod=seq)
        qkv_c, z_c, ab_c, abt_c, nq_c, nk_c, nv_c, ft_c = proj(c2, rows_per_mod=ctx_len, fixed_row=ctx_row)

        prep = functools.partial(dn_prep, conv_w=dn_conv[layer], a_log=dn_a_log[layer], dt_bias=dn_dt_bias[layer],
                                 batch=bsz)
        q_c, k_c, v_c, gcc_c, beta_c, gcr_c = prep(qkv_c, ab_c, abt_c)
        o_c, s_ctx = dn_scan(q_c, k_c, v_c, gcc_c, beta_c, gcr_c, zero_state, batch=bsz)
        q_l, k_l, v_l, gcc_l, beta_l, gcr_l = prep(qkv, ab, abt)
        o_l, _ = dn_scan(q_l, k_l, v_l, gcc_l, beta_l, gcr_l, s_ctx, batch=bsz)

        ob = na_attention(nq, nk, nv, nk_c, nv_c, na_bias_table(na_rpb[layer]), batch=bsz)
        oc = fourier_mix(ft, fno_w[layer], batch=bsz)

        w_out_l = w_out[layer].astype(BF16)
        j = layer // 2
        if layer % 2 == 0:
            x2, h2 = mix_out(o_l, z, ob, oc, x2, mod, dn_norm_w[layer], w_out_l, g_ffn[layer], rows_per_mod=seq)
            ffn_w = (ffn_w1[j].astype(BF16), ffn_w3[j].astype(BF16), ffn_w2[j].astype(BF16))
            x2 = dense_ffn(h2, x2, mod, *ffn_w, rows_per_mod=seq)
        else:
            x2, h2, top_e, top_p = mix_out(o_l, z, ob, oc, x2, mod, dn_norm_w[layer], w_out_l, g_ffn[layer],
                                           moe_router[j], rows_per_mod=seq)
            x2 = moe_ffn(h2, x2, mod, top_e, top_p, moe_w1[j].astype(BF16), moe_w3[j].astype(BF16),
                         moe_w2[j].astype(BF16), rows_per_mod=seq)
        if need_ctx:
            ob_c = ctx_attention(nq_c, nk_c, nv_c, batch=bsz)
            oc_c = fourier_mix_direct(ft_c, fno_w[layer], batch=bsz)
            assert layer % 2 == 0, "context tokens only pass through dense layers at this depth"
            c2, hc2 = mix_out(o_c, z_c, ob_c, oc_c, c2, mod, dn_norm_w[layer], w_out_l, g_ffn[layer],
                              rows_per_mod=ctx_len, fixed_row=ctx_row)
            c2 = dense_ffn(hc2, c2, mod, *ffn_w, rows_per_mod=ctx_len, fixed_row=ctx_row)
    return x2.reshape(bsz, seq, d)
```

```python
import functools
import math

import jax
import jax.numpy as jnp
import numpy as np
from jax import lax
from jax.experimental import pallas as pl
from jax.experimental.pallas import tpu as pltpu

D_MODEL = 1024
DEPTH = 2
GRID_W = 64
DN_HEADS = 4
DN_HEAD_DIM = 128
DN_WIDTH = DN_HEADS * DN_HEAD_DIM
DN_CONV = 3
DN_CHUNK = 64
DN_DIRS = 2
NA_HEADS = 4
NA_HEAD_DIM = 64
NA_WIDTH = NA_HEADS * NA_HEAD_DIM
NA_WIN_H = 8
NA_WIN_W = 16
FT_GROUPS = 4
FT_GROUP_DIM = 64
FT_WIDTH = FT_GROUPS * FT_GROUP_DIM
N_EXPERTS = 8
TOP_K = 2
N_MOD = 6
EPS = 1e-6

SUBLANES = 8
LANES = 128
VMEM_LIMIT = 48 * 1024 * 1024

BF16 = jnp.bfloat16
F32 = jnp.float32
N_DH = DN_DIRS * DN_HEADS


def _cparams(*sem):
    return pltpu.CompilerParams(dimension_semantics=sem, vmem_limit_bytes=VMEM_LIMIT)


def _silu(x):
    return x * jax.nn.sigmoid(x)


def _dot(a, b):
    return jnp.dot(a, b, preferred_element_type=F32)


def _dot_nt(a, b):
    return lax.dot_general(a, b, (((1,), (1,)), ((), ())), preferred_element_type=F32)


def _dot_tn(a, b):
    return lax.dot_general(a, b, (((0,), (0,)), ((), ())), preferred_element_type=F32)


def _group_ones(width, group):
    idx = np.arange(width) // group
    return jnp.asarray((idx[:, None] == idx[None, :]).astype(np.float32), BF16)


def _mod_kernel(c_ref, w_ref, b_ref, o_ref):
    s = _silu(c_ref[...])
    o_ref[0] = jnp.dot(s, w_ref[0], preferred_element_type=F32, precision=lax.Precision.HIGHEST) + b_ref[0]


def adaln_mod(cond_rows, w_mod, b_mod, *, tn=1536):
    depth, d, n = w_mod.shape
    rows = cond_rows.shape[0]
    return pl.pallas_call(
        _mod_kernel,
        out_shape=jax.ShapeDtypeStruct((depth, rows, n), F32),
        grid=(depth, n // tn),
        in_specs=[pl.BlockSpec((rows, d), lambda l, j: (0, 0)),
                  pl.BlockSpec((1, d, tn), lambda l, j: (l, 0, j)),
                  pl.BlockSpec((1, 1, tn), lambda l, j: (l, 0, j))],
        out_specs=pl.BlockSpec((1, rows, tn), lambda l, j: (l, 0, j)),
        compiler_params=_cparams("arbitrary", "arbitrary"),
        name="adaln_mod",
    )(cond_rows, w_mod, b_mod.reshape(depth, 1, n))


C_QKV = 0
C_Z = 3 * DN_WIDTH
C_NQ = C_Z + DN_WIDTH
C_NK = C_NQ + NA_WIDTH
C_NV = C_NK + NA_WIDTH
C_FT = C_NV + NA_WIDTH
C_END = C_FT + FT_WIDTH


def _modulated_norm(x, g, shift, scale):
    y = x * lax.rsqrt(jnp.mean(x * x, axis=-1, keepdims=True) + EPS)
    return y * g * (1.0 + scale) + shift


ROW_PART = 128


def _row_parts(tm):
    part = min(ROW_PART, tm)
    return [slice(r, r + part) for r in range(0, tm, part)]


def _head_rms(x, ones, width, gain):
    ss = _dot((x * x).astype(BF16), ones)
    return x * lax.rsqrt(ss * (1.0 / width) + EPS) * gain


def _in_proj_kernel(rows_per_mod, fixed_row, x_ref, mod_ref, g_ref, w_ref, wab_ref, wabt_ref, qn_ref, kn_ref,
                    ones_ref, qkv_ref, z_ref, ab_ref, abt_ref, nq_ref, nk_ref, nv_ref, ft_ref):
    tm = x_ref.shape[0]
    row = fixed_row if fixed_row is not None else (pl.program_id(0) * tm) // rows_per_mod
    m = mod_ref[pl.ds(row, 1), :]
    shift, scale = m[:, 0:D_MODEL], m[:, D_MODEL:2 * D_MODEL]
    ones = ones_ref[...]
    for rows in _row_parts(tm):
        h = _modulated_norm(x_ref[rows, :], g_ref[...], shift, scale).astype(BF16)
        qkv_ref[rows, :] = _dot(h, w_ref[:, C_QKV:C_Z]).astype(qkv_ref.dtype)
        z_ref[rows, :] = _dot(h, w_ref[:, C_Z:C_NQ]).astype(z_ref.dtype)
        nq = _dot(h, w_ref[:, C_NQ:C_NK])
        nq_ref[rows, :] = (_head_rms(nq, ones, NA_HEAD_DIM, qn_ref[...]) * (NA_HEAD_DIM ** -0.5)).astype(nq_ref.dtype)
        nk = _dot(h, w_ref[:, C_NK:C_NV])
        nk_ref[rows, :] = _head_rms(nk, ones, NA_HEAD_DIM, kn_ref[...]).astype(nk_ref.dtype)
        nv_ref[rows, :] = _dot(h, w_ref[:, C_NV:C_FT]).astype(nv_ref.dtype)
        ft_ref[rows, :] = _dot(h, w_ref[:, C_FT:C_END]).astype(ft_ref.dtype)
        ab_ref[rows, :] = _dot(h, wab_ref[...])
        abt_ref[:, rows] = _dot_nt(wabt_ref[...], h)


def in_proj(x2d, mod, g, w_main, w_ab, w_abt, q_norm, k_norm, *, rows_per_mod, fixed_row=None, tm=512):
    n, d = x2d.shape
    tm = min(tm, n)
    nab = w_ab.shape[1]
    full = lambda shape: pl.BlockSpec(shape, lambda i: (0,) * len(shape))
    rowblk = lambda w: pl.BlockSpec((tm, w), lambda i: (i, 0))
    outs = [jax.ShapeDtypeStruct((n, 3 * DN_WIDTH), BF16), jax.ShapeDtypeStruct((n, DN_WIDTH), BF16),
            jax.ShapeDtypeStruct((n, nab), F32), jax.ShapeDtypeStruct((nab, n), F32),
            jax.ShapeDtypeStruct((n, NA_WIDTH), BF16), jax.ShapeDtypeStruct((n, NA_WIDTH), BF16),
            jax.ShapeDtypeStruct((n, NA_WIDTH), BF16), jax.ShapeDtypeStruct((n, FT_WIDTH), BF16)]
    return pl.pallas_call(
        functools.partial(_in_proj_kernel, rows_per_mod, fixed_row),
        out_shape=outs,
        grid=(n // tm,),
        in_specs=[rowblk(d), full(mod.shape), full((1, d)), full(w_main.shape), full(w_ab.shape), full(w_abt.shape),
                  full((1, NA_WIDTH)), full((1, NA_WIDTH)), full((NA_WIDTH, NA_WIDTH))],
        out_specs=[rowblk(3 * DN_WIDTH), rowblk(DN_WIDTH), rowblk(nab), pl.BlockSpec((nab, tm), lambda i: (0, i)),
                   rowblk(NA_WIDTH), rowblk(NA_WIDTH), rowblk(NA_WIDTH), rowblk(FT_WIDTH)],
        compiler_params=_cparams("parallel"),
        name="in_proj",
    )(x2d, mod, g.reshape(1, d), w_main, w_ab, w_abt,
      jnp.tile(q_norm, NA_HEADS).reshape(1, NA_WIDTH), jnp.tile(k_norm, NA_HEADS).reshape(1, NA_WIDTH),
      _group_ones(NA_WIDTH, NA_HEAD_DIM))


QKV_HALO = 16


def _softplus(x):
    return jnp.maximum(x, 0.0) + jnp.log1p(jnp.exp(-jnp.abs(x)))


def _tri(n, lower):
    r = lax.broadcasted_iota(jnp.int32, (n, n), 0)
    c = lax.broadcasted_iota(jnp.int32, (n, n), 1)
    return jnp.where((r >= c) if lower else (r <= c), 1.0, 0.0).astype(F32)


def _dot_hi(a, b):
    return jnp.dot(a, b, preferred_element_type=F32, precision=lax.Precision.HIGHEST)


def _dn_prep_kernel(x_ref, xp_ref, xn_ref, ab_ref, abt_ref, cw_ref, alr_ref, dtr_ref, alc_ref, dtc_ref, ones_ref,
                    q_ref, k_ref, v_ref, gcc_ref, beta_ref, gcr_ref):
    i = pl.program_id(1)
    tl = x_ref.shape[0]
    x = x_ref[...].astype(F32)
    prev = jnp.where(i > 0, xp_ref[QKV_HALO - 1:QKV_HALO, :].astype(F32), 0.0)
    nxt = jnp.where(i < pl.num_programs(1) - 1, xn_ref[0:1, :].astype(F32), 0.0)
    rows = lax.broadcasted_iota(jnp.int32, x.shape, 0)
    xm1 = jnp.where(rows == 0, prev, pltpu.roll(x, 1, axis=0))
    xp1 = jnp.where(rows == tl - 1, nxt, pltpu.roll(x, tl - 1, axis=0))
    y = _silu(xm1 * cw_ref[0:1, :] + x * cw_ref[1:2, :] + xp1 * cw_ref[2:3, :])
    ones = ones_ref[...]
    q = y[:, 0:DN_WIDTH]
    k = y[:, DN_WIDTH:2 * DN_WIDTH]
    q_ref[...] = (q * lax.rsqrt(_dot((q * q).astype(BF16), ones) + EPS) * (DN_HEAD_DIM ** -0.5)).astype(q_ref.dtype)
    k_ref[...] = (k * lax.rsqrt(_dot((k * k).astype(BF16), ones) + EPS)).astype(k_ref.dtype)
    v_ref[...] = y[:, 2 * DN_WIDTH:].astype(v_ref.dtype)

    ab = ab_ref[...]
    g_col = -jnp.exp(alr_ref[...]) * _softplus(ab[:, 0:N_DH] + dtr_ref[...])
    beta = jax.nn.sigmoid(ab[:, N_DH:2 * N_DH])
    beta_ref[0] = beta[:, 0:DN_HEADS]
    beta_ref[1] = beta[:, DN_HEADS:]
    g_row = -jnp.exp(alc_ref[...]) * _softplus(abt_ref[0:N_DH, :] + dtc_ref[...])
    lo, up = _tri(DN_CHUNK, True), _tri(DN_CHUNK, False)
    sub = lax.broadcasted_iota(jnp.int32, (N_DH, DN_CHUNK), 0)
    for c in range(tl // DN_CHUNK):
        sl = slice(c * DN_CHUNK, (c + 1) * DN_CHUNK)
        gc = g_col[sl, :]
        gcc_ref[0, sl, :] = _dot_hi(lo, gc)[:, 0:DN_HEADS]
        gcc_ref[1, sl, :] = _dot_hi(up, gc)[:, DN_HEADS:]
        gr = g_row[:, sl]
        cs = jnp.where(sub < DN_HEADS, _dot_hi(gr, up), _dot_hi(gr, lo))
        gcr_ref[0, 0, c] = cs[0:DN_HEADS]
        gcr_ref[1, 0, c] = cs[DN_HEADS:]


def dn_prep(qkv, ab, abt, conv_w, a_log, dt_bias, *, batch, tl=512):
    n = qkv.shape[0]
    seq = n // batch
    tl = min(tl, seq)
    nblk = seq // tl
    hb = tl // QKV_HALO
    nc_all = seq // DN_CHUNK
    full = lambda shape: pl.BlockSpec(shape, lambda b, i: (0,) * len(shape))
    rowblk = lambda w: pl.BlockSpec((tl, w), lambda b, i: (b * nblk + i, 0))
    outs = [jax.ShapeDtypeStruct((n, DN_WIDTH), BF16)] * 3 + [
        jax.ShapeDtypeStruct((DN_DIRS, n, DN_HEADS), F32), jax.ShapeDtypeStruct((DN_DIRS, n, DN_HEADS), F32),
        jax.ShapeDtypeStruct((DN_DIRS, batch, nc_all, DN_HEADS, DN_CHUNK), F32)]
    a_log = a_log.reshape(N_DH)
    dt_bias = dt_bias.reshape(N_DH)
    return pl.pallas_call(
        _dn_prep_kernel,
        out_shape=outs,
        grid=(batch, nblk),
        in_specs=[rowblk(3 * DN_WIDTH),
                  pl.BlockSpec((QKV_HALO, 3 * DN_WIDTH), lambda b, i: (jnp.maximum((b * nblk + i) * hb - 1, 0), 0)),
                  pl.BlockSpec((QKV_HALO, 3 * DN_WIDTH),
                               lambda b, i: (jnp.minimum((b * nblk + i + 1) * hb, n // QKV_HALO - 1), 0)),
                  rowblk(2 * N_DH),
                  pl.BlockSpec((2 * N_DH, tl), lambda b, i: (0, b * nblk + i)),
                  full((DN_CONV, 3 * DN_WIDTH)), full((1, N_DH)), full((1, N_DH)), full((N_DH, 1)), full((N_DH, 1)),
                  full((DN_WIDTH, DN_WIDTH))],
        out_specs=[rowblk(DN_WIDTH)] * 3 + [
            pl.BlockSpec((DN_DIRS, tl, DN_HEADS), lambda b, i: (0, b * nblk + i, 0)),
            pl.BlockSpec((DN_DIRS, tl, DN_HEADS), lambda b, i: (0, b * nblk + i, 0)),
            pl.BlockSpec((DN_DIRS, 1, tl // DN_CHUNK, DN_HEADS, DN_CHUNK), lambda b, i: (0, b, i, 0, 0))],
        compiler_params=_cparams("parallel", "parallel"),
        name="dn_prep",
    )(qkv, qkv, qkv, ab, abt, conv_w, a_log.reshape(1, N_DH), dt_bias.reshape(1, N_DH),
      a_log.reshape(N_DH, 1), dt_bias.reshape(N_DH, 1), _group_ones(DN_WIDTH, DN_HEAD_DIM))


DN_LOCAL_GROUP = 4


def _unit_tri_inverses(mats, block):
    n = mats[0].shape[0]
    r = lax.broadcasted_iota(jnp.int32, (n, n), 0)
    c = lax.broadcasted_iota(jnp.int32, (n, n), 1)
    eye = jnp.where(r == c, 1.0, 0.0)
    ds = [eye - jnp.where(r // 2 == c // 2, a, 0.0) for a in mats]
    size = 2
    while size < block:
        off = jnp.logical_and(r // (2 * size) == c // (2 * size), r // size != c // size)
        d16s = [d.astype(BF16) for d in ds]
        ts = [_dot(d16, jnp.where(off, a, 0.0).astype(BF16)) for d16, a in zip(d16s, mats)]
        ds = [d - _dot(t.astype(BF16), d16) for d, t, d16 in zip(ds, ts, d16s)]
        size *= 2
    return ds


def _dn_scan_kernel(q_ref, k_ref, v_ref, gcc_ref, beta_ref, gcr_ref, s0_ref, o_ref, s_ref,
                    rg_scr, n_scr, h_scr, eg_scr):
    d = pl.program_id(0)
    i = pl.program_id(1)
    nb, nc = rg_scr.shape[0], rg_scr.shape[1]
    per_batch = max(DN_LOCAL_GROUP // nb, 1)
    cs = DN_CHUNK
    hd = DN_HEAD_DIM
    fwd = d == 0

    @pl.when(i == 0)
    def _():
        s_ref[...] = s0_ref[...]

    hc = DN_HEADS * cs
    r = lax.broadcasted_iota(jnp.int32, (hc, hc), 0)
    c = lax.broadcasted_iota(jnp.int32, (hc, hc), 1)
    same_head = (r // cs) == (c // cs)
    rel = (r % cs - c % cs) * jnp.where(fwd, 1, -1)
    incl = jnp.logical_and(same_head, rel >= 0)
    strict = jnp.logical_and(same_head, rel > 0)
    last = jnp.where(fwd, cs - 1, 0)

    def load(b, ci):
        row0 = pl.multiple_of(ci * cs, cs)
        stack = lambda ref: jnp.concatenate(
            [ref[b, pl.ds(row0, cs), h * hd:(h + 1) * hd] for h in range(DN_HEADS)], axis=0)
        gate = lambda ref, rows: jnp.concatenate([ref[0, b, rows, h:h + 1] for h in range(DN_HEADS)], axis=0)
        q, k, v = stack(q_ref), stack(k_ref), stack(v_ref).astype(F32)
        gcol = gate(gcc_ref, pl.ds(row0, cs))
        bcol = gate(beta_ref, pl.ds(row0, cs))
        grows = gcr_ref[0, b, ci]
        grow = jnp.concatenate([grows[h:h + 1, :] for h in range(DN_HEADS)], axis=1)
        glast_h = [gcc_ref[0, b, pl.ds(row0 + last, 1), h:h + 1] for h in range(DN_HEADS)]
        glast = jnp.concatenate([jnp.broadcast_to(g, (cs, 1)) for g in glast_h], axis=0)
        decay = jnp.where(incl, jnp.exp(jnp.where(incl, gcol - grow, 0.0)), 0.0)
        kf = k.astype(F32)
        kb = kf * bcol
        eg = jnp.exp(gcol)
        gram = _dot_nt(jnp.concatenate([kb.astype(BF16), q], axis=0), k)
        a = jnp.where(strict, gram[0:hc] * decay, 0.0)
        rhs = jnp.concatenate([v * bcol, kb * eg], axis=1).astype(BF16)
        qk = jnp.where(incl, gram[hc:] * decay, 0.0).astype(BF16)
        kd = (kf * jnp.exp(glast - gcol)).astype(BF16)
        return dict(b=b, ci=ci, a=a, rhs=rhs, qk=qk, kd=kd, qe=q.astype(F32) * eg, glast_h=glast_h)

    def local_group(j, carry):
        chunks = [load(b, per_batch * j + g) for b in range(nb) for g in range(per_batch)]
        ts = _unit_tri_inverses([ch["a"] for ch in chunks], cs)
        uws = [_dot(t.astype(BF16), ch["rhs"]).astype(BF16) for t, ch in zip(ts, chunks)]
        hgs = [_dot(ch["qk"], uw) for ch, uw in zip(chunks, uws)]
        for ch, uw, hg in zip(chunks, uws, hgs):
            b, ci = ch["b"], ch["ci"]
            g_mat = (ch["qe"] - hg[:, hd:]).astype(BF16)
            for h in range(DN_HEADS):
                rows = slice(h * cs, (h + 1) * cs)
                nr = _dot_tn(ch["kd"][rows], uw[rows])
                n_scr[b, ci, h] = nr[:, 0:hd]
                h_scr[b, ci, h] = hg[rows, 0:hd]
                rg_scr[b, ci, h, 0:hd, :] = nr[:, hd:].astype(BF16)
                rg_scr[b, ci, h, hd:hd + cs, :] = g_mat[rows]
                eg_scr[b, ci, h] = jnp.broadcast_to(jnp.exp(ch["glast_h"][h]), (1, hd))
        return carry

    lax.fori_loop(0, nc // per_batch, local_group, 0)

    def step(cc, carry):
        ci = jnp.where(fwd, cc, nc - 1 - cc)
        row0 = pl.multiple_of(ci * cs, cs)
        for b in range(nb):
            for h in range(DN_HEADS):
                s = s_ref[b, 0, h]
                y = _dot(rg_scr[b, ci, h], s.astype(BF16))
                s_ref[b, 0, h] = s * eg_scr[b, ci, h] - y[0:hd] + n_scr[b, ci, h]
                o_ref[0, b, pl.ds(row0, cs), h * hd:(h + 1) * hd] = (
                    y[hd:hd + cs] + h_scr[b, ci, h]).astype(o_ref.dtype)
        return carry

    lax.fori_loop(0, nc, step, 0)


def dn_scan(q, k, v, gcc, beta, gcr, s0, *, batch, tl=512):
    n = q.shape[0]
    seq = n // batch
    tl = min(tl, seq)
    nblk = seq // tl
    nc = tl // DN_CHUNK
    blk = lambda d, i: jnp.where(d == 0, i, nblk - 1 - i)
    rowspec = pl.BlockSpec((batch, tl, DN_WIDTH), lambda d, i: (0, blk(d, i), 0))
    gate = pl.BlockSpec((1, batch, tl, DN_HEADS), lambda d, i: (d, 0, blk(d, i), 0))
    state = pl.BlockSpec((batch, 1, DN_HEADS, DN_HEAD_DIM, DN_HEAD_DIM), lambda d, i: (0, d, 0, 0, 0))
    per_chunk = (batch, nc, DN_HEADS)
    rows3 = lambda t: t.reshape(batch, seq, DN_WIDTH)
    gates4 = lambda t: t.reshape(DN_DIRS, batch, seq, DN_HEADS)
    o, s_fin = pl.pallas_call(
        _dn_scan_kernel,
        out_shape=[jax.ShapeDtypeStruct((DN_DIRS, batch, seq, DN_WIDTH), BF16), jax.ShapeDtypeStruct(s0.shape, F32)],
        grid=(DN_DIRS, nblk),
        in_specs=[rowspec, rowspec, rowspec, gate, gate,
                  pl.BlockSpec((1, batch, nc, DN_HEADS, DN_CHUNK), lambda d, i: (d, 0, blk(d, i), 0, 0)),
                  state],
        out_specs=[pl.BlockSpec((1, batch, tl, DN_WIDTH), lambda d, i: (d, 0, blk(d, i), 0)), state],
        scratch_shapes=[pltpu.VMEM(per_chunk + (DN_HEAD_DIM + DN_CHUNK, DN_HEAD_DIM), BF16),
                        pltpu.VMEM(per_chunk + (DN_HEAD_DIM, DN_HEAD_DIM), F32),
                        pltpu.VMEM(per_chunk + (DN_CHUNK, DN_HEAD_DIM), F32),
                        pltpu.VMEM(per_chunk + (1, DN_HEAD_DIM), F32)],
        compiler_params=_cparams("parallel", "arbitrary"),
        name="dn_scan",
    )(rows3(q), rows3(k), rows3(v), gates4(gcc), gates4(beta), gcr, s0)
    return o.reshape(DN_DIRS, n, DN_WIDTH), s_fin


NA_ROWS_PER_STEP = 8
NA_ROW_GROUP = 4
MASKED = -1e30


def na_bias_table(rpb):
    qc = np.arange(GRID_W)[:, None]
    kc = np.arange(GRID_W)[None, :]
    c0 = np.clip(qc - NA_WIN_W // 2, 0, GRID_W - NA_WIN_W)
    valid = (kc >= c0) & (kc < c0 + NA_WIN_W)
    rel_c = kc - qc + NA_WIN_W - 1
    case = np.arange(NA_WIN_H)[:, None]
    i = np.arange(NA_WIN_H)[None, :]
    rel_r = i + NA_WIN_H - 1 - case
    pick_r = (rel_r[:, :, None] == np.arange(2 * NA_WIN_H - 1)).astype(np.float32)
    pick_c = ((rel_c[:, :, None] == np.arange(2 * NA_WIN_W - 1)) & valid[:, :, None]).astype(np.float32)
    tab = jnp.einsum('hrc,xir,qkc->xhqik', rpb, pick_r, pick_c, precision=lax.Precision.HIGHEST)
    tab = jnp.where(valid[None, None, :, None, :], tab, MASKED)
    return tab.reshape(NA_WIN_H, NA_HEADS * GRID_W, NA_WIN_H * GRID_W)


def _head_masks():
    lane = lax.broadcasted_iota(jnp.int32, (1, NA_WIDTH), 1)
    return [(lane // NA_HEAD_DIM == h) for h in range(NA_HEADS)]


def _na_kernel(grid_rows, q_ref, kp_ref, kc_ref, kn_ref, vp_ref, vc_ref, vn_ref, kx_ref, vx_ref, bias_ref, o_ref,
               kwin, vwin):
    j = pl.program_id(1)
    tq = q_ref.shape[0]
    nkeys = NA_WIN_H * GRID_W
    kwin[0:tq] = kp_ref[...]
    kwin[tq:2 * tq] = kc_ref[...]
    kwin[2 * tq:3 * tq] = kn_ref[...]
    vwin[0:tq] = vp_ref[...]
    vwin[tq:2 * tq] = vc_ref[...]
    vwin[2 * tq:3 * tq] = vn_ref[...]
    kx = kx_ref[...]
    vx = vx_ref[...]
    hq = NA_HEADS * GRID_W
    own = (lax.broadcasted_iota(jnp.int32, (hq, NA_WIDTH), 0) // GRID_W
           == lax.broadcasted_iota(jnp.int32, (hq, NA_WIDTH), 1) // NA_HEAD_DIM)
    for g in range(NA_ROWS_PER_STEP // NA_ROW_GROUP):
        rls = [g * NA_ROW_GROUP + t for t in range(NA_ROW_GROUP)]
        scores = []
        for rl in rls:
            r = j * NA_ROWS_PER_STEP + rl
            r0 = jnp.clip(r - NA_WIN_H // 2, 0, grid_rows - NA_WIN_H)
            start = pl.multiple_of((r0 - (j - 1) * NA_ROWS_PER_STEP) * GRID_W, GRID_W)
            q = q_ref[rl * GRID_W:(rl + 1) * GRID_W, :]
            qs = jnp.where(own, jnp.concatenate([q] * NA_HEADS, axis=0), jnp.zeros((hq, NA_WIDTH), q.dtype))
            s_loc = _dot_nt(qs, kwin[pl.ds(start, nkeys), :]) + bias_ref[r - r0]
            scores.append((s_loc, _dot_nt(qs, kx), start))
        probs = []
        for s_loc, s_ctx, start in scores:
            m = jnp.maximum(jnp.max(s_loc, axis=-1, keepdims=True), jnp.max(s_ctx, axis=-1, keepdims=True))
            p_loc = jnp.exp(s_loc - m)
            p_ctx = jnp.exp(s_ctx - m)
            denom = jnp.sum(p_loc, axis=-1, keepdims=True) + jnp.sum(p_ctx, axis=-1, keepdims=True)
            probs.append((p_loc.astype(BF16), p_ctx.astype(BF16), denom, start))
        for rl, (p_loc, p_ctx, denom, start) in zip(rls, probs):
            pv = _dot(p_loc, vwin[pl.ds(start, nkeys), :]) + _dot(p_ctx, vx)
            pv = jnp.where(own, pv / denom, 0.0)
            out = pv[0:GRID_W]
            for h in range(1, NA_HEADS):
                out = out + pv[h * GRID_W:(h + 1) * GRID_W]
            o_ref[rl * GRID_W:(rl + 1) * GRID_W, :] = out.astype(o_ref.dtype)


def na_attention(q, k, v, k_ctx, v_ctx, bias, *, batch):
    n = q.shape[0]
    seq = n // batch
    ctx_len = k_ctx.shape[0] // batch
    grid_rows = seq // GRID_W
    tq = NA_ROWS_PER_STEP * GRID_W
    nblk = seq // tq
    cur = lambda b, j: (b * nblk + j, 0)
    prev = lambda b, j: (b * nblk + jnp.maximum(j - 1, 0), 0)
    nxt = lambda b, j: (b * nblk + jnp.minimum(j + 1, nblk - 1), 0)
    blk = lambda f: pl.BlockSpec((tq, NA_WIDTH), f)
    ctx = pl.BlockSpec((ctx_len, NA_WIDTH), lambda b, j: (b, 0))
    return pl.pallas_call(
        functools.partial(_na_kernel, grid_rows),
        out_shape=jax.ShapeDtypeStruct((n, NA_WIDTH), BF16),
        grid=(batch, nblk),
        in_specs=[blk(cur), blk(prev), blk(cur), blk(nxt), blk(prev), blk(cur), blk(nxt), ctx, ctx,
                  pl.BlockSpec(bias.shape, lambda b, j: (0, 0, 0))],
        out_specs=blk(cur),
        scratch_shapes=[pltpu.VMEM((3 * tq, NA_WIDTH), BF16), pltpu.VMEM((3 * tq, NA_WIDTH), BF16)],
        compiler_params=_cparams("parallel", "parallel"),
        name="na_attention",
    )(q, k, k, k, v, v, v, k_ctx, v_ctx, bias)


def _ctx_attn_kernel(q_ref, k_ref, v_ref, o_ref):
    q = q_ref[...]
    k = k_ref[...]
    v = v_ref[...]
    masks = _head_masks()
    acc = jnp.zeros(q.shape, F32)
    for h in range(NA_HEADS):
        s = _dot_nt(jnp.where(masks[h], q, jnp.zeros_like(q)), k)
        p = jnp.exp(s - jnp.max(s, axis=-1, keepdims=True))
        pv = _dot(p.astype(BF16), v) / jnp.sum(p, axis=-1, keepdims=True)
        acc = acc + jnp.where(masks[h], pv, 0.0)
    o_ref[...] = acc.astype(o_ref.dtype)


def ctx_attention(q, k, v, *, batch):
    n = q.shape[0]
    blk = pl.BlockSpec((n // batch, NA_WIDTH), lambda b: (b, 0))
    return pl.pallas_call(
        _ctx_attn_kernel,
        out_shape=jax.ShapeDtypeStruct((n, NA_WIDTH), BF16),
        grid=(batch,),
        in_specs=[blk, blk, blk],
        out_specs=blk,
        compiler_params=_cparams("parallel"),
        name="ctx_attention",
    )(q, k, v)


FT_N1 = 128
FT_T2_BLOCK = 8
FT_K1_BLOCK = 8


def _dft_cos_sin(n, scale=1.0):
    ang = 2.0 * np.pi * np.outer(np.arange(n), np.arange(n)) / n
    return np.cos(ang) * scale, np.sin(ang) * scale


def _channel_dft():
    c, s = _dft_cos_sin(FT_GROUP_DIM)
    eye = np.eye(FT_GROUPS)
    return np.concatenate([np.kron(eye, c), -np.kron(eye, s)], axis=1)


def _ft_stage1_kernel(u_ref, cs_ref, ff_ref, yr_ref, yi_ref):
    n1 = u_ref.shape[1]
    for s in range(u_ref.shape[2] // FT_WIDTH):
        lanes = slice(s * FT_WIDTH, (s + 1) * FT_WIDTH)
        z = _dot(u_ref[0, :, lanes], cs_ref[...])
        zz = jnp.concatenate([z[:, 0:FT_WIDTH], z[:, FT_WIDTH:]], axis=0).astype(BF16)
        y = _dot(ff_ref[...], zz)
        yr_ref[0, :, s, :] = y[0:n1]
        yi_ref[0, :, s, :] = y[n1:]


def _ft_stage2_kernel(yr_ref, yi_ref, twr_ref, twi_ref, g_ref, w_ref, o_ref):
    for i in range(yr_ref.shape[1]):
        yr = yr_ref[0, i]
        yi = yi_ref[0, i]
        tr = twr_ref[0, :, i:i + 1]
        ti = twi_ref[0, :, i:i + 1]
        yy = jnp.concatenate([yr * tr - yi * ti, yr * ti + yi * tr], axis=0).astype(BF16)
        xr = _dot(g_ref[...], yy)
        o_ref[0, :, i, :] = _dot(xr.astype(BF16), w_ref[...]).astype(o_ref.dtype)


def fourier_mix(u, fno_w, *, batch):
    n = u.shape[0]
    seq = n // batch
    n1 = FT_N1
    n2 = seq // n1
    tb = min(FT_T2_BLOCK, n2)
    norm = 1.0 / math.sqrt(seq * FT_GROUP_DIM)
    c1, s1 = _dft_cos_sin(n1)
    ff = jnp.asarray(np.block([[c1, s1], [-s1, c1]]), BF16)
    yr, yi = pl.pallas_call(
        _ft_stage1_kernel,
        out_shape=[jax.ShapeDtypeStruct((batch, n1, n2, FT_WIDTH), F32)] * 2,
        grid=(batch, n2 // tb),
        in_specs=[pl.BlockSpec((1, n1, tb * FT_WIDTH), lambda b, j: (b, 0, j)),
                  pl.BlockSpec((FT_WIDTH, 2 * FT_WIDTH), lambda b, j: (0, 0)),
                  pl.BlockSpec((2 * n1, 2 * n1), lambda b, j: (0, 0))],
        out_specs=[pl.BlockSpec((1, n1, tb, FT_WIDTH), lambda b, j: (b, 0, j, 0))] * 2,
        compiler_params=_cparams("parallel", "parallel"),
        name="fourier_stage1",
    )(u.reshape(batch, n1, n2 * FT_WIDTH), jnp.asarray(_channel_dft(), BF16), ff)
    kb = FT_K1_BLOCK
    ang = 2.0 * np.pi * np.outer(np.arange(n1), np.arange(n2)) / seq
    tw = lambda f: jnp.asarray(f(ang).reshape(n1 // kb, kb, n2).transpose(0, 2, 1), F32)
    c2, s2 = _dft_cos_sin(n2, norm)
    g = jnp.asarray(np.concatenate([c2, s2], axis=1), BF16)
    out = pl.pallas_call(
        _ft_stage2_kernel,
        out_shape=jax.ShapeDtypeStruct((batch, n2, n1, FT_WIDTH), BF16),
        grid=(batch, n1 // kb),
        in_specs=[pl.BlockSpec((1, kb, n2, FT_WIDTH), lambda b, j: (b, j, 0, 0))] * 2 + [
            pl.BlockSpec((1, n2, kb), lambda b, j: (j, 0, 0))] * 2 + [
            pl.BlockSpec((n2, 2 * n2), lambda b, j: (0, 0)), pl.BlockSpec((FT_WIDTH, FT_WIDTH), lambda b, j: (0, 0))],
        out_specs=pl.BlockSpec((1, n2, kb, FT_WIDTH), lambda b, j: (b, 0, j, 0)),
        compiler_params=_cparams("parallel", "parallel"),
        name="fourier_stage2",
    )(yr, yi,
      tw(np.cos), tw(lambda a: -np.sin(a)), g, fno_w.astype(BF16))
    return out.reshape(n, FT_WIDTH)


def _ft_direct_kernel(u_ref, cs_ref, g_ref, w_ref, o_ref):
    z = _dot(u_ref[...], cs_ref[...])
    zz = jnp.concatenate([z[:, 0:FT_WIDTH], z[:, FT_WIDTH:]], axis=0).astype(BF16)
    xr = _dot(g_ref[...], zz)
    o_ref[...] = _dot(xr.astype(BF16), w_ref[...]).astype(o_ref.dtype)


def fourier_mix_direct(u, fno_w, *, batch):
    n = u.shape[0]
    seq = n // batch
    c, s = _dft_cos_sin(seq, 1.0 / math.sqrt(seq * FT_GROUP_DIM))
    g = jnp.asarray(np.concatenate([c, s], axis=1), BF16)
    return pl.pallas_call(
        _ft_direct_kernel,
        out_shape=jax.ShapeDtypeStruct((n, FT_WIDTH), BF16),
        grid=(batch,),
        in_specs=[pl.BlockSpec((seq, FT_WIDTH), lambda b: (b, 0)),
                  pl.BlockSpec((FT_WIDTH, 2 * FT_WIDTH), lambda b: (0, 0)),
                  pl.BlockSpec((seq, 2 * seq), lambda b: (0, 0)),
                  pl.BlockSpec((FT_WIDTH, FT_WIDTH), lambda b: (0, 0))],
        out_specs=pl.BlockSpec((seq, FT_WIDTH), lambda b: (b, 0)),
        compiler_params=_cparams("parallel"),
        name="fourier_direct",
    )(u, jnp.asarray(_channel_dft(), BF16), g, fno_w.astype(BF16))


def _mod_row(mod_ref, rows_per_mod, fixed_row, tm):
    row = fixed_row if fixed_row is not None else (pl.program_id(0) * tm) // rows_per_mod
    return mod_ref[pl.ds(row, 1), :]


def _mix_out_kernel(rows_per_mod, fixed_row, route, o_ref, z_ref, ob_ref, oc_ref, x_ref, mod_ref, nw_ref, ones_ref,
                    w_ref, g_ref, *rest):
    if route:
        wr_ref, x_out, h_out, e_out, p_out = rest
    else:
        x_out, h_out = rest
    tm = x_ref.shape[0]
    m = _mod_row(mod_ref, rows_per_mod, fixed_row, tm)
    for rows in _row_parts(tm):
        o = o_ref[0, rows, :].astype(F32) + o_ref[1, rows, :].astype(F32)
        oa = _head_rms(o, ones_ref[...], DN_HEAD_DIM, nw_ref[...]) * _silu(z_ref[rows, :].astype(F32))
        y = (_dot(oa.astype(BF16), w_ref[0:DN_WIDTH, :])
             + _dot(ob_ref[rows, :], w_ref[DN_WIDTH:DN_WIDTH + NA_WIDTH, :])
             + _dot(oc_ref[rows, :], w_ref[DN_WIDTH + NA_WIDTH:, :]))
        x = x_ref[rows, :] + m[:, 2 * D_MODEL:3 * D_MODEL] * y
        x_out[rows, :] = x
        h = _modulated_norm(x, g_ref[...], m[:, 3 * D_MODEL:4 * D_MODEL], m[:, 4 * D_MODEL:5 * D_MODEL])
        h_out[rows, :] = h.astype(h_out.dtype)
        if route:
            h_hi = h.astype(BF16)
            h_lo = (h - h_hi.astype(F32)).astype(BF16)
            both = _dot_nt(wr_ref[...], h_hi) + _dot_nt(wr_ref[...], h_lo)
            logits = both[0:N_EXPERTS] + both[N_EXPERTS:]
            idx = lax.broadcasted_iota(jnp.int32, logits.shape, 0)
            m1 = jnp.max(logits, axis=0, keepdims=True)
            e1 = jnp.min(jnp.where(logits == m1, idx, N_EXPERTS), axis=0, keepdims=True)
            rest_l = jnp.where(idx == e1, -jnp.inf, logits)
            m2 = jnp.max(rest_l, axis=0, keepdims=True)
            e2 = jnp.min(jnp.where(rest_l == m2, idx, N_EXPERTS), axis=0, keepdims=True)
            t = jnp.exp(m2 - m1)
            e_out[:, rows] = jnp.concatenate([e1, e2], axis=0)
            p_out[:, rows] = jnp.concatenate([1.0 / (1.0 + t), t / (1.0 + t)], axis=0)


def mix_out(o_dirs, z, ob, oc, x2d, mod, dn_norm_w, w_out, g_ffn, w_router=None, *, rows_per_mod, fixed_row=None,
            tm=512):
    n, d = x2d.shape
    tm = min(tm, n)
    route = w_router is not None
    full = lambda shape: pl.BlockSpec(shape, lambda i: (0,) * len(shape))
    rowblk = lambda w: pl.BlockSpec((tm, w), lambda i: (i, 0))
    ins = [o_dirs, z, ob, oc, x2d, mod, jnp.tile(dn_norm_w, DN_HEADS).reshape(1, DN_WIDTH),
           _group_ones(DN_WIDTH, DN_HEAD_DIM), w_out, g_ffn.reshape(1, d)]
    in_specs = [pl.BlockSpec((DN_DIRS, tm, DN_WIDTH), lambda i: (0, i, 0)), rowblk(DN_WIDTH), rowblk(NA_WIDTH),
                rowblk(FT_WIDTH), rowblk(d), full(mod.shape), full((1, DN_WIDTH)), full((DN_WIDTH, DN_WIDTH)),
                full(w_out.shape), full((1, d))]
    outs = [jax.ShapeDtypeStruct((n, d), F32), jax.ShapeDtypeStruct((n, d), F32 if route else BF16)]
    out_specs = [rowblk(d), rowblk(d)]
    if route:
        w_hi = w_router.astype(BF16)
        w_lo = (w_router - w_hi.astype(F32)).astype(BF16)
        ins.append(jnp.concatenate([w_hi, w_lo], axis=1).T)
        in_specs.append(full((2 * N_EXPERTS, d)))
        outs += [jax.ShapeDtypeStruct((TOP_K, n), jnp.int32), jax.ShapeDtypeStruct((TOP_K, n), F32)]
        out_specs += [pl.BlockSpec((TOP_K, tm), lambda i: (0, i))] * 2
    res = pl.pallas_call(
        functools.partial(_mix_out_kernel, rows_per_mod, fixed_row, route),
        out_shape=outs,
        grid=(n // tm,),
        in_specs=in_specs,
        out_specs=out_specs,
        compiler_params=_cparams("parallel"),
        name="mix_out_route" if route else "mix_out",
    )(*ins)
    if route:
        x_new, h_new, top_e, top_p = res
        return x_new, h_new, top_e.T, top_p.T
    return res


def _ffn_kernel(rows_per_mod, fixed_row, tf, h_ref, x_ref, mod_ref, w1_ref, w3_ref, w2_ref, o_ref):
    h = h_ref[...]
    gated = [(_silu(_dot(h, w1_ref[:, c:c + tf])) * _dot(h, w3_ref[:, c:c + tf])).astype(BF16)
             for c in range(0, w1_ref.shape[1], tf)]
    y = _dot(jnp.concatenate(gated, axis=1), w2_ref[...])
    m = _mod_row(mod_ref, rows_per_mod, fixed_row, h_ref.shape[0])
    o_ref[...] = x_ref[...] + m[:, 5 * D_MODEL:6 * D_MODEL] * y


def dense_ffn(h, x2d, mod, w1, w3, w2, *, rows_per_mod, fixed_row=None, tm=512, tf=1408):
    n, d = x2d.shape
    tm = min(tm, n)
    dff = w1.shape[1]
    resident = lambda shape: pl.BlockSpec(shape, lambda i: (0, 0), pipeline_mode=pl.Buffered(1))
    return pl.pallas_call(
        functools.partial(_ffn_kernel, rows_per_mod, fixed_row, tf),
        out_shape=jax.ShapeDtypeStruct((n, d), F32),
        grid=(n // tm,),
        in_specs=[pl.BlockSpec((tm, d), lambda i: (i, 0)), pl.BlockSpec((tm, d), lambda i: (i, 0)),
                  pl.BlockSpec(mod.shape, lambda i: (0, 0)),
                  resident((d, dff)), resident((d, dff)), resident((dff, d))],
        out_specs=pl.BlockSpec((tm, d), lambda i: (i, 0)),
        compiler_params=_cparams("parallel"),
        name="dense_ffn",
    )(h, x2d, mod, w1, w3, w2)


MOE_ROWS = 512
MOE_DMA_ROWS = 256
MOE_VMEM_LIMIT = 56 * 1024 * 1024


def _row_copy(src, src_row, dst, dst_row, sem):
    return pltpu.make_async_copy(src.at[pl.ds(src_row, 1)], dst.at[pl.ds(dst_row, 1)], sem)


def _moe_scatter_kernel(dest_ref, h_ref, xb_in, xb_out, sem):
    del xb_in
    tm = h_ref.shape[0]

    def start(t, carry):
        for j in range(TOP_K):
            _row_copy(h_ref, t, xb_out, dest_ref[0, 0, TOP_K * t + j], sem).start()
        return carry

    def wait(t, carry):
        for j in range(TOP_K):
            _row_copy(h_ref, 0, xb_out, 0, sem).wait()
        return carry

    lax.fori_loop(0, tm, start, 0, unroll=8)
    lax.fori_loop(0, tm, wait, 0, unroll=8)


def _moe_ffn_kernel(tf, be_ref, na_ref, x_ref, w1_ref, w3_ref, w2_ref, y_ref):
    active = pl.program_id(0) < na_ref[0]

    @pl.when(active)
    def _():
        h = x_ref[...].astype(BF16)
        gated = [(_silu(_dot(h, w1_ref[0, :, c:c + tf])) * _dot(h, w3_ref[0, :, c:c + tf])).astype(BF16)
                 for c in range(0, w1_ref.shape[2], tf)]
        y_ref[...] = _dot(jnp.concatenate(gated, axis=1), w2_ref[0])

    @pl.when(jnp.logical_not(active))
    def _():
        y_ref[...] = jnp.zeros_like(y_ref)


def _moe_combine_kernel(rows_per_mod, dest_ref, p_ref, x_ref, mod_ref, yb_hbm, o_ref, buf, sem):
    tm = x_ref.shape[0]

    def start(t, carry):
        for j in range(TOP_K):
            _row_copy(yb_hbm, dest_ref[0, 0, TOP_K * t + j], buf.at[j], t, sem).start()
        return carry

    def wait(t, carry):
        for j in range(TOP_K):
            _row_copy(yb_hbm, 0, buf.at[j], 0, sem).wait()
        return carry

    lax.fori_loop(0, tm, start, 0, unroll=8)
    lax.fori_loop(0, tm, wait, 0, unroll=8)
    p = p_ref[...]
    y = p[:, 0:1] * buf[0] + p[:, 1:2] * buf[1]
    m = _mod_row(mod_ref, rows_per_mod, None, tm)
    o_ref[...] = x_ref[...] + m[:, 5 * D_MODEL:6 * D_MODEL] * y


def moe_ffn(h, x2d, mod, top_e, top_p, w1, w3, w2, *, rows_per_mod, tf=1792):
    n, d = x2d.shape
    dff = w1.shape[2]
    n_assign = n * TOP_K
    n_blocks = -(-n_assign // MOE_ROWS) + N_EXPERTS
    n_slots = n_blocks * MOE_ROWS
    e_flat = top_e.reshape(n_assign)
    onehot = (e_flat[:, None] == jnp.arange(N_EXPERTS, dtype=jnp.int32)[None, :]).astype(jnp.int32)
    csum = jnp.cumsum(onehot, axis=0)
    rank = jnp.sum(onehot * csum, axis=1) - 1
    counts = csum[-1]
    padded = (counts + MOE_ROWS - 1) // MOE_ROWS * MOE_ROWS
    pad_end = jnp.cumsum(padded)
    dest = (jnp.sum(onehot * (pad_end - padded)[None, :], axis=1) + rank).astype(jnp.int32)
    n_active = (pad_end[-1] // MOE_ROWS).astype(jnp.int32).reshape(1)
    blk_ids = jnp.minimum(jnp.arange(n_blocks, dtype=jnp.int32), n_active[0] - 1)
    block_e = jnp.minimum(jnp.searchsorted(pad_end, blk_ids * MOE_ROWS, side='right'), N_EXPERTS - 1).astype(jnp.int32)

    tmd = min(MOE_DMA_ROWS, n)
    dest2d = dest.reshape(n // tmd, 1, TOP_K * tmd)
    dest_spec = pl.BlockSpec((1, 1, TOP_K * tmd), lambda i: (i, 0, 0), memory_space=pltpu.SMEM)
    hbm = pl.BlockSpec(memory_space=pl.ANY)
    xb = pl.pallas_call(
        _moe_scatter_kernel,
        out_shape=jax.ShapeDtypeStruct((n_slots, d), F32),
        grid=(n // tmd,),
        in_specs=[dest_spec, pl.BlockSpec((tmd, d), lambda i: (i, 0)), hbm],
        out_specs=hbm,
        scratch_shapes=[pltpu.SemaphoreType.DMA],
        input_output_aliases={2: 0},
        compiler_params=_cparams("arbitrary"),
        name="moe_scatter",
    )(dest2d, h, jnp.zeros((n_slots, d), F32))

    expert_w = lambda shape: pl.BlockSpec(shape, lambda i, be, na: (be[i], 0, 0), pipeline_mode=pl.Buffered(1))
    yb = pl.pallas_call(
        functools.partial(_moe_ffn_kernel, tf),
        out_shape=jax.ShapeDtypeStruct((n_slots, d), F32),
        grid_spec=pltpu.PrefetchScalarGridSpec(
            num_scalar_prefetch=2,
            grid=(n_blocks,),
            in_specs=[pl.BlockSpec((MOE_ROWS, d), lambda i, be, na: (i, 0)),
                      expert_w((1, d, dff)), expert_w((1, d, dff)), expert_w((1, dff, d))],
            out_specs=pl.BlockSpec((MOE_ROWS, d), lambda i, be, na: (i, 0))),
        compiler_params=pltpu.CompilerParams(dimension_semantics=("arbitrary",), vmem_limit_bytes=MOE_VMEM_LIMIT),
        name="moe_expert_ffn",
    )(block_e, n_active, xb, w1, w3, w2)

    return pl.pallas_call(
        functools.partial(_moe_combine_kernel, rows_per_mod),
        out_shape=jax.ShapeDtypeStruct((n, d), F32),
        grid=(n // tmd,),
        in_specs=[dest_spec, pl.BlockSpec((tmd, TOP_K), lambda i: (i, 0)), pl.BlockSpec((tmd, d), lambda i: (i, 0)),
                  pl.BlockSpec(mod.shape, lambda i: (0, 0)), hbm],
        out_specs=pl.BlockSpec((tmd, d), lambda i: (i, 0)),
        scratch_shapes=[pltpu.VMEM((TOP_K, tmd, d), F32), pltpu.SemaphoreType.DMA],
        compiler_params=_cparams("arbitrary"),
        name="moe_combine",
    )(dest2d, top_p, x2d, mod, yb)


def _split_w_in(w_in_l):
    g0 = 4 * DN_WIDTH
    g1 = g0 + 2 * N_DH
    w_main = jnp.concatenate([w_in_l[:, :g0], w_in_l[:, g1:]], axis=1).astype(BF16)
    w_ab = w_in_l[:, g0:g1].astype(BF16)
    return w_main, w_ab, w_ab.T


def kernel(x, c, ctx, c_ctx, w_mod, b_mod, g_mix, g_ffn, w_in, dn_conv, dn_a_log, dn_dt_bias, dn_norm_w, na_q_norm,
           na_k_norm, na_rpb, fno_w, w_out, ffn_w1, ffn_w3, ffn_w2, moe_router, moe_w1, moe_w3, moe_w2):
    bsz, seq, d = x.shape
    ctx_len = ctx.shape[1]
    ctx_row = bsz
    cond = jnp.zeros((SUBLANES, d), F32).at[:bsz].set(c).at[ctx_row].set(c_ctx)
    mods = adaln_mod(cond, w_mod, b_mod)
    x2 = x.reshape(bsz * seq, d)
    c2 = ctx.reshape(bsz * ctx_len, d)
    zero_state = jnp.zeros((bsz, DN_DIRS, DN_HEADS, DN_HEAD_DIM, DN_HEAD_DIM), F32)

    for layer in range(DEPTH):
        need_ctx = layer < DEPTH - 1
        mod = mods[layer]
        w_main, w_ab, w_abt = _split_w_in(w_in[layer])
        proj = functools.partial(in_proj, mod=mod, g=g_mix[layer], w_main=w_main, w_ab=w_ab, w_abt=w_abt,
                                 q_norm=na_q_norm[layer], k_norm=na_k_norm[layer])
        qkv, z, ab, abt, nq, nk, nv, ft = proj(x2, rows_per_mod=seq)
        qkv_c, z_c, ab_c, abt_c, nq_c, nk_c, nv_c, ft_c = proj(c2, rows_per_mod=ctx_len, fixed_row=ctx_row)

        prep = functools.partial(dn_prep, conv_w=dn_conv[layer], a_log=dn_a_log[layer], dt_bias=dn_dt_bias[layer],
                                 batch=bsz)
        q_c, k_c, v_c, gcc_c, beta_c, gcr_c = prep(qkv_c, ab_c, abt_c)
        o_c, s_ctx = dn_scan(q_c, k_c, v_c, gcc_c, beta_c, gcr_c, zero_state, batch=bsz)
        q_l, k_l, v_l, gcc_l, beta_l, gcr_l = prep(qkv, ab, abt)
        o_l, _ = dn_scan(q_l, k_l, v_l, gcc_l, beta_l, gcr_l, s_ctx, batch=bsz)

        ob = na_attention(nq, nk, nv, nk_c, nv_c, na_bias_table(na_rpb[layer]), batch=bsz)
        oc = fourier_mix(ft, fno_w[layer], batch=bsz)

        w_out_l = w_out[layer].astype(BF16)
        j = layer // 2
        if layer % 2 == 0:
            x2, h2 = mix_out(o_l, z, ob, oc, x2, mod, dn_norm_w[layer], w_out_l, g_ffn[layer], rows_per_mod=seq)
            ffn_w = (ffn_w1[j].astype(BF16), ffn_w3[j].astype(BF16), ffn_w2[j].astype(BF16))
            x2 = dense_ffn(h2, x2, mod, *ffn_w, rows_per_mod=seq)
        else:
            x2, h2, top_e, top_p = mix_out(o_l, z, ob, oc, x2, mod, dn_norm_w[layer], w_out_l, g_ffn[layer],
                                           moe_router[j], rows_per_mod=seq)
            x2 = moe_ffn(h2, x2, mod, top_e, top_p, moe_w1[j].astype(BF16), moe_w3[j].astype(BF16),
                         moe_w2[j].astype(BF16), rows_per_mod=seq)
        if need_ctx:
            ob_c = ctx_attention(nq_c, nk_c, nv_c, batch=bsz)
            oc_c = fourier_mix_direct(ft_c, fno_w[layer], batch=bsz)
            assert layer % 2 == 0, "context tokens only pass through dense layers at this depth"
            c2, hc2 = mix_out(o_c, z_c, ob_c, oc_c, c2, mod, dn_norm_w[layer], w_out_l, g_ffn[layer],
                              rows_per_mod=ctx_len, fixed_row=ctx_row)
            c2 = dense_ffn(hc2, c2, mod, *ffn_w, rows_per_mod=ctx_len, fixed_row=ctx_row)
    return x2.reshape(bsz, seq, d)
```

```python
import functools
import math

import jax
import jax.numpy as jnp
import numpy as np
from jax import lax
from jax.experimental import pallas as pl
from jax.experimental.pallas import tpu as pltpu

D_MODEL = 1024
DEPTH = 2
GRID_W = 64
DN_HEADS = 4
DN_HEAD_DIM = 128
DN_WIDTH = DN_HEADS * DN_HEAD_DIM
DN_CONV = 3
DN_CHUNK = 64
DN_DIRS = 2
NA_HEADS = 4
NA_HEAD_DIM = 64
NA_WIDTH = NA_HEADS * NA_HEAD_DIM
NA_WIN_H = 8
NA_WIN_W = 16
FT_GROUPS = 4
FT_GROUP_DIM = 64
FT_WIDTH = FT_GROUPS * FT_GROUP_DIM
N_EXPERTS = 8
TOP_K = 2
N_MOD = 6
EPS = 1e-6

SUBLANES = 8
LANES = 128
VMEM_LIMIT = 48 * 1024 * 1024

BF16 = jnp.bfloat16
F32 = jnp.float32
N_DH = DN_DIRS * DN_HEADS


def _cparams(*sem):
    return pltpu.CompilerParams(dimension_semantics=sem, vmem_limit_bytes=VMEM_LIMIT)


def _silu(x):
    return x * jax.nn.sigmoid(x)


def _dot(a, b):
    return jnp.dot(a, b, preferred_element_type=F32)


def _dot_nt(a, b):
    return lax.dot_general(a, b, (((1,), (1,)), ((), ())), preferred_element_type=F32)


def _dot_tn(a, b):
    return lax.dot_general(a, b, (((0,), (0,)), ((), ())), preferred_element_type=F32)


def _group_ones(width, group):
    idx = np.arange(width) // group
    return jnp.asarray((idx[:, None] == idx[None, :]).astype(np.float32), BF16)


def _mod_kernel(c_ref, w_ref, b_ref, o_ref):
    s = _silu(c_ref[...])
    o_ref[0] = jnp.dot(s, w_ref[0], preferred_element_type=F32, precision=lax.Precision.HIGHEST) + b_ref[0]


def adaln_mod(cond_rows, w_mod, b_mod, *, tn=1536):
    depth, d, n = w_mod.shape
    rows = cond_rows.shape[0]
    return pl.pallas_call(
        _mod_kernel,
        out_shape=jax.ShapeDtypeStruct((depth, rows, n), F32),
        grid=(depth, n // tn),
        in_specs=[pl.BlockSpec((rows, d), lambda l, j: (0, 0)),
                  pl.BlockSpec((1, d, tn), lambda l, j: (l, 0, j)),
                  pl.BlockSpec((1, 1, tn), lambda l, j: (l, 0, j))],
        out_specs=pl.BlockSpec((1, rows, tn), lambda l, j: (l, 0, j)),
        compiler_params=_cparams("arbitrary", "arbitrary"),
        name="adaln_mod",
    )(cond_rows, w_mod, b_mod.reshape(depth, 1, n))


C_QKV = 0
C_Z = 3 * DN_WIDTH
C_NQ = C_Z + DN_WIDTH
C_NK = C_NQ + NA_WIDTH
C_NV = C_NK + NA_WIDTH
C_FT = C_NV + NA_WIDTH
C_END = C_FT + FT_WIDTH


def _modulated_norm(x, g, shift, scale):
    y = x * lax.rsqrt(jnp.mean(x * x, axis=-1, keepdims=True) + EPS)
    return y * g * (1.0 + scale) + shift


ROW_PART = 128


def _row_parts(tm):
    part = min(ROW_PART, tm)
    return [slice(r, r + part) for r in range(0, tm, part)]


def _head_rms(x, ones, width, gain):
    ss = _dot((x * x).astype(BF16), ones)
    return x * lax.rsqrt(ss * (1.0 / width) + EPS) * gain


def _in_proj_kernel(rows_per_mod, fixed_row, x_ref, mod_ref, g_ref, w_ref, wab_ref, wabt_ref, qn_ref, kn_ref,
                    ones_ref, qkv_ref, z_ref, ab_ref, abt_ref, nq_ref, nk_ref, nv_ref, ft_ref):
    tm = x_ref.shape[0]
    row = fixed_row if fixed_row is not None else (pl.program_id(0) * tm) // rows_per_mod
    m = mod_ref[pl.ds(row, 1), :]
    shift, scale = m[:, 0:D_MODEL], m[:, D_MODEL:2 * D_MODEL]
    ones = ones_ref[...]
    for rows in _row_parts(tm):
        h = _modulated_norm(x_ref[rows, :], g_ref[...], shift, scale).astype(BF16)
        qkv_ref[rows, :] = _dot(h, w_ref[:, C_QKV:C_Z]).astype(qkv_ref.dtype)
        z_ref[rows, :] = _dot(h, w_ref[:, C_Z:C_NQ]).astype(z_ref.dtype)
        nq = _dot(h, w_ref[:, C_NQ:C_NK])
        nq_ref[rows, :] = (_head_rms(nq, ones, NA_HEAD_DIM, qn_ref[...]) * (NA_HEAD_DIM ** -0.5)).astype(nq_ref.dtype)
        nk = _dot(h, w_ref[:, C_NK:C_NV])
        nk_ref[rows, :] = _head_rms(nk, ones, NA_HEAD_DIM, kn_ref[...]).astype(nk_ref.dtype)
        nv_ref[rows, :] = _dot(h, w_ref[:, C_NV:C_FT]).astype(nv_ref.dtype)
        ft_ref[rows, :] = _dot(h, w_ref[:, C_FT:C_END]).astype(ft_ref.dtype)
        ab_ref[rows, :] = _dot(h, wab_ref[...])
        abt_ref[:, rows] = _dot_nt(wabt_ref[...], h)


def in_proj(x2d, mod, g, w_main, w_ab, w_abt, q_norm, k_norm, *, rows_per_mod, fixed_row=None, tm=512):
    n, d = x2d.shape
    tm = min(tm, n)
    nab = w_ab.shape[1]
    full = lambda shape: pl.BlockSpec(shape, lambda i: (0,) * len(shape))
    rowblk = lambda w: pl.BlockSpec((tm, w), lambda i: (i, 0))
    outs = [jax.ShapeDtypeStruct((n, 3 * DN_WIDTH), BF16), jax.ShapeDtypeStruct((n, DN_WIDTH), BF16),
            jax.ShapeDtypeStruct((n, nab), F32), jax.ShapeDtypeStruct((nab, n), F32),
            jax.ShapeDtypeStruct((n, NA_WIDTH), BF16), jax.ShapeDtypeStruct((n, NA_WIDTH), BF16),
            jax.ShapeDtypeStruct((n, NA_WIDTH), BF16), jax.ShapeDtypeStruct((n, FT_WIDTH), BF16)]
    return pl.pallas_call(
        functools.partial(_in_proj_kernel, rows_per_mod, fixed_row),
        out_shape=outs,
        grid=(n // tm,),
        in_specs=[rowblk(d), full(mod.shape), full((1, d)), full(w_main.shape), full(w_ab.shape), full(w_abt.shape),
                  full((1, NA_WIDTH)), full((1, NA_WIDTH)), full((NA_WIDTH, NA_WIDTH))],
        out_specs=[rowblk(3 * DN_WIDTH), rowblk(DN_WIDTH), rowblk(nab), pl.BlockSpec((nab, tm), lambda i: (0, i)),
                   rowblk(NA_WIDTH), rowblk(NA_WIDTH), rowblk(NA_WIDTH), rowblk(FT_WIDTH)],
        compiler_params=_cparams("parallel"),
        name="in_proj",
    )(x2d, mod, g.reshape(1, d), w_main, w_ab, w_abt,
      jnp.tile(q_norm, NA_HEADS).reshape(1, NA_WIDTH), jnp.tile(k_norm, NA_HEADS).reshape(1, NA_WIDTH),
      _group_ones(NA_WIDTH, NA_HEAD_DIM))


QKV_HALO = 16


def _softplus(x):
    return jnp.maximum(x, 0.0) + jnp.log1p(jnp.exp(-jnp.abs(x)))


def _tri(n, lower):
    r = lax.broadcasted_iota(jnp.int32, (n, n), 0)
    c = lax.broadcasted_iota(jnp.int32, (n, n), 1)
    return jnp.where((r >= c) if lower else (r <= c), 1.0, 0.0).astype(F32)


def _dot_hi(a, b):
    return jnp.dot(a, b, preferred_element_type=F32, precision=lax.Precision.HIGHEST)


def _dn_prep_kernel(x_ref, xp_ref, xn_ref, ab_ref, abt_ref, cw_ref, alr_ref, dtr_ref, alc_ref, dtc_ref, ones_ref,
                    q_ref, k_ref, v_ref, gcc_ref, beta_ref, gcr_ref):
    i = pl.program_id(1)
    tl = x_ref.shape[0]
    x = x_ref[...].astype(F32)
    prev = jnp.where(i > 0, xp_ref[QKV_HALO - 1:QKV_HALO, :].astype(F32), 0.0)
    nxt = jnp.where(i < pl.num_programs(1) - 1, xn_ref[0:1, :].astype(F32), 0.0)
    rows = lax.broadcasted_iota(jnp.int32, x.shape, 0)
    xm1 = jnp.where(rows == 0, prev, pltpu.roll(x, 1, axis=0))
    xp1 = jnp.where(rows == tl - 1, nxt, pltpu.roll(x, tl - 1, axis=0))
    y = _silu(xm1 * cw_ref[0:1, :] + x * cw_ref[1:2, :] + xp1 * cw_ref[2:3, :])
    ones = ones_ref[...]
    q = y[:, 0:DN_WIDTH]
    k = y[:, DN_WIDTH:2 * DN_WIDTH]
    q_ref[...] = (q * lax.rsqrt(_dot((q * q).astype(BF16), ones) + EPS) * (DN_HEAD_DIM ** -0.5)).astype(q_ref.dtype)
    k_ref[...] = (k * lax.rsqrt(_dot((k * k).astype(BF16), ones) + EPS)).astype(k_ref.dtype)
    v_ref[...] = y[:, 2 * DN_WIDTH:].astype(v_ref.dtype)

    ab = ab_ref[...]
    g_col = -jnp.exp(alr_ref[...]) * _softplus(ab[:, 0:N_DH] + dtr_ref[...])
    beta = jax.nn.sigmoid(ab[:, N_DH:2 * N_DH])
    beta_ref[0] = beta[:, 0:DN_HEADS]
    beta_ref[1] = beta[:, DN_HEADS:]
    g_row = -jnp.exp(alc_ref[...]) * _softplus(abt_ref[0:N_DH, :] + dtc_ref[...])
    lo, up = _tri(DN_CHUNK, True), _tri(DN_CHUNK, False)
    sub = lax.broadcasted_iota(jnp.int32, (N_DH, DN_CHUNK), 0)
    for c in range(tl // DN_CHUNK):
        sl = slice(c * DN_CHUNK, (c + 1) * DN_CHUNK)
        gc = g_col[sl, :]
        gcc_ref[0, sl, :] = _dot_hi(lo, gc)[:, 0:DN_HEADS]
        gcc_ref[1, sl, :] = _dot_hi(up, gc)[:, DN_HEADS:]
        gr = g_row[:, sl]
        cs = jnp.where(sub < DN_HEADS, _dot_hi(gr, up), _dot_hi(gr, lo))
        gcr_ref[0, 0, c] = cs[0:DN_HEADS]
        gcr_ref[1, 0, c] = cs[DN_HEADS:]


def dn_prep(qkv, ab, abt, conv_w, a_log, dt_bias, *, batch, tl=512):
    n = qkv.shape[0]
    seq = n // batch
    tl = min(tl, seq)
    nblk = seq // tl
    hb = tl // QKV_HALO
    nc_all = seq // DN_CHUNK
    full = lambda shape: pl.BlockSpec(shape, lambda b, i: (0,) * len(shape))
    rowblk = lambda w: pl.BlockSpec((tl, w), lambda b, i: (b * nblk + i, 0))
    outs = [jax.ShapeDtypeStruct((n, DN_WIDTH), BF16)] * 3 + [
        jax.ShapeDtypeStruct((DN_DIRS, n, DN_HEADS), F32), jax.ShapeDtypeStruct((DN_DIRS, n, DN_HEADS), F32),
        jax.ShapeDtypeStruct((DN_DIRS, batch, nc_all, DN_HEADS, DN_CHUNK), F32)]
    a_log = a_log.reshape(N_DH)
    dt_bias = dt_bias.reshape(N_DH)
    return pl.pallas_call(
        _dn_prep_kernel,
        out_shape=outs,
        grid=(batch, nblk),
        in_specs=[rowblk(3 * DN_WIDTH),
                  pl.BlockSpec((QKV_HALO, 3 * DN_WIDTH), lambda b, i: (jnp.maximum((b * nblk + i) * hb - 1, 0), 0)),
                  pl.BlockSpec((QKV_HALO, 3 * DN_WIDTH),
                               lambda b, i: (jnp.minimum((b * nblk + i + 1) * hb, n // QKV_HALO - 1), 0)),
                  rowblk(2 * N_DH),
                  pl.BlockSpec((2 * N_DH, tl), lambda b, i: (0, b * nblk + i)),
                  full((DN_CONV, 3 * DN_WIDTH)), full((1, N_DH)), full((1, N_DH)), full((N_DH, 1)), full((N_DH, 1)),
                  full((DN_WIDTH, DN_WIDTH))],
        out_specs=[rowblk(DN_WIDTH)] * 3 + [
            pl.BlockSpec((DN_DIRS, tl, DN_HEADS), lambda b, i: (0, b * nblk + i, 0)),
            pl.BlockSpec((DN_DIRS, tl, DN_HEADS), lambda b, i: (0, b * nblk + i, 0)),
            pl.BlockSpec((DN_DIRS, 1, tl // DN_CHUNK, DN_HEADS, DN_CHUNK), lambda b, i: (0, b, i, 0, 0))],
        compiler_params=_cparams("parallel", "parallel"),
        name="dn_prep",
    )(qkv, qkv, qkv, ab, abt, conv_w, a_log.reshape(1, N_DH), dt_bias.reshape(1, N_DH),
      a_log.reshape(N_DH, 1), dt_bias.reshape(N_DH, 1), _group_ones(DN_WIDTH, DN_HEAD_DIM))


DN_LOCAL_GROUP = 4


def _unit_tri_inverses(mats, block):
    n = mats[0].shape[0]
    r = lax.broadcasted_iota(jnp.int32, (n, n), 0)
    c = lax.broadcasted_iota(jnp.int32, (n, n), 1)
    eye = jnp.where(r == c, 1.0, 0.0)
    ds = [eye - jnp.where(r // 2 == c // 2, a, 0.0) for a in mats]
    size = 2
    while size < block:
        off = jnp.logical_and(r // (2 * size) == c // (2 * size), r // size != c // size)
        d16s = [d.astype(BF16) for d in ds]
        ts = [_dot(d16, jnp.where(off, a, 0.0).astype(BF16)) for d16, a in zip(d16s, mats)]
        ds = [d - _dot(t.astype(BF16), d16) for d, t, d16 in zip(ds, ts, d16s)]
        size *= 2
    return ds


def _dn_scan_kernel(q_ref, k_ref, v_ref, gcc_ref, beta_ref, gcr_ref, s0_ref, o_ref, s_ref,
                    rg_scr, n_scr, h_scr, eg_scr):
    d = pl.program_id(0)
    i = pl.program_id(1)
    nb, nc = rg_scr.shape[0], rg_scr.shape[1]
    per_batch = max(DN_LOCAL_GROUP // nb, 1)
    cs = DN_CHUNK
    hd = DN_HEAD_DIM
    fwd = d == 0

    @pl.when(i == 0)
    def _():
        s_ref[...] = s0_ref[...]

    hc = DN_HEADS * cs
    r = lax.broadcasted_iota(jnp.int32, (hc, hc), 0)
    c = lax.broadcasted_iota(jnp.int32, (hc, hc), 1)
    same_head = (r // cs) == (c // cs)
    rel = (r % cs - c % cs) * jnp.where(fwd, 1, -1)
    incl = jnp.logical_and(same_head, rel >= 0)
    strict = jnp.logical_and(same_head, rel > 0)
    last = jnp.where(fwd, cs - 1, 0)

    def load(b, ci):
        row0 = pl.multiple_of(ci * cs, cs)
        stack = lambda ref: jnp.concatenate(
            [ref[b, pl.ds(row0, cs), h * hd:(h + 1) * hd] for h in range(DN_HEADS)], axis=0)
        gate = lambda ref, rows: jnp.concatenate([ref[0, b, rows, h:h + 1] for h in range(DN_HEADS)], axis=0)
        q, k, v = stack(q_ref), stack(k_ref), stack(v_ref).astype(F32)
        gcol = gate(gcc_ref, pl.ds(row0, cs))
        bcol = gate(beta_ref, pl.ds(row0, cs))
        grows = gcr_ref[0, b, ci]
        grow = jnp.concatenate([grows[h:h + 1, :] for h in range(DN_HEADS)], axis=1)
        glast_h = [gcc_ref[0, b, pl.ds(row0 + last, 1), h:h + 1] for h in range(DN_HEADS)]
        glast = jnp.concatenate([jnp.broadcast_to(g, (cs, 1)) for g in glast_h], axis=0)
        decay = jnp.where(incl, jnp.exp(jnp.where(incl, gcol - grow, 0.0)), 0.0)
        kf = k.astype(F32)
        kb = kf * bcol
        eg = jnp.exp(gcol)
        gram = _dot_nt(jnp.concatenate([kb.astype(BF16), q], axis=0), k)
        a = jnp.where(strict, gram[0:hc] * decay, 0.0)
        rhs = jnp.concatenate([v * bcol, kb * eg], axis=1).astype(BF16)
        qk = jnp.where(incl, gram[hc:] * decay, 0.0).astype(BF16)
        kd = (kf * jnp.exp(glast - gcol)).astype(BF16)
        return dict(b=b, ci=ci, a=a, rhs=rhs, qk=qk, kd=kd, qe=q.astype(F32) * eg, glast_h=glast_h)

    def local_group(j, carry):
        chunks = [load(b, per_batch * j + g) for b in range(nb) for g in range(per_batch)]
        ts = _unit_tri_inverses([ch["a"] for ch in chunks], cs)
        uws = [_dot(t.astype(BF16), ch["rhs"]).astype(BF16) for t, ch in zip(ts, chunks)]
        hgs = [_dot(ch["qk"], uw) for ch, uw in zip(chunks, uws)]
        for ch, uw, hg in zip(chunks, uws, hgs):
            b, ci = ch["b"], ch["ci"]
            g_mat = (ch["qe"] - hg[:, hd:]).astype(BF16)
            for h in range(DN_HEADS):
                rows = slice(h * cs, (h + 1) * cs)
                nr = _dot_tn(ch["kd"][rows], uw[rows])
                n_scr[b, ci, h] = nr[:, 0:hd]
                h_scr[b, ci, h] = hg[rows, 0:hd]
                rg_scr[b, ci, h, 0:hd, :] = nr[:, hd:].astype(BF16)
                rg_scr[b, ci, h, hd:hd + cs, :] = g_mat[rows]
                eg_scr[b, ci, h] = jnp.broadcast_to(jnp.exp(ch["glast_h"][h]), (1, hd))
        return carry

    lax.fori_loop(0, nc // per_batch, local_group, 0)

    def step(cc, carry):
        ci = jnp.where(fwd, cc, nc - 1 - cc)
        row0 = pl.multiple_of(ci * cs, cs)
        for b in range(nb):
            for h in range(DN_HEADS):
                s = s_ref[b, 0, h]
                y = _dot(rg_scr[b, ci, h], s.astype(BF16))
                s_ref[b, 0, h] = s * eg_scr[b, ci, h] - y[0:hd] + n_scr[b, ci, h]
                o_ref[0, b, pl.ds(row0, cs), h * hd:(h + 1) * hd] = (
                    y[hd:hd + cs] + h_scr[b, ci, h]).astype(o_ref.dtype)
        return carry

    lax.fori_loop(0, nc, step, 0)


def dn_scan(q, k, v, gcc, beta, gcr, s0, *, batch, tl=512):
    n = q.shape[0]
    seq = n // batch
    tl = min(tl, seq)
    nblk = seq // tl
    nc = tl // DN_CHUNK
    blk = lambda d, i: jnp.where(d == 0, i, nblk - 1 - i)
    rowspec = pl.BlockSpec((batch, tl, DN_WIDTH), lambda d, i: (0, blk(d, i), 0))
    gate = pl.BlockSpec((1, batch, tl, DN_HEADS), lambda d, i: (d, 0, blk(d, i), 0))
    state = pl.BlockSpec((batch, 1, DN_HEADS, DN_HEAD_DIM, DN_HEAD_DIM), lambda d, i: (0, d, 0, 0, 0))
    per_chunk = (batch, nc, DN_HEADS)
    rows3 = lambda t: t.reshape(batch, seq, DN_WIDTH)
    gates4 = lambda t: t.reshape(DN_DIRS, batch, seq, DN_HEADS)
    o, s_fin = pl.pallas_call(
        _dn_scan_kernel,
        out_shape=[jax.ShapeDtypeStruct((DN_DIRS, batch, seq, DN_WIDTH), BF16), jax.ShapeDtypeStruct(s0.shape, F32)],
        grid=(DN_DIRS, nblk),
        in_specs=[rowspec, rowspec, rowspec, gate, gate,
                  pl.BlockSpec((1, batch, nc, DN_HEADS, DN_CHUNK), lambda d, i: (d, 0, blk(d, i), 0, 0)),
                  state],
        out_specs=[pl.BlockSpec((1, batch, tl, DN_WIDTH), lambda d, i: (d, 0, blk(d, i), 0)), state],
        scratch_shapes=[pltpu.VMEM(per_chunk + (DN_HEAD_DIM + DN_CHUNK, DN_HEAD_DIM), BF16),
                        pltpu.VMEM(per_chunk + (DN_HEAD_DIM, DN_HEAD_DIM), F32),
                        pltpu.VMEM(per_chunk + (DN_CHUNK, DN_HEAD_DIM), F32),
                        pltpu.VMEM(per_chunk + (1, DN_HEAD_DIM), F32)],
        compiler_params=_cparams("parallel", "arbitrary"),
        name="dn_scan",
    )(rows3(q), rows3(k), rows3(v), gates4(gcc), gates4(beta), gcr, s0)
    return o.reshape(DN_DIRS, n, DN_WIDTH), s_fin


NA_ROWS_PER_STEP = 8
NA_ROW_GROUP = 4
MASKED = -1e30


def na_bias_table(rpb):
    qc = np.arange(GRID_W)[:, None]
    kc = np.arange(GRID_W)[None, :]
    c0 = np.clip(qc - NA_WIN_W // 2, 0, GRID_W - NA_WIN_W)
    valid = (kc >= c0) & (kc < c0 + NA_WIN_W)
    rel_c = kc - qc + NA_WIN_W - 1
    case = np.arange(NA_WIN_H)[:, None]
    i = np.arange(NA_WIN_H)[None, :]
    rel_r = i + NA_WIN_H - 1 - case
    pick_r = (rel_r[:, :, None] == np.arange(2 * NA_WIN_H - 1)).astype(np.float32)
    pick_c = ((rel_c[:, :, None] == np.arange(2 * NA_WIN_W - 1)) & valid[:, :, None]).astype(np.float32)
    tab = jnp.einsum('hrc,xir,qkc->xhqik', rpb, pick_r, pick_c, precision=lax.Precision.HIGHEST)
    tab = jnp.where(valid[None, None, :, None, :], tab, MASKED)
    return tab.reshape(NA_WIN_H, NA_HEADS * GRID_W, NA_WIN_H * GRID_W)


def _head_masks():
    lane = lax.broadcasted_iota(jnp.int32, (1, NA_WIDTH), 1)
    return [(lane // NA_HEAD_DIM == h) for h in range(NA_HEADS)]


def _na_kernel(grid_rows, q_ref, kp_ref, kc_ref, kn_ref, vp_ref, vc_ref, vn_ref, kx_ref, vx_ref, bias_ref, o_ref,
               kwin, vwin):
    j = pl.program_id(1)
    tq = q_ref.shape[0]
    nkeys = NA_WIN_H * GRID_W
    kwin[0:tq] = kp_ref[...]
    kwin[tq:2 * tq] = kc_ref[...]
    kwin[2 * tq:3 * tq] = kn_ref[...]
    vwin[0:tq] = vp_ref[...]
    vwin[tq:2 * tq] = vc_ref[...]
    vwin[2 * tq:3 * tq] = vn_ref[...]
    kx = kx_ref[...]
    vx = vx_ref[...]
    hq = NA_HEADS * GRID_W
    own = (lax.broadcasted_iota(jnp.int32, (hq, NA_WIDTH), 0) // GRID_W
           == lax.broadcasted_iota(jnp.int32, (hq, NA_WIDTH), 1) // NA_HEAD_DIM)
    for g in range(NA_ROWS_PER_STEP // NA_ROW_GROUP):
        rls = [g * NA_ROW_GROUP + t for t in range(NA_ROW_GROUP)]
        scores = []
        for rl in rls:
            r = j * NA_ROWS_PER_STEP + rl
            r0 = jnp.clip(r - NA_WIN_H // 2, 0, grid_rows - NA_WIN_H)
            start = pl.multiple_of((r0 - (j - 1) * NA_ROWS_PER_STEP) * GRID_W, GRID_W)
            q = q_ref[rl * GRID_W:(rl + 1) * GRID_W, :]
            qs = jnp.where(own, jnp.concatenate([q] * NA_HEADS, axis=0), jnp.zeros((hq, NA_WIDTH), q.dtype))
            s_loc = _dot_nt(qs, kwin[pl.ds(start, nkeys), :]) + bias_ref[r - r0]
            scores.append((s_loc, _dot_nt(qs, kx), start))
        probs = []
        for s_loc, s_ctx, start in scores:
            m = jnp.maximum(jnp.max(s_loc, axis=-1, keepdims=True), jnp.max(s_ctx, axis=-1, keepdims=True))
            p_loc = jnp.exp(s_loc - m)
            p_ctx = jnp.exp(s_ctx - m)
            denom = jnp.sum(p_loc, axis=-1, keepdims=True) + jnp.sum(p_ctx, axis=-1, keepdims=True)
            probs.append((p_loc.astype(BF16), p_ctx.astype(BF16), denom, start))
        for rl, (p_loc, p_ctx, denom, start) in zip(rls, probs):
            pv = _dot(p_loc, vwin[pl.ds(start, nkeys), :]) + _dot(p_ctx, vx)
            pv = jnp.where(own, pv / denom, 0.0)
            out = pv[0:GRID_W]
            for h in range(1, NA_HEADS):
                out = out + pv[h * GRID_W:(h + 1) * GRID_W]
            o_ref[rl * GRID_W:(rl + 1) * GRID_W, :] = out.astype(o_ref.dtype)


def na_attention(q, k, v, k_ctx, v_ctx, bias, *, batch):
    n = q.shape[0]
    seq = n // batch
    ctx_len = k_ctx.shape[0] // batch
    grid_rows = seq // GRID_W
    tq = NA_ROWS_PER_STEP * GRID_W
    nblk = seq // tq
    cur = lambda b, j: (b * nblk + j, 0)
    prev = lambda b, j: (b * nblk + jnp.maximum(j - 1, 0), 0)
    nxt = lambda b, j: (b * nblk + jnp.minimum(j + 1, nblk - 1), 0)
    blk = lambda f: pl.BlockSpec((tq, NA_WIDTH), f)
    ctx = pl.BlockSpec((ctx_len, NA_WIDTH), lambda b, j: (b, 0))
    return pl.pallas_call(
        functools.partial(_na_kernel, grid_rows),
        out_shape=jax.ShapeDtypeStruct((n, NA_WIDTH), BF16),
        grid=(batch, nblk),
        in_specs=[blk(cur), blk(prev), blk(cur), blk(nxt), blk(prev), blk(cur), blk(nxt), ctx, ctx,
                  pl.BlockSpec(bias.shape, lambda b, j: (0, 0, 0))],
        out_specs=blk(cur),
        scratch_shapes=[pltpu.VMEM((3 * tq, NA_WIDTH), BF16), pltpu.VMEM((3 * tq, NA_WIDTH), BF16)],
        compiler_params=_cparams("parallel", "parallel"),
        name="na_attention",
    )(q, k, k, k, v, v, v, k_ctx, v_ctx, bias)


def _ctx_attn_kernel(q_ref, k_ref, v_ref, o_ref):
    q = q_ref[...]
    k = k_ref[...]
    v = v_ref[...]
    masks = _head_masks()
    acc = jnp.zeros(q.shape, F32)
    for h in range(NA_HEADS):
        s = _dot_nt(jnp.where(masks[h], q, jnp.zeros_like(q)), k)
        p = jnp.exp(s - jnp.max(s, axis=-1, keepdims=True))
        pv = _dot(p.astype(BF16), v) / jnp.sum(p, axis=-1, keepdims=True)
        acc = acc + jnp.where(masks[h], pv, 0.0)
    o_ref[...] = acc.astype(o_ref.dtype)


def ctx_attention(q, k, v, *, batch):
    n = q.shape[0]
    blk = pl.BlockSpec((n // batch, NA_WIDTH), lambda b: (b, 0))
    return pl.pallas_call(
        _ctx_attn_kernel,
        out_shape=jax.ShapeDtypeStruct((n, NA_WIDTH), BF16),
        grid=(batch,),
        in_specs=[blk, blk, blk],
        out_specs=blk,
        compiler_params=_cparams("parallel"),
        name="ctx_attention",
    )(q, k, v)


FT_N1 = 128
FT_T2_BLOCK = 8
FT_K1_BLOCK = 8


def _dft_cos_sin(n, scale=1.0):
    ang = 2.0 * np.pi * np.outer(np.arange(n), np.arange(n)) / n
    return np.cos(ang) * scale, np.sin(ang) * scale


def _channel_dft():
    c, s = _dft_cos_sin(FT_GROUP_DIM)
    eye = np.eye(FT_GROUPS)
    return np.concatenate([np.kron(eye, c), -np.kron(eye, s)], axis=1)


def _ft_stage1_kernel(u_ref, cs_ref, ff_ref, yr_ref, yi_ref):
    n1 = u_ref.shape[1]
    for s in range(u_ref.shape[2] // FT_WIDTH):
        lanes = slice(s * FT_WIDTH, (s + 1) * FT_WIDTH)
        z = _dot(u_ref[0, :, lanes], cs_ref[...])
        zz = jnp.concatenate([z[:, 0:FT_WIDTH], z[:, FT_WIDTH:]], axis=0).astype(BF16)
        y = _dot(ff_ref[...], zz)
        yr_ref[0, :, s, :] = y[0:n1]
        yi_ref[0, :, s, :] = y[n1:]


def _ft_stage2_kernel(yr_ref, yi_ref, twr_ref, twi_ref, g_ref, w_ref, o_ref):
    for i in range(yr_ref.shape[1]):
        yr = yr_ref[0, i]
        yi = yi_ref[0, i]
        tr = twr_ref[0, :, i:i + 1]
        ti = twi_ref[0, :, i:i + 1]
        yy = jnp.concatenate([yr * tr - yi * ti, yr * ti + yi * tr], axis=0).astype(BF16)
        xr = _dot(g_ref[...], yy)
        o_ref[0, :, i, :] = _dot(xr.astype(BF16), w_ref[...]).astype(o_ref.dtype)


def fourier_mix(u, fno_w, *, batch):
    n = u.shape[0]
    seq = n // batch
    n1 = FT_N1
    n2 = seq // n1
    tb = min(FT_T2_BLOCK, n2)
    norm = 1.0 / math.sqrt(seq * FT_GROUP_DIM)
    c1, s1 = _dft_cos_sin(n1)
    ff = jnp.asarray(np.block([[c1, s1], [-s1, c1]]), BF16)
    yr, yi = pl.pallas_call(
        _ft_stage1_kernel,
        out_shape=[jax.ShapeDtypeStruct((batch, n1, n2, FT_WIDTH), F32)] * 2,
        grid=(batch, n2 // tb),
        in_specs=[pl.BlockSpec((1, n1, tb * FT_WIDTH), lambda b, j: (b, 0, j)),
                  pl.BlockSpec((FT_WIDTH, 2 * FT_WIDTH), lambda b, j: (0, 0)),
                  pl.BlockSpec((2 * n1, 2 * n1), lambda b, j: (0, 0))],
        out_specs=[pl.BlockSpec((1, n1, tb, FT_WIDTH), lambda b, j: (b, 0, j, 0))] * 2,
        compiler_params=_cparams("parallel", "parallel"),
        name="fourier_stage1",
    )(u.reshape(batch, n1, n2 * FT_WIDTH), jnp.asarray(_channel_dft(), BF16), ff)
    kb = FT_K1_BLOCK
    ang = 2.0 * np.pi * np.outer(np.arange(n1), np.arange(n2)) / seq
    tw = lambda f: jnp.asarray(f(ang).reshape(n1 // kb, kb, n2).transpose(0, 2, 1), F32)
    c2, s2 = _dft_cos_sin(n2, norm)
    g = jnp.asarray(np.concatenate([c2, s2], axis=1), BF16)
    out = pl.pallas_call(
        _ft_stage2_kernel,
        out_shape=jax.ShapeDtypeStruct((batch, n2, n1, FT_WIDTH), BF16),
        grid=(batch, n1 // kb),
        in_specs=[pl.BlockSpec((1, kb, n2, FT_WIDTH), lambda b, j: (b, j, 0, 0))] * 2 + [
            pl.BlockSpec((1, n2, kb), lambda b, j: (j, 0, 0))] * 2 + [
            pl.BlockSpec((n2, 2 * n2), lambda b, j: (0, 0)), pl.BlockSpec((FT_WIDTH, FT_WIDTH), lambda b, j: (0, 0))],
        out_specs=pl.BlockSpec((1, n2, kb, FT_WIDTH), lambda b, j: (b, 0, j, 0)),
        compiler_params=_cparams("parallel", "parallel"),
        name="fourier_stage2",
    )(yr, yi,
      tw(np.cos), tw(lambda a: -np.sin(a)), g, fno_w.astype(BF16))
    return out.reshape(n, FT_WIDTH)


def _ft_direct_kernel(u_ref, cs_ref, g_ref, w_ref, o_ref):
    z = _dot(u_ref[...], cs_ref[...])
    zz = jnp.concatenate([z[:, 0:FT_WIDTH], z[:, FT_WIDTH:]], axis=0).astype(BF16)
    xr = _dot(g_ref[...], zz)
    o_ref[...] = _dot(xr.astype(BF16), w_ref[...]).astype(o_ref.dtype)


def fourier_mix_direct(u, fno_w, *, batch):
    n = u.shape[0]
    seq = n // batch
    c, s = _dft_cos_sin(seq, 1.0 / math.sqrt(seq * FT_GROUP_DIM))
    g = jnp.asarray(np.concatenate([c, s], axis=1), BF16)
    return pl.pallas_call(
        _ft_direct_kernel,
        out_shape=jax.ShapeDtypeStruct((n, FT_WIDTH), BF16),
        grid=(batch,),
        in_specs=[pl.BlockSpec((seq, FT_WIDTH), lambda b: (b, 0)),
                  pl.BlockSpec((FT_WIDTH, 2 * FT_WIDTH), lambda b: (0, 0)),
                  pl.BlockSpec((seq, 2 * seq), lambda b: (0, 0)),
                  pl.BlockSpec((FT_WIDTH, FT_WIDTH), lambda b: (0, 0))],
        out_specs=pl.BlockSpec((seq, FT_WIDTH), lambda b: (b, 0)),
        compiler_params=_cparams("parallel"),
        name="fourier_direct",
    )(u, jnp.asarray(_channel_dft(), BF16), g, fno_w.astype(BF16))


def _mod_row(mod_ref, rows_per_mod, fixed_row, tm):
    row = fixed_row if fixed_row is not None else (pl.program_id(0) * tm) // rows_per_mod
    return mod_ref[pl.ds(row, 1), :]


def _mix_out_kernel(rows_per_mod, fixed_row, route, o_ref, z_ref, ob_ref, oc_ref, x_ref, mod_ref, nw_ref, ones_ref,
                    w_ref, g_ref, *rest):
    if route:
        wr_ref, x_out, h_out, e_out, p_out = rest
    else:
        x_out, h_out = rest
    tm = x_ref.shape[0]
    m = _mod_row(mod_ref, rows_per_mod, fixed_row, tm)
    for rows in _row_parts(tm):
        o = o_ref[0, rows, :].astype(F32) + o_ref[1, rows, :].astype(F32)
        oa = _head_rms(o, ones_ref[...], DN_HEAD_DIM, nw_ref[...]) * _silu(z_ref[rows, :].astype(F32))
        y = (_dot(oa.astype(BF16), w_ref[0:DN_WIDTH, :])
             + _dot(ob_ref[rows, :], w_ref[DN_WIDTH:DN_WIDTH + NA_WIDTH, :])
             + _dot(oc_ref[rows, :], w_ref[DN_WIDTH + NA_WIDTH:, :]))
        x = x_ref[rows, :] + m[:, 2 * D_MODEL:3 * D_MODEL] * y
        x_out[rows, :] = x
        h = _modulated_norm(x, g_ref[...], m[:, 3 * D_MODEL:4 * D_MODEL], m[:, 4 * D_MODEL:5 * D_MODEL])
        h_out[rows, :] = h.astype(h_out.dtype)
        if route:
            h_hi = h.astype(BF16)
            h_lo = (h - h_hi.astype(F32)).astype(BF16)
            both = _dot_nt(wr_ref[...], h_hi) + _dot_nt(wr_ref[...], h_lo)
            logits = both[0:N_EXPERTS] + both[N_EXPERTS:]
            idx = lax.broadcasted_iota(jnp.int32, logits.shape, 0)
            m1 = jnp.max(logits, axis=0, keepdims=True)
            e1 = jnp.min(jnp.where(logits == m1, idx, N_EXPERTS), axis=0, keepdims=True)
            rest_l = jnp.where(idx == e1, -jnp.inf, logits)
            m2 = jnp.max(rest_l, axis=0, keepdims=True)
            e2 = jnp.min(jnp.where(rest_l == m2, idx, N_EXPERTS), axis=0, keepdims=True)
            t = jnp.exp(m2 - m1)
            e_out[:, rows] = jnp.concatenate([e1, e2], axis=0)
            p_out[:, rows] = jnp.concatenate([1.0 / (1.0 + t), t / (1.0 + t)], axis=0)


def mix_out(o_dirs, z, ob, oc, x2d, mod, dn_norm_w, w_out, g_ffn, w_router=None, *, rows_per_mod, fixed_row=None,
            tm=512):
    n, d = x2d.shape
    tm = min(tm, n)
    route = w_router is not None
    full = lambda shape: pl.BlockSpec(shape, lambda i: (0,) * len(shape))
    rowblk = lambda w: pl.BlockSpec((tm, w), lambda i: (i, 0))
    ins = [o_dirs, z, ob, oc, x2d, mod, jnp.tile(dn_norm_w, DN_HEADS).reshape(1, DN_WIDTH),
           _group_ones(DN_WIDTH, DN_HEAD_DIM), w_out, g_ffn.reshape(1, d)]
    in_specs = [pl.BlockSpec((DN_DIRS, tm, DN_WIDTH), lambda i: (0, i, 0)), rowblk(DN_WIDTH), rowblk(NA_WIDTH),
                rowblk(FT_WIDTH), rowblk(d), full(mod.shape), full((1, DN_WIDTH)), full((DN_WIDTH, DN_WIDTH)),
                full(w_out.shape), full((1, d))]
    outs = [jax.ShapeDtypeStruct((n, d), F32), jax.ShapeDtypeStruct((n, d), F32 if route else BF16)]
    out_specs = [rowblk(d), rowblk(d)]
    if route:
        w_hi = w_router.astype(BF16)
        w_lo = (w_router - w_hi.astype(F32)).astype(BF16)
        ins.append(jnp.concatenate([w_hi, w_lo], axis=1).T)
        in_specs.append(full((2 * N_EXPERTS, d)))
        outs += [jax.ShapeDtypeStruct((TOP_K, n), jnp.int32), jax.ShapeDtypeStruct((TOP_K, n), F32)]
        out_specs += [pl.BlockSpec((TOP_K, tm), lambda i: (0, i))] * 2
    res = pl.pallas_call(
        functools.partial(_mix_out_kernel, rows_per_mod, fixed_row, route),
        out_shape=outs,
        grid=(n // tm,),
        in_specs=in_specs,
        out_specs=out_specs,
        compiler_params=_cparams("parallel"),
        name="mix_out_route" if route else "mix_out",
    )(*ins)
    if route:
        x_new, h_new, top_e, top_p = res
        return x_new, h_new, top_e.T, top_p.T
    return res


def _ffn_kernel(rows_per_mod, fixed_row, tf, h_ref, x_ref, mod_ref, w1_ref, w3_ref, w2_ref, o_ref):
    h = h_ref[...]
    gated = [(_silu(_dot(h, w1_ref[:, c:c + tf])) * _dot(h, w3_ref[:, c:c + tf])).astype(BF16)
             for c in range(0, w1_ref.shape[1], tf)]
    y = _dot(jnp.concatenate(gated, axis=1), w2_ref[...])
    m = _mod_row(mod_ref, rows_per_mod, fixed_row, h_ref.shape[0])
    o_ref[...] = x_ref[...] + m[:, 5 * D_MODEL:6 * D_MODEL] * y


def dense_ffn(h, x2d, mod, w1, w3, w2, *, rows_per_mod, fixed_row=None, tm=512, tf=1408):
    n, d = x2d.shape
    tm = min(tm, n)
    dff = w1.shape[1]
    resident = lambda shape: pl.BlockSpec(shape, lambda i: (0, 0), pipeline_mode=pl.Buffered(1))
    return pl.pallas_call(
        functools.partial(_ffn_kernel, rows_per_mod, fixed_row, tf),
        out_shape=jax.ShapeDtypeStruct((n, d), F32),
        grid=(n // tm,),
        in_specs=[pl.BlockSpec((tm, d), lambda i: (i, 0)), pl.BlockSpec((tm, d), lambda i: (i, 0)),
                  pl.BlockSpec(mod.shape, lambda i: (0, 0)),
                  resident((d, dff)), resident((d, dff)), resident((dff, d))],
        out_specs=pl.BlockSpec((tm, d), lambda i: (i, 0)),
        compiler_params=_cparams("parallel"),
        name="dense_ffn",
    )(h, x2d, mod, w1, w3, w2)


MOE_ROWS = 512
MOE_DMA_ROWS = 256
MOE_VMEM_LIMIT = 56 * 1024 * 1024


def _row_copy(src, src_row, dst, dst_row, sem):
    return pltpu.make_async_copy(src.at[pl.ds(src_row, 1)], dst.at[pl.ds(dst_row, 1)], sem)


def _moe_scatter_kernel(dest_ref, h_ref, xb_in, xb_out, sem):
    del xb_in
    tm = h_ref.shape[0]

    def start(t, carry):
        for j in range(TOP_K):
            _row_copy(h_ref, t, xb_out, dest_ref[0, 0, TOP_K * t + j], sem).start(priority=j % 2)
        return carry

    def wait(t, carry):
        for j in range(TOP_K):
            _row_copy(h_ref, 0, xb_out, 0, sem).wait()
        return carry

    lax.fori_loop(0, tm, start, 0, unroll=8)
    lax.fori_loop(0, tm, wait, 0, unroll=8)


def _moe_ffn_kernel(tf, be_ref, na_ref, x_ref, w1_ref, w3_ref, w2_ref, y_ref):
    active = pl.program_id(0) < na_ref[0]

    @pl.when(active)
    def _():
        h = x_ref[...].astype(BF16)
        gated = [(_silu(_dot(h, w1_ref[0, :, c:c + tf])) * _dot(h, w3_ref[0, :, c:c + tf])).astype(BF16)
                 for c in range(0, w1_ref.shape[2], tf)]
        y_ref[...] = _dot(jnp.concatenate(gated, axis=1), w2_ref[0])

    @pl.when(jnp.logical_not(active))
    def _():
        y_ref[...] = jnp.zeros_like(y_ref)


def _moe_combine_kernel(rows_per_mod, dest_ref, p_ref, x_ref, mod_ref, yb_hbm, o_ref, buf, sem):
    tm = x_ref.shape[0]

    def start(t, carry):
        for j in range(TOP_K):
            _row_copy(yb_hbm, dest_ref[0, 0, TOP_K * t + j], buf.at[j], t, sem).start(priority=j % 2)
        return carry

    def wait(t, carry):
        for j in range(TOP_K):
            _row_copy(yb_hbm, 0, buf.at[j], 0, sem).wait()
        return carry

    lax.fori_loop(0, tm, start, 0, unroll=8)
    lax.fori_loop(0, tm, wait, 0, unroll=8)
    p = p_ref[...]
    y = p[:, 0:1] * buf[0] + p[:, 1:2] * buf[1]
    m = _mod_row(mod_ref, rows_per_mod, None, tm)
    o_ref[...] = x_ref[...] + m[:, 5 * D_MODEL:6 * D_MODEL] * y


def moe_ffn(h, x2d, mod, top_e, top_p, w1, w3, w2, *, rows_per_mod, tf=1792):
    n, d = x2d.shape
    dff = w1.shape[2]
    n_assign = n * TOP_K
    n_blocks = -(-n_assign // MOE_ROWS) + N_EXPERTS
    n_slots = n_blocks * MOE_ROWS
    e_flat = top_e.reshape(n_assign)
    onehot = (e_flat[:, None] == jnp.arange(N_EXPERTS, dtype=jnp.int32)[None, :]).astype(jnp.int32)
    csum = jnp.cumsum(onehot, axis=0)
    rank = jnp.sum(onehot * csum, axis=1) - 1
    counts = csum[-1]
    padded = (counts + MOE_ROWS - 1) // MOE_ROWS * MOE_ROWS
    pad_end = jnp.cumsum(padded)
    dest = (jnp.sum(onehot * (pad_end - padded)[None, :], axis=1) + rank).astype(jnp.int32)
    n_active = (pad_end[-1] // MOE_ROWS).astype(jnp.int32).reshape(1)
    blk_ids = jnp.minimum(jnp.arange(n_blocks, dtype=jnp.int32), n_active[0] - 1)
    block_e = jnp.minimum(jnp.searchsorted(pad_end, blk_ids * MOE_ROWS, side='right'), N_EXPERTS - 1).astype(jnp.int32)

    tmd = min(MOE_DMA_ROWS, n)
    dest2d = dest.reshape(n // tmd, 1, TOP_K * tmd)
    dest_spec = pl.BlockSpec((1, 1, TOP_K * tmd), lambda i: (i, 0, 0), memory_space=pltpu.SMEM)
    hbm = pl.BlockSpec(memory_space=pl.ANY)
    xb = pl.pallas_call(
        _moe_scatter_kernel,
        out_shape=jax.ShapeDtypeStruct((n_slots, d), F32),
        grid=(n // tmd,),
        in_specs=[dest_spec, pl.BlockSpec((tmd, d), lambda i: (i, 0)), hbm],
        out_specs=hbm,
        scratch_shapes=[pltpu.SemaphoreType.DMA],
        input_output_aliases={2: 0},
        compiler_params=_cparams("arbitrary"),
        name="moe_scatter",
    )(dest2d, h, jnp.zeros((n_slots, d), F32))

    expert_w = lambda shape: pl.BlockSpec(shape, lambda i, be, na: (be[i], 0, 0), pipeline_mode=pl.Buffered(1))
    yb = pl.pallas_call(
        functools.partial(_moe_ffn_kernel, tf),
        out_shape=jax.ShapeDtypeStruct((n_slots, d), F32),
        grid_spec=pltpu.PrefetchScalarGridSpec(
            num_scalar_prefetch=2,
            grid=(n_blocks,),
            in_specs=[pl.BlockSpec((MOE_ROWS, d), lambda i, be, na: (i, 0)),
                      expert_w((1, d, dff)), expert_w((1, d, dff)), expert_w((1, dff, d))],
            out_specs=pl.BlockSpec((MOE_ROWS, d), lambda i, be, na: (i, 0))),
        compiler_params=pltpu.CompilerParams(dimension_semantics=("arbitrary",), vmem_limit_bytes=MOE_VMEM_LIMIT),
        name="moe_expert_ffn",
    )(block_e, n_active, xb, w1, w3, w2)

    return pl.pallas_call(
        functools.partial(_moe_combine_kernel, rows_per_mod),
        out_shape=jax.ShapeDtypeStruct((n, d), F32),
        grid=(n // tmd,),
        in_specs=[dest_spec, pl.BlockSpec((tmd, TOP_K), lambda i: (i, 0)), pl.BlockSpec((tmd, d), lambda i: (i, 0)),
                  pl.BlockSpec(mod.shape, lambda i: (0, 0)), hbm],
        out_specs=pl.BlockSpec((tmd, d), lambda i: (i, 0)),
        scratch_shapes=[pltpu.VMEM((TOP_K, tmd, d), F32), pltpu.SemaphoreType.DMA],
        compiler_params=_cparams("arbitrary"),
        name="moe_combine",
    )(dest2d, top_p, x2d, mod, yb)


def _split_w_in(w_in_l):
    g0 = 4 * DN_WIDTH
    g1 = g0 + 2 * N_DH
    w_main = jnp.concatenate([w_in_l[:, :g0], w_in_l[:, g1:]], axis=1).astype(BF16)
    w_ab = w_in_l[:, g0:g1].astype(BF16)
    return w_main, w_ab, w_ab.T


def kernel(x, c, ctx, c_ctx, w_mod, b_mod, g_mix, g_ffn, w_in, dn_conv, dn_a_log, dn_dt_bias, dn_norm_w, na_q_norm,
           na_k_norm, na_rpb, fno_w, w_out, ffn_w1, ffn_w3, ffn_w2, moe_router, moe_w1, moe_w3, moe_w2):
    bsz, seq, d = x.shape
    ctx_len = ctx.shape[1]
    ctx_row = bsz
    cond = jnp.zeros((SUBLANES, d), F32).at[:bsz].set(c).at[ctx_row].set(c_ctx)
    mods = adaln_mod(cond, w_mod, b_mod)
    x2 = x.reshape(bsz * seq, d)
    c2 = ctx.reshape(bsz * ctx_len, d)
    zero_state = jnp.zeros((bsz, DN_DIRS, DN_HEADS, DN_HEAD_DIM, DN_HEAD_DIM), F32)

    for layer in range(DEPTH):
        need_ctx = layer < DEPTH - 1
        mod = mods[layer]
        w_main, w_ab, w_abt = _split_w_in(w_in[layer])
        proj = functools.partial(in_proj, mod=mod, g=g_mix[layer], w_main=w_main, w_ab=w_ab, w_abt=w_abt,
                                 q_norm=na_q_norm[layer], k_norm=na_k_norm[layer])
        qkv, z, ab, abt, nq, nk, nv, ft = proj(x2, rows_per_mod=seq)
        qkv_c, z_c, ab_c, abt_c, nq_c, nk_c, nv_c, ft_c = proj(c2, rows_per_mod=ctx_len, fixed_row=ctx_row)

        prep = functools.partial(dn_prep, conv_w=dn_conv[layer], a_log=dn_a_log[layer], dt_bias=dn_dt_bias[layer],
                                 batch=bsz)
        q_c, k_c, v_c, gcc_c, beta_c, gcr_c = prep(qkv_c, ab_c, abt_c)
        o_c, s_ctx = dn_scan(q_c, k_c, v_c, gcc_c, beta_c, gcr_c, zero_state, batch=bsz)
        q_l, k_l, v_l, gcc_l, beta_l, gcr_l = prep(qkv, ab, abt)
        o_l, _ = dn_scan(q_l, k_l, v_l, gcc_l, beta_l, gcr_l, s_ctx, batch=bsz)

        ob = na_attention(nq, nk, nv, nk_c, nv_c, na_bias_table(na_rpb[layer]), batch=bsz)
        oc = fourier_mix(ft, fno_w[layer], batch=bsz)

        w_out_l = w_out[layer].astype(BF16)
        j = layer // 2
        if layer % 2 == 0:
            x2, h2 = mix_out(o_l, z, ob, oc, x2, mod, dn_norm_w[layer], w_out_l, g_ffn[layer], rows_per_mod=seq)
            ffn_w = (ffn_w1[j].astype(BF16), ffn_w3[j].astype(BF16), ffn_w2[j].astype(BF16))
            x2 = dense_ffn(h2, x2, mod, *ffn_w, rows_per_mod=seq)
        else:
            x2, h2, top_e, top_p = mix_out(o_l, z, ob, oc, x2, mod, dn_norm_w[layer], w_out_l, g_ffn[layer],
                                           moe_router[j], rows_per_mod=seq)
            x2 = moe_ffn(h2, x2, mod, top_e, top_p, moe_w1[j].astype(BF16), moe_w3[j].astype(BF16),
                         moe_w2[j].astype(BF16), rows_per_mod=seq)
        if need_ctx:
            ob_c = ctx_attention(nq_c, nk_c, nv_c, batch=bsz)
            oc_c = fourier_mix_direct(ft_c, fno_w[layer], batch=bsz)
            assert layer % 2 == 0, "context tokens only pass through dense layers at this depth"
            c2, hc2 = mix_out(o_c, z_c, ob_c, oc_c, c2, mod, dn_norm_w[layer], w_out_l, g_ffn[layer],
                              rows_per_mod=ctx_len, fixed_row=ctx_row)
            c2 = dense_ffn(hc2, c2, mod, *ffn_w, rows_per_mod=ctx_len, fixed_row=ctx_row)
    return x2.reshape(bsz, seq, d)
```
